```python
import math
import jax
import jax.numpy as jnp
from jax import lax
import numpy as np

D_MODEL = 1024
BATCH = 8
SEQ = 4096
DEPTH = 2
DEC_BATCH = 32
DEC_SEQ = 4
PAST_LEN = 16384
PAGE_SIZE = 128

N_MIXERS = 2
N_RET_LAYERS = (DEPTH + 1) // 2
N_ATT_LAYERS = DEPTH // 2

RET_HEADS = D_MODEL // 256
RET_DK = D_MODEL // RET_HEADS
RET_DV = 2 * RET_DK
RET_CHUNK = 128
RET_IN = RET_HEADS * (2 * RET_DK + 2 * RET_DV)
ROPE_BASE = 10000.0

ATT_GROUPS = ((128, 1), (512, 4), (2048, 16))
ATT_GROUP_HEADS = 8
ATT_HD = D_MODEL // ATT_GROUP_HEADS
ATT_HEADS = len(ATT_GROUPS) * ATT_GROUP_HEADS
ATT_IN = 3 * ATT_HEADS * ATT_HD
N_BUCKETS = 32
MAX_DISTANCE = 2048

MOE_GROUPS = 4
MOE_PER_GROUP = 4
MOE_EXPERTS = MOE_GROUPS * MOE_PER_GROUP
MOE_TOPK = 2
MOE_HIDDEN = 512

EPS = 1e-6

kernel_name = 'retention_dilated_attention_hmoe_step'


def rmsnorm(x, gain):
    x32 = x.astype(jnp.float32)
    return x32 * lax.rsqrt(jnp.mean(x32 * x32, axis=-1, keepdims=True) + EPS) * gain.astype(jnp.float32)


def rotary(x, pos):
    half = x.shape[-1] // 2
    inv = 1.0 / (ROPE_BASE ** jnp.linspace(0.0, 1.0, half, dtype=jnp.float32))
    ang = jnp.repeat(pos.astype(jnp.float32)[:, None] * inv[None, :], 2, axis=-1)[:, None, :]
    x_even, x_odd = x[..., 0::2], x[..., 1::2]
    rot = jnp.stack([-x_odd, x_even], axis=-1).reshape(x.shape)
    return x * jnp.cos(ang) + rot * jnp.sin(ang)


def retention_chunked(q, k, v, s0, chunk):
    b, L, h, _ = q.shape
    n = L // chunk
    log_g = jnp.log(1.0 - 2.0 ** (-5.0 - jnp.arange(h, dtype=jnp.float32)))
    idx = jnp.arange(chunk, dtype=jnp.float32)
    diff = idx[:, None] - idx[None, :]
    intra = jnp.where(diff >= 0, jnp.exp(jnp.maximum(diff, 0.0)[None] * log_g[:, None, None]), 0.0)
    q_dec = jnp.exp((idx + 1.0)[:, None] * log_g[None, :])[..., None]
    k_dec = jnp.exp((chunk - 1.0 - idx)[:, None] * log_g[None, :])[..., None]
    c_dec = jnp.exp(chunk * log_g)[:, None, None]

    def to_chunks(t):
        return jnp.swapaxes(t.reshape(b, n, chunk, h, t.shape[-1]), 0, 1)

    def step(s, inp):
        qc, kc, vc = inp
        sc = jnp.einsum('bihd,bjhd->bhij', qc, kc) * intra
        o = jnp.einsum('bhij,bjhv->bihv', sc, vc) + jnp.einsum('bihd,bhdv->bihv', qc * q_dec, s)
        s = s * c_dec + jnp.einsum('bjhd,bjhv->bhdv', kc * k_dec, vc)
        return s, o

    s, o = lax.scan(step, s0, (to_chunks(q), to_chunks(k), to_chunks(v)))
    return jnp.swapaxes(o, 0, 1).reshape(b, L, h, v.shape[-1]), s


def retention_mixer(x, s0, pos, chunk, w_in, w_out):
    b, L, _ = x.shape
    qk = RET_HEADS * RET_DK
    vw = RET_HEADS * RET_DV
    proj = jnp.einsum('bld,de->ble', x, w_in).astype(jnp.float32)
    q = rotary(proj[..., :qk].reshape(b, L, RET_HEADS, RET_DK), pos)
    k = rotary(proj[..., qk:2 * qk].reshape(b, L, RET_HEADS, RET_DK), pos) * RET_DK ** -0.5
    v = proj[..., 2 * qk:2 * qk + vw].reshape(b, L, RET_HEADS, RET_DV)
    gate = proj[..., 2 * qk + vw:]
    o, s = retention_chunked(q, k, v, s0, chunk)
    o = o * lax.rsqrt(jnp.mean(o * o, axis=-1, keepdims=True) + EPS)
    o = o.reshape(b, L, vw) * jax.nn.silu(gate)
    return jnp.einsum('ble,ed->bld', o.astype(x.dtype), w_out), s


def t5_bucket(dist):
    max_exact = N_BUCKETS // 2
    d32 = jnp.maximum(dist, 1).astype(jnp.float32)
    large = max_exact + (jnp.log(d32 / max_exact) / math.log(MAX_DISTANCE / max_exact)
                         * (N_BUCKETS - max_exact)).astype(jnp.int32)
    return jnp.where(dist < max_exact, dist, jnp.minimum(large, N_BUCKETS - 1))


def project_att(x, w_in, q_gain, k_gain):
    b, L, _ = x.shape
    proj = jnp.einsum('bld,de->ble', x, w_in).astype(jnp.float32)
    proj = proj.reshape(b, L, 3, len(ATT_GROUPS), ATT_GROUP_HEADS, ATT_HD)
    q = rmsnorm(proj[:, :, 0], q_gain) * ATT_HD ** -0.5
    k = rmsnorm(proj[:, :, 1], k_gain)
    return q, k, proj[:, :, 2]


def dilated_group_prompt(q, k, v, bias_tab, dil, span):
    b, L, G, hd = q.shape
    ls = L // dil
    nb = -(-ls // span)
    lp = nb * span

    def by_residue(t):
        t = jnp.swapaxes(t.reshape(b, ls, dil, G, hd), 1, 2)
        return jnp.pad(t, ((0, 0), (0, 0), (0, lp - ls), (0, 0), (0, 0)))

    def band(t):
        prev = jnp.pad(t, ((0, 0), (0, 0), (span, 0), (0, 0), (0, 0)))[:, :, :lp]
        prev = prev.reshape(b, dil, nb, span, G, hd)
        return jnp.concatenate([prev, t.reshape(b, dil, nb, span, G, hd)], axis=3)

    qb = by_residue(q).reshape(b, dil, nb, span, G, hd)
    kb, vb = band(by_residue(k)), band(by_residue(v))
    a = jnp.arange(span)[:, None]
    c = jnp.arange(2 * span)[None, :]
    rel = a - c + span
    key_sub = jnp.arange(nb)[:, None, None] * span - span + c[None]
    valid = ((rel >= 0) & (rel <= span))[None] & (key_sub >= 0)
    bias = jnp.moveaxis(bias_tab[t5_bucket(jnp.clip(rel, 0, span) * dil)], -1, 0)
    s = jnp.einsum('brnigd,brnjgd->brngij', qb, kb) + bias
    s = jnp.where(valid[:, None], s, -jnp.inf)
    m = jnp.max(s, axis=-1, keepdims=True)
    p = jnp.exp(s - m)
    den = jnp.sum(p, axis=-1)
    o = jnp.einsum('brngij,brnjgd->brnigd', p, vb) / jnp.swapaxes(den, -1, -2)[..., None]
    lse = jnp.swapaxes(m[..., 0] + jnp.log(den), -1, -2)

    def back(t):
        t = t.reshape((b, dil, lp) + t.shape[4:])[:, :, :ls]
        return jnp.swapaxes(t, 1, 2).reshape((b, L) + t.shape[3:])

    return back(o), back(lse)


def dilated_group_sample(q, k, v, buf, bias_tab, dil, span):
    T = q.shape[1]
    lb = buf.shape[1]
    kc = jnp.concatenate([buf[:, :, 0], k], axis=1)
    vc = jnp.concatenate([buf[:, :, 1], v], axis=1)
    j = jnp.arange(span + 1)
    idx = lb + jnp.arange(T)[:, None] - j[None, :] * dil
    valid = idx >= 0
    idx = jnp.maximum(idx, 0)
    kg, vg = kc[:, idx], vc[:, idx]
    bias = bias_tab[t5_bucket(j * dil)].T
    s = jnp.einsum('btgd,btjgd->btgj', q, kg) + bias
    s = jnp.where(valid[:, None, :], s, -jnp.inf)
    m = jnp.max(s, axis=-1, keepdims=True)
    p = jnp.exp(s - m)
    den = jnp.sum(p, axis=-1)
    o = jnp.einsum('btgj,btjgd->btgd', p, vg) / den[..., None]
    lse = m[..., 0] + jnp.log(den)
    new_buf = jnp.concatenate([buf, jnp.stack([k, v], axis=2)], axis=1)[:, T:]
    return o, lse, new_buf


def merge_groups(outs, lses, w_out, dtype):
    o = jnp.stack(outs)
    w = jax.nn.softmax(jnp.stack(lses), axis=0)
    o = jnp.sum(o * w[..., None], axis=0)
    b, L = o.shape[:2]
    return jnp.einsum('ble,ed->bld', o.reshape(b, L, -1).astype(dtype), w_out)


def dilated_attention_prompt(x, rel_bias, w_in, q_gain, k_gain, w_out):
    q, k, v = project_att(x, w_in, q_gain, k_gain)
    L = x.shape[1]
    outs, lses, bufs = [], [], []
    for g, (win, dil) in enumerate(ATT_GROUPS):
        tab = rel_bias[:, g * ATT_GROUP_HEADS:(g + 1) * ATT_GROUP_HEADS].astype(jnp.float32)
        o, l = dilated_group_prompt(q[:, :, g], k[:, :, g], v[:, :, g], tab, dil, win // dil)
        outs.append(o)
        lses.append(l)
        lw = min(win, L)
        bufs.append(jnp.stack([k[:, L - lw:, g], v[:, L - lw:, g]], axis=2))
    return merge_groups(outs, lses, w_out, x.dtype), bufs


def dilated_attention_sample(x, bufs_in, rel_bias, w_in, q_gain, k_gain, w_out):
    q, k, v = project_att(x, w_in, q_gain, k_gain)
    outs, lses, bufs = [], [], []
    for g, (win, dil) in enumerate(ATT_GROUPS):
        tab = rel_bias[:, g * ATT_GROUP_HEADS:(g + 1) * ATT_GROUP_HEADS].astype(jnp.float32)
        o, l, nbuf = dilated_group_sample(q[:, :, g], k[:, :, g], v[:, :, g],
                                          bufs_in[g].astype(jnp.float32), tab, dil, win // dil)
        outs.append(o)
        lses.append(l)
        bufs.append(nbuf)
    return merge_groups(outs, lses, w_out, x.dtype), bufs


def hier_moe(x, w_rg, b_rg, w_re, b_re, w_gate, w_up, w_down):
    b, L, d = x.shape
    xt = x.reshape(b * L, d)
    n = xt.shape[0]
    rows = jnp.arange(n)
    lg = jnp.einsum('nd,dg->ng', xt, w_rg).astype(jnp.float32) + b_rg.astype(jnp.float32)
    g_sel = jnp.argmax(lg, axis=-1)
    p_g = jax.nn.softmax(lg, axis=-1)[rows, g_sel][:, None]
    le = jnp.einsum('nd,de->ne', xt, w_re).astype(jnp.float32) + b_re.astype(jnp.float32)
    le = le.reshape(n, MOE_GROUPS, MOE_PER_GROUP)[rows, g_sel]
    top_v, top_i = lax.top_k(le, MOE_TOPK)
    wts = jax.nn.softmax(top_v, axis=-1) * p_g
    ids = g_sel[:, None] * MOE_PER_GROUP + top_i
    combine = jnp.sum(jax.nn.one_hot(ids, MOE_EXPERTS, dtype=jnp.float32) * wts[..., None], axis=1)
    y = jnp.zeros((n, d), jnp.float32)
    for e in range(MOE_EXPERTS):
        h = jax.nn.silu(xt @ w_gate[e]) * (xt @ w_up[e])
        y = y + combine[:, e:e + 1] * (h @ w_down[e]).astype(jnp.float32)
    return y.reshape(b, L, d)


def setup_inputs(seed: int = 0) -> dict:
    key = jax.random.key(seed)
    ks = jax.random.split(key, 24)
    f = jnp.float32
    nrm = jax.random.normal
    g, hd = ATT_GROUP_HEADS, ATT_HD
    (w0, _), (w1, _), (w2, _) = ATT_GROUPS
    return {
        'x_prompt': nrm(ks[0], (BATCH, SEQ, D_MODEL), f),
        'x_sample': nrm(ks[1], (DEC_BATCH, DEC_SEQ, D_MODEL), f),
        'state_ret': 0.25 * nrm(ks[2], (N_RET_LAYERS, DEC_BATCH, RET_HEADS, RET_DK, RET_DV), f),
        'cache_win0': nrm(ks[3], (N_ATT_LAYERS, DEC_BATCH, min(w0, PAST_LEN), 2, g, hd), f),
        'cache_win1': nrm(ks[4], (N_ATT_LAYERS, DEC_BATCH, min(w1, PAST_LEN), 2, g, hd), f),
        'cache_win2': nrm(ks[5], (N_ATT_LAYERS, DEC_BATCH, min(w2, PAST_LEN), 2, g, hd), f),
        'rel_bias': 0.5 * nrm(ks[6], (N_BUCKETS, ATT_HEADS), f),
        'norm_mix': 1.0 + 0.02 * nrm(ks[7], (DEPTH, D_MODEL), f),
        'norm_ffn': 1.0 + 0.02 * nrm(ks[8], (DEPTH, D_MODEL), f),
        'w_in_ret': nrm(ks[9], (N_RET_LAYERS, D_MODEL, RET_IN), f) * D_MODEL ** -0.5,
        'w_out_ret': nrm(ks[10], (N_RET_LAYERS, RET_HEADS * RET_DV, D_MODEL), f) * (RET_HEADS * RET_DV) ** -0.5,
        'w_in_att': nrm(ks[11], (N_ATT_LAYERS, D_MODEL, ATT_IN), f) * D_MODEL ** -0.5,
        'q_norm': 1.0 + 0.02 * nrm(ks[12], (N_ATT_LAYERS, ATT_HD), f),
        'k_norm': 1.0 + 0.02 * nrm(ks[13], (N_ATT_LAYERS, ATT_HD), f),
        'w_out_att': nrm(ks[14], (N_ATT_LAYERS, g * hd, D_MODEL), f) * (g * hd) ** -0.5,
        'w_router_group': nrm(ks[15], (DEPTH, D_MODEL, MOE_GROUPS), f) * D_MODEL ** -0.5,
        'b_router_group': 0.01 * nrm(ks[16], (DEPTH, MOE_GROUPS), f),
        'w_router_expert': nrm(ks[17], (DEPTH, D_MODEL, MOE_EXPERTS), f) * D_MODEL ** -0.5,
        'b_router_expert': 0.01 * nrm(ks[18], (DEPTH, MOE_EXPERTS), f),
        'w_gate': nrm(ks[19], (DEPTH, MOE_EXPERTS, D_MODEL, MOE_HIDDEN), f) * D_MODEL ** -0.5,
        'w_up': nrm(ks[20], (DEPTH, MOE_EXPERTS, D_MODEL, MOE_HIDDEN), f) * D_MODEL ** -0.5,
        'w_down': nrm(ks[21], (DEPTH, MOE_EXPERTS, MOE_HIDDEN, D_MODEL), f) * MOE_HIDDEN ** -0.5,
    }


def reference(x_prompt, x_sample, state_ret, cache_win0, cache_win1, cache_win2, rel_bias,
              norm_mix, norm_ffn, w_in_ret, w_out_ret, w_in_att, q_norm, k_norm, w_out_att,
              w_router_group, b_router_group, w_router_expert, b_router_expert, w_gate, w_up, w_down):
    caches = (cache_win0, cache_win1, cache_win2)
    n_p, l_p = x_prompt.shape[:2]
    l_s = x_sample.shape[1]
    pos_p = jnp.arange(l_p)
    pos_s = PAST_LEN + jnp.arange(l_s)
    xp, xs = x_prompt, x_sample
    ret_p, ret_s = [], []
    win_p = [[] for _ in ATT_GROUPS]
    win_s = [[] for _ in ATT_GROUPS]
    for i in range(DEPTH):
        j = i // N_MIXERS
        hp = rmsnorm(xp, norm_mix[i]).astype(xp.dtype)
        hs = rmsnorm(xs, norm_mix[i]).astype(xs.dtype)
        if i % N_MIXERS == 0:
            s0 = jnp.zeros((n_p, RET_HEADS, RET_DK, RET_DV), jnp.float32)
            mp, sp = retention_mixer(hp, s0, pos_p, RET_CHUNK, w_in_ret[j], w_out_ret[j])
            ms, ss = retention_mixer(hs, state_ret[j].astype(jnp.float32), pos_s, l_s, w_in_ret[j], w_out_ret[j])
            ret_p.append(sp.astype(state_ret.dtype))
            ret_s.append(ss.astype(state_ret.dtype))
        else:
            mp, bp = dilated_attention_prompt(hp, rel_bias, w_in_att[j], q_norm[j], k_norm[j], w_out_att[j])
            ms, bs = dilated_attention_sample(hs, [c[j] for c in caches], rel_bias,
                                              w_in_att[j], q_norm[j], k_norm[j], w_out_att[j])
            for g in range(len(ATT_GROUPS)):
                win_p[g].append(bp[g].astype(caches[g].dtype))
                win_s[g].append(bs[g].astype(caches[g].dtype))
        xp = xp + mp.astype(xp.dtype)
        xs = xs + ms.astype(xs.dtype)
        xp = xp + hier_moe(rmsnorm(xp, norm_ffn[i]).astype(xp.dtype), w_router_group[i], b_router_group[i],
                           w_router_expert[i], b_router_expert[i], w_gate[i], w_up[i], w_down[i]).astype(xp.dtype)
        xs = xs + hier_moe(rmsnorm(xs, norm_ffn[i]).astype(xs.dtype), w_router_group[i], b_router_group[i],
                           w_router_expert[i], b_router_expert[i], w_gate[i], w_up[i], w_down[i]).astype(xs.dtype)
    return (xp, xs, jnp.stack(ret_p), jnp.stack(ret_s),
            jnp.stack(win_p[0]), jnp.stack(win_s[0]),
            jnp.stack(win_p[1]), jnp.stack(win_s[1]),
            jnp.stack(win_p[2]), jnp.stack(win_s[2]))
```

```python
import functools

import jax
import jax.numpy as jnp
from jax import lax
from jax.experimental import pallas as pl
from jax.experimental.pallas import tpu as pltpu

F32 = jnp.float32
BF16 = jnp.bfloat16

D_MODEL = 1024
PAST_LEN = 16384
RET_HEADS = 4
RET_DK = 256
RET_DV = 512
RET_QK = RET_HEADS * RET_DK
RET_VW = RET_HEADS * RET_DV
RET_IN = 2 * RET_QK + 2 * RET_VW
ROPE_BASE = 10000.0
ATT_GROUPS = ((128, 1), (512, 4), (2048, 16))
ATT_NG = len(ATT_GROUPS)
ATT_G = 8
ATT_HD = 128
ATT_W = ATT_G * ATT_HD
ATT_IN = 3 * ATT_NG * ATT_W
ATT_SPAN = 128
N_BUCKETS = 32
MAX_DISTANCE = 2048
MOE_GROUPS = 4
MOE_PER_GROUP = 4
MOE_EXPERTS = MOE_GROUPS * MOE_PER_GROUP
MOE_HIDDEN = 512
EPS = 1e-6
NEG = -1e30

LANES = 128
V7X_VMEM_LIMIT_BYTES = 56 * 1024 * 1024
ROW_TILE = 512
RET_CHUNK = 256
SAMPLE_PAD = 16
MOE_TILE = 512
OUT_LSE = ATT_W + LANES


def _params(*sem):
    return pltpu.CompilerParams(dimension_semantics=sem, vmem_limit_bytes=V7X_VMEM_LIMIT_BYTES)


def _resident(shape):
    return pl.BlockSpec(shape, lambda *_: (0,) * len(shape), pipeline_mode=pl.Buffered(1))


def _rmsnorm_rows(x, gain):
    return x * lax.rsqrt(jnp.mean(x * x, axis=-1, keepdims=True) + EPS) * gain


def _dot(a, b):
    return jnp.dot(a, b, preferred_element_type=F32)


def _dot_nt(a, b):
    return lax.dot_general(a, b, (((1,), (1,)), ((), ())), preferred_element_type=F32)


def _dot_tn(a, b):
    return lax.dot_general(a, b, (((0,), (0,)), ((), ())), preferred_element_type=F32)


def _ret_proj_kernel(x_ref, g_ref, w_ref, cos_ref, sin_ref, o_ref):
    xn = _rmsnorm_rows(x_ref[...], g_ref[...]).astype(BF16)
    cos = cos_ref[...]
    sin = sin_ref[...]
    even = (lax.broadcasted_iota(jnp.int32, cos.shape, 1) & 1) == 0
    for c in range(2 * RET_HEADS):
        sl = slice(c * RET_DK, (c + 1) * RET_DK)
        acc = _dot(xn, w_ref[:, sl])
        partner = jnp.where(even, pltpu.roll(acc, RET_DK - 1, 1), pltpu.roll(acc, 1, 1))
        r = acc * cos + partner * sin
        if c >= RET_HEADS:
            r = r * (RET_DK ** -0.5)
        o_ref[:, sl] = r.astype(o_ref.dtype)
    for c in range(2 * RET_QK // RET_DV, RET_IN // RET_DV):
        sl = slice(c * RET_DV, (c + 1) * RET_DV)
        o_ref[:, sl] = _dot(xn, w_ref[:, sl]).astype(o_ref.dtype)


def _ret_proj(x, row_off, rows, tm, gain, w, cos, sin):
    tab_blocks = cos.shape[0] // tm
    return pl.pallas_call(
        _ret_proj_kernel,
        out_shape=jax.ShapeDtypeStruct((rows, RET_IN), BF16),
        grid=(rows // tm,),
        in_specs=[
            pl.BlockSpec((tm, D_MODEL), lambda i: (i + row_off, 0)),
            _resident((1, D_MODEL)),
            _resident((D_MODEL, RET_IN)),
            pl.BlockSpec((tm, RET_DK), lambda i: (i % tab_blocks, 0)),
            pl.BlockSpec((tm, RET_DK), lambda i: (i % tab_blocks, 0)),
        ],
        out_specs=pl.BlockSpec((tm, RET_IN), lambda i: (i, 0)),
        compiler_params=_params("parallel"),
        name="ret_proj",
    )(x, gain, w, cos, sin)


def _ret_core_kernel(cdec_ref, q_ref, k_ref, v_ref, gt_ref, s0_ref, intra_ref, qdec_ref, kdec_ref, o_ref, st_ref):
    @pl.when(pl.program_id(1) == 0)
    def _():
        st_ref[...] = s0_ref[...]

    for h in range(RET_HEADS):
        qk = slice(h * RET_DK, (h + 1) * RET_DK)
        vv = slice(h * RET_DV, (h + 1) * RET_DV)
        qh, kh, vh = q_ref[:, qk], k_ref[:, qk], v_ref[:, vv]
        s = st_ref[0, h]
        sc = _dot_nt(qh, kh) * intra_ref[h]
        o = _dot(sc.astype(BF16), vh)
        o = o + _dot(qh, s.astype(BF16)) * jnp.concatenate([qdec_ref[h]] * (RET_DV // LANES), axis=1)
        kd = (kh.astype(F32) * jnp.concatenate([kdec_ref[h]] * (RET_DK // LANES), axis=1)).astype(BF16)
        st_ref[0, h] = s * cdec_ref[h] + _dot_tn(kd, vh)
        o = o * lax.rsqrt(jnp.mean(o * o, axis=-1, keepdims=True) + EPS)
        g = gt_ref[:, vv].astype(F32)
        o_ref[:, vv] = (o * (g * jax.nn.sigmoid(g))).astype(o_ref.dtype)


def _ret_decay_tables(chunk, n_real):
    log_g = jnp.log(1.0 - 2.0 ** (-5.0 - jnp.arange(RET_HEADS, dtype=F32)))
    idx = jnp.arange(chunk, dtype=F32)
    diff = idx[:, None] - idx[None, :]
    intra = jnp.where(diff >= 0, jnp.exp(jnp.maximum(diff, 0.0)[None] * log_g[:, None, None]), 0.0)
    q_dec = jnp.exp((idx + 1.0)[None, :] * log_g[:, None])
    k_dec = jnp.where(idx[None, :] < n_real, jnp.exp((n_real - 1.0 - idx)[None, :] * log_g[:, None]), 0.0)
    c_dec = jnp.exp(n_real * log_g)
    rep = lambda t: jnp.broadcast_to(t[:, :, None], (RET_HEADS, chunk, LANES))
    return c_dec, intra, rep(q_dec), rep(k_dec)


def _ret_core(proj, s0, nseq, nchunk, chunk, n_real):
    c_dec, intra, q_dec, k_dec = _ret_decay_tables(chunk, n_real)
    rows = nseq * nchunk * chunk
    row = lambda b, c: b * nchunk + c
    st_spec = pl.BlockSpec((1, RET_HEADS, RET_DK, RET_DV), lambda b, c: (b, 0, 0, 0))
    return pl.pallas_call(
        _ret_core_kernel,
        out_shape=(jax.ShapeDtypeStruct((rows, RET_VW), BF16),
                   jax.ShapeDtypeStruct((nseq, RET_HEADS, RET_DK, RET_DV), F32)),
        grid=(nseq, nchunk),
        in_specs=[
            pl.BlockSpec(memory_space=pltpu.SMEM),
            pl.BlockSpec((chunk, RET_QK), lambda b, c: (row(b, c), 0)),
            pl.BlockSpec((chunk, RET_QK), lambda b, c: (row(b, c), 1)),
            pl.BlockSpec((chunk, RET_VW), lambda b, c: (row(b, c), 1)),
            pl.BlockSpec((chunk, RET_VW), lambda b, c: (row(b, c), 2)),
            st_spec,
            _resident((RET_HEADS, chunk, chunk)),
            _resident((RET_HEADS, chunk, LANES)),
            _resident((RET_HEADS, chunk, LANES)),
        ],
        out_specs=(pl.BlockSpec((chunk, RET_VW), lambda b, c: (row(b, c), 0)), st_spec),
        compiler_params=_params("parallel", "arbitrary"),
        name="ret_core",
    )(c_dec, proj, proj, proj, proj, s0, intra, q_dec, k_dec)


def _with_tail(body, tail_ref, o_ref):
    last = pl.num_programs(0) - 1

    @pl.when(pl.program_id(0) < last)
    def _():
        body()

    @pl.when(pl.program_id(0) == last)
    def _():
        o_ref[...] = jnp.zeros(o_ref.shape, o_ref.dtype)
        o_ref[0:tail_ref.shape[0], :] = tail_ref[...]


def _mm_res_kernel(a_ref, w_ref, x_ref, *rest):
    o_ref = rest[-1]

    def body():
        o_ref[...] = x_ref[...] + _dot(a_ref[...], w_ref[...])

    if len(rest) == 2:
        _with_tail(body, rest[0], o_ref)
    else:
        body()


def _mm_res(a, w, x, x_off, tm, tail=None):
    m, k = a.shape
    nblk = m // tm
    clamp = (lambda i: jnp.minimum(i, nblk - 1)) if tail is not None else (lambda i: i)
    in_specs = [
        pl.BlockSpec((tm, k), lambda i: (clamp(i), 0)),
        _resident((k, D_MODEL)),
        pl.BlockSpec((tm, D_MODEL), lambda i: (clamp(i) + x_off, 0)),
    ]
    args = [a, w, x]
    if tail is not None:
        in_specs.append(_resident(tail.shape))
        args.append(tail)
    steps = nblk + (tail is not None)
    return pl.pallas_call(
        _mm_res_kernel,
        out_shape=jax.ShapeDtypeStruct((steps * tm, D_MODEL), F32),
        grid=(steps,),
        in_specs=in_specs,
        out_specs=pl.BlockSpec((tm, D_MODEL), lambda i: (i, 0)),
        compiler_params=_params("arbitrary"),
        name="mm_res",
    )(*args)


def _att_proj_kernel(x_ref, g_ref, w_ref, qg_ref, kg_ref, o_ref):
    xn = _rmsnorm_rows(x_ref[...], g_ref[...]).astype(BF16)
    chunk = 4 * ATT_HD
    n_qk = 2 * ATT_NG * ATT_W // chunk
    for c in range(ATT_IN // chunk):
        acc = _dot(xn, w_ref[:, c * chunk:(c + 1) * chunk])
        if c >= n_qk:
            o_ref[:, c * chunk:(c + 1) * chunk] = acc.astype(o_ref.dtype)
            continue
        gain = qg_ref[...] if c < n_qk // 2 else kg_ref[...]
        for hh in range(chunk // ATT_HD):
            a = acc[:, hh * ATT_HD:(hh + 1) * ATT_HD]
            a = a * lax.rsqrt(jnp.mean(a * a, axis=-1, keepdims=True) + EPS) * gain
            lo = c * chunk + hh * ATT_HD
            o_ref[:, lo:lo + ATT_HD] = a.astype(o_ref.dtype)


def _att_proj(x, row_off, rows, tm, gain, w, q_gain, k_gain, out_dtype):
    return pl.pallas_call(
        _att_proj_kernel,
        out_shape=jax.ShapeDtypeStruct((rows, ATT_IN), out_dtype),
        grid=(rows // tm,),
        in_specs=[
            pl.BlockSpec((tm, D_MODEL), lambda i: (i + row_off, 0)),
            _resident((1, D_MODEL)),
            _resident((D_MODEL, ATT_IN)),
            _resident((1, ATT_HD)),
            _resident((1, ATT_HD)),
        ],
        out_specs=pl.BlockSpec((tm, ATT_IN), lambda i: (i, 0)),
        compiler_params=_params("parallel"),
        name="att_proj",
    )(x, gain, w, q_gain, k_gain)


def _att_band_kernel(q_ref, kp_ref, kc_ref, vp_ref, vc_ref, bias_ref, o_ref):
    n = pl.program_id(1)
    col = lax.broadcasted_iota(jnp.int32, (ATT_SPAN, 2 * ATT_SPAN), 1)
    no_prev = jnp.logical_and(col < ATT_SPAN, n == 0)
    lane = lax.broadcasted_iota(jnp.int32, (ATT_SPAN, LANES), 1)
    lse_tile = jnp.zeros((ATT_SPAN, LANES), F32)
    for h in range(ATT_G):
        sl = slice(h * ATT_HD, (h + 1) * ATT_HD)
        kcat = jnp.concatenate([kp_ref[0, :, sl], kc_ref[0, :, sl]], axis=0)
        vcat = jnp.concatenate([vp_ref[0, :, sl], vc_ref[0, :, sl]], axis=0)
        s = _dot_nt(q_ref[0, :, sl], kcat) + bias_ref[h]
        s = jnp.where(no_prev, NEG, s)
        m = jnp.max(s, axis=-1, keepdims=True)
        p = jnp.exp(s - m)
        den = jnp.sum(p, axis=-1, keepdims=True)
        o_ref[0, :, sl] = _dot(p.astype(BF16), vcat) / den
        lse_tile = jnp.where(lane == h, m + jnp.log(den), lse_tile)
    o_ref[0, :, ATT_W:OUT_LSE] = lse_tile


def _att_band(qkv, cols, bias):
    nseq, ls, _ = qkv.shape
    blk = lambda cb, prev: pl.BlockSpec(
        (1, ATT_SPAN, ATT_W), (lambda s, n: (s, jnp.maximum(n - 1, 0), cb)) if prev else (lambda s, n: (s, n, cb)))
    return pl.pallas_call(
        _att_band_kernel,
        out_shape=jax.ShapeDtypeStruct((nseq, ls, OUT_LSE), F32),
        grid=(nseq, ls // ATT_SPAN),
        in_specs=[blk(cols[0], False), blk(cols[1], True), blk(cols[1], False), blk(cols[2], True),
                  blk(cols[2], False), _resident((ATT_G, ATT_SPAN, 2 * ATT_SPAN))],
        out_specs=pl.BlockSpec((1, ATT_SPAN, OUT_LSE), lambda s, n: (s, n, 0)),
        compiler_params=_params("parallel", "arbitrary"),
        name="att_band",
    )(qkv, qkv, qkv, qkv, qkv, bias)


def _att_merge_kernel(o0_ref, o1_ref, o2_ref, w_ref, x_ref, tail_ref, out_ref):
    def body():
        acc = x_ref[...]
        refs = (o0_ref, o1_ref, o2_ref)
        for h in range(ATT_G):
            sl = slice(h * ATT_HD, (h + 1) * ATT_HD)
            lses = [r[:, ATT_W + h:ATT_W + h + 1] for r in refs]
            m = jnp.maximum(jnp.maximum(lses[0], lses[1]), lses[2])
            es = [jnp.exp(l - m) for l in lses]
            merged = (es[0] * o0_ref[:, sl] + es[1] * o1_ref[:, sl] + es[2] * o2_ref[:, sl]) / (es[0] + es[1] + es[2])
            acc = acc + _dot(merged.astype(BF16), w_ref[sl, :])
        out_ref[...] = acc

    _with_tail(body, tail_ref, out_ref)


def _att_merge(outs, w, x, rows, tm, tail):
    nblk = rows // tm
    row_spec = lambda width: pl.BlockSpec((tm, width), lambda i: (jnp.minimum(i, nblk - 1), 0))
    return pl.pallas_call(
        _att_merge_kernel,
        out_shape=jax.ShapeDtypeStruct(((nblk + 1) * tm, D_MODEL), F32),
        grid=(nblk + 1,),
        in_specs=[row_spec(OUT_LSE), row_spec(OUT_LSE), row_spec(OUT_LSE), _resident((ATT_W, D_MODEL)),
                  row_spec(D_MODEL), _resident(tail.shape)],
        out_specs=pl.BlockSpec((tm, D_MODEL), lambda i: (i, 0)),
        compiler_params=_params("arbitrary"),
        name="att_merge",
    )(*outs, w, x, tail)


def _t5_bucket(dist):
    max_exact = N_BUCKETS // 2
    d32 = jnp.maximum(dist, 1).astype(F32)
    large = max_exact + (jnp.log(d32 / max_exact) / jnp.log(MAX_DISTANCE / max_exact)
                         * (N_BUCKETS - max_exact)).astype(jnp.int32)
    return jnp.where(dist < max_exact, dist, jnp.minimum(large, N_BUCKETS - 1))


def _band_bias(tab, dil):
    a = jnp.arange(ATT_SPAN)[:, None]
    c = jnp.arange(2 * ATT_SPAN)[None, :]
    rel = a - c + ATT_SPAN
    bias = jnp.moveaxis(tab[_t5_bucket(jnp.clip(rel, 0, ATT_SPAN) * dil)], -1, 0)
    return jnp.where(((rel >= 0) & (rel <= ATT_SPAN))[None], bias, NEG)


def _att_sample_kernel(qkv_ref, c0_ref, c1_ref, c2_ref, b0_ref, b1_ref, b2_ref, h0_ref, h1_ref, h2_ref,
                       o_ref, n0_ref, n1_ref, n2_ref, sem, *, steps):
    b = pl.program_id(0)
    copies = []
    for g, (hbm, new) in enumerate(((h0_ref, n0_ref), (h1_ref, n1_ref), (h2_ref, n2_ref))):
        lb = hbm.shape[1]
        copies.append(pltpu.make_async_copy(hbm.at[b, pl.ds(steps, lb - steps)], new.at[b, pl.ds(0, lb - steps)],
                                            sem.at[3 * g]))
        for kv in range(2):
            copies.append(pltpu.make_async_copy(qkv_ref.at[0, :, 1 + kv, g], new.at[b, pl.ds(lb - steps, steps), kv],
                                                sem.at[3 * g + 1 + kv]))
    for cp in copies:
        cp.start()

    for t in range(steps):
        outs, lses = [], []
        for g, (c_ref, b_ref) in enumerate(((c0_ref, b0_ref), (c1_ref, b1_ref), (c2_ref, b2_ref))):
            dil = ATT_GROUPS[g][1]
            q = qkv_ref[0, t, 0, g]
            if dil == 1:
                keys = jnp.concatenate([c_ref[0, t:, 0, 0], qkv_ref[0, :t + 1, 1, g]], axis=0)
                vals = jnp.concatenate([c_ref[0, t:, 0, 1], qkv_ref[0, :t + 1, 2, g]], axis=0)
            else:
                keys = jnp.concatenate([c_ref[0, :, t, 0], qkv_ref[0, t:t + 1, 1, g]], axis=0)
                vals = jnp.concatenate([c_ref[0, :, t, 1], qkv_ref[0, t:t + 1, 2, g]], axis=0)
            s = jnp.sum(keys * q[None], axis=-1, keepdims=True) + b_ref[...]
            m = jnp.max(s, axis=0)
            p = jnp.exp(s - m[None])
            den = jnp.sum(p, axis=0)
            outs.append(jnp.sum(p * vals, axis=0) / den)
            lses.append(m + jnp.log(den))
        m = jnp.maximum(jnp.maximum(lses[0], lses[1]), lses[2])
        es = [jnp.exp(l - m) for l in lses]
        o_ref[0, t] = (es[0] * outs[0] + es[1] * outs[1] + es[2] * outs[2]) / (es[0] + es[1] + es[2])

    for cp in copies:
        cp.wait()


def _att_sample(qkv, caches, biases):
    nb, steps = qkv.shape[:2]
    assert steps <= min(d for _, d in ATT_GROUPS[1:]) and all(c.shape[1] == w for c, (w, _) in zip(caches, ATT_GROUPS))
    views, view_specs = [], []
    for c, (win, dil) in zip(caches, ATT_GROUPS):
        views.append(c.reshape(nb, ATT_SPAN, dil, 2, ATT_G, ATT_HD))
        view_specs.append(pl.BlockSpec((1, ATT_SPAN, min(dil, steps), 2, ATT_G, ATT_HD), lambda b: (b, 0, 0, 0, 0, 0)))
    any_spec = pl.BlockSpec(memory_space=pl.ANY)
    return pl.pallas_call(
        functools.partial(_att_sample_kernel, steps=steps),
        out_shape=(jax.ShapeDtypeStruct((nb, steps, ATT_G, ATT_HD), F32),
                   *[jax.ShapeDtypeStruct(c.shape, c.dtype) for c in caches]),
        grid=(nb,),
        in_specs=[pl.BlockSpec((1, steps, 3, ATT_NG, ATT_G, ATT_HD), lambda b: (b, 0, 0, 0, 0, 0)),
                  *view_specs,
                  *[_resident((ATT_SPAN + 1, ATT_G, LANES))] * ATT_NG,
                  any_spec, any_spec, any_spec],
        out_specs=(pl.BlockSpec((1, steps, ATT_G, ATT_HD), lambda b: (b, 0, 0, 0)), any_spec, any_spec, any_spec),
        scratch_shapes=[pltpu.SemaphoreType.DMA((3 * ATT_NG,))],
        compiler_params=_params("arbitrary"),
        name="att_sample",
    )(qkv, *views, *biases, *caches)


def _router_logits(xn, wr_ref, br_ref):
    return jnp.dot(xn, wr_ref[...], preferred_element_type=F32, precision=lax.Precision.HIGHEST) + br_ref[...]


def _router_kernel(x_ref, g_ref, wr_ref, br_ref, o_ref):
    logits = _router_logits(_rmsnorm_rows(x_ref[...], g_ref[...]), wr_ref, br_ref)
    lane = lax.broadcasted_iota(jnp.int32, logits.shape, 1)
    lg = jnp.where(lane < MOE_GROUPS, logits, NEG)
    mx = jnp.max(lg, axis=-1, keepdims=True)
    sel = jnp.min(jnp.where(lg == mx, lane, LANES), axis=-1, keepdims=True)
    o_ref[...] = jnp.broadcast_to(sel, o_ref.shape)


def _router(x, row_off, rows, tm, gain, wr, br):
    return pl.pallas_call(
        _router_kernel,
        out_shape=jax.ShapeDtypeStruct((rows, LANES), jnp.int32),
        grid=(rows // tm,),
        in_specs=[pl.BlockSpec((tm, D_MODEL), lambda i: (i + row_off, 0)), _resident((1, D_MODEL)),
                  _resident((D_MODEL, LANES)), _resident((1, LANES))],
        out_specs=pl.BlockSpec((tm, LANES), lambda i: (i, 0)),
        compiler_params=_params("parallel"),
        name="moe_router",
    )(x, gain, wr, br)


def _moe_kernel(tg_ref, src_ref, dst_ref, x_hbm, gain_ref, wr_ref, br_ref, wg_ref, wu_ref, wd_ref, y_hbm,
                xbuf, ybuf, gsem, ssem):
    t = pl.program_id(0)
    nt = pl.num_programs(0)
    slot = t % 2
    tile = xbuf.shape[1]

    def start_gather(tile_idx, sl):
        def body(k, carry):
            pltpu.make_async_copy(x_hbm.at[pl.ds(src_ref[tile_idx * tile + k], 1)], xbuf.at[sl, pl.ds(k, 1)],
                                  gsem.at[sl]).start()
            return carry
        lax.fori_loop(0, tile, body, 0)

    def wait_scatter(sl):
        pltpu.make_async_copy(ybuf.at[sl], y_hbm.at[pl.ds(0, tile)], ssem.at[sl]).wait()

    @pl.when(t == 0)
    def _():
        start_gather(0, 0)

    @pl.when(t + 1 < nt)
    def _():
        start_gather(t + 1, 1 - slot)

    pltpu.make_async_copy(x_hbm.at[pl.ds(0, tile)], xbuf.at[slot], gsem.at[slot]).wait()

    @pl.when(t >= 2)
    def _():
        wait_scatter(slot)

    grp = tg_ref[t]
    x = xbuf[slot]
    xn = _rmsnorm_rows(x, gain_ref[...])
    logits = _router_logits(xn, wr_ref, br_ref)
    lane = lax.broadcasted_iota(jnp.int32, logits.shape, 1)
    is_group = lane < MOE_GROUPS
    mx = jnp.max(jnp.where(is_group, logits, NEG), axis=-1, keepdims=True)
    den = jnp.sum(jnp.where(is_group, jnp.exp(logits - mx), 0.0), axis=-1, keepdims=True)
    sel = jnp.sum(jnp.where(lane == grp, logits, 0.0), axis=-1, keepdims=True)
    p_group = jnp.exp(sel - mx) / den
    lo = MOE_GROUPS + MOE_PER_GROUP * grp
    le = jnp.where(jnp.logical_and(lane >= lo, lane < lo + MOE_PER_GROUP), logits, NEG)
    v1 = jnp.max(le, axis=-1, keepdims=True)
    i1 = jnp.min(jnp.where(le == v1, lane, LANES), axis=-1, keepdims=True)
    le = jnp.where(lane == i1, NEG, le)
    v2 = jnp.max(le, axis=-1, keepdims=True)
    i2 = jnp.min(jnp.where(le == v2, lane, LANES), axis=-1, keepdims=True)
    e2 = jnp.exp(v2 - v1)
    w1 = p_group / (1.0 + e2)
    w2 = p_group * e2 / (1.0 + e2)

    xb = xn.astype(BF16)
    acc = jnp.zeros(x.shape, F32)
    for e in range(MOE_PER_GROUP):
        ce = jnp.where(i1 == lo + e, w1, 0.0) + jnp.where(i2 == lo + e, w2, 0.0)
        hg = _dot(xb, wg_ref[e])
        hid = hg * jax.nn.sigmoid(hg) * _dot(xb, wu_ref[e])
        acc = acc + ce * _dot(hid.astype(BF16), wd_ref[e])
    ybuf[slot] = x + acc

    def scatter(k, carry):
        pltpu.make_async_copy(ybuf.at[slot, pl.ds(k, 1)], y_hbm.at[pl.ds(dst_ref[t * tile + k], 1)],
                              ssem.at[slot]).start()
        return carry
    lax.fori_loop(0, tile, scatter, 0)

    @pl.when(t == nt - 1)
    def _():
        wait_scatter(slot)

        @pl.when(nt >= 2)
        def _():
            wait_scatter(1 - slot)


def _moe_plan(group, tile, n_tiles):
    n = group.shape[0]
    onehot = (group[:, None] == jnp.arange(MOE_GROUPS)[None, :]).astype(jnp.int32)
    csum = jnp.cumsum(onehot, axis=0)
    rank = jnp.take_along_axis(csum, group[:, None], axis=1)[:, 0] - 1
    tiles_per = (csum[-1] + tile - 1) // tile
    tile_end = jnp.cumsum(tiles_per)
    tile_start = tile_end - tiles_per
    slot_of_row = tile_start[group] * tile + rank
    tile_group = jnp.minimum(jnp.sum(jnp.arange(n_tiles)[:, None] >= tile_end[None, :], axis=1), MOE_GROUPS - 1)
    n_slots = n_tiles * tile
    rows = jnp.arange(n, dtype=jnp.int32)
    src = jnp.zeros((n_slots,), jnp.int32).at[slot_of_row].set(rows)
    real = jnp.zeros((n_slots,), jnp.int32).at[slot_of_row].set(1)
    pad_rank = jnp.cumsum(1 - real) - 1
    dst = jnp.where(real == 1, src, n + pad_rank).astype(jnp.int32)
    return tile_group.astype(jnp.int32), src, dst


def _moe(x, n_prompt, n_sample, gain, br, wr_f32, wg, wu, wd):
    n = n_prompt + n_sample
    tile = MOE_TILE
    n_tiles = n // tile + MOE_GROUPS
    gp = _router(x, 0, n_prompt, ROW_TILE, gain, wr_f32, br)
    gs = _router(x, n_prompt // n_sample, n_sample, n_sample, gain, wr_f32, br)
    group = jnp.concatenate([gp[:, 0], gs[:, 0]])
    tile_group, src, dst = _moe_plan(group, tile, n_tiles)
    w_spec = lambda shape: pl.BlockSpec((MOE_PER_GROUP,) + shape, lambda t, tg, s, d: (tg[t], 0, 0))
    const = lambda shape: pl.BlockSpec(shape, lambda t, tg, s, d: (0,) * len(shape), pipeline_mode=pl.Buffered(1))
    return pl.pallas_call(
        _moe_kernel,
        out_shape=jax.ShapeDtypeStruct((n_tiles * tile, D_MODEL), F32),
        grid_spec=pltpu.PrefetchScalarGridSpec(
            num_scalar_prefetch=3,
            grid=(n_tiles,),
            in_specs=[pl.BlockSpec(memory_space=pl.ANY), const((1, D_MODEL)), const((D_MODEL, LANES)), const((1, LANES)),
                      w_spec((D_MODEL, MOE_HIDDEN)), w_spec((D_MODEL, MOE_HIDDEN)), w_spec((MOE_HIDDEN, D_MODEL))],
            out_specs=pl.BlockSpec(memory_space=pl.ANY),
            scratch_shapes=[pltpu.VMEM((2, tile, D_MODEL), F32), pltpu.VMEM((2, tile, D_MODEL), F32),
                            pltpu.SemaphoreType.DMA((2,)), pltpu.SemaphoreType.DMA((2,))]),
        compiler_params=_params("arbitrary"),
        name="moe_experts",
    )(tile_group, src, dst, x, gain, wr_f32, br, wg, wu, wd)


def _rope_tables(pos):
    half = RET_DK // 2
    inv = 1.0 / (ROPE_BASE ** jnp.linspace(0.0, 1.0, half, dtype=F32))
    ang = jnp.repeat(pos.astype(F32)[:, None] * inv[None, :], 2, axis=-1)
    sign = jnp.where(jnp.arange(RET_DK) % 2 == 0, -1.0, 1.0).astype(F32)
    return jnp.cos(ang), jnp.sin(ang) * sign


def _window_rows(qkv, g, n_rows):
    nb, length, _ = qkv.shape
    k = qkv[:, length - n_rows:, (ATT_NG + g) * ATT_W:(ATT_NG + g + 1) * ATT_W]
    v = qkv[:, length - n_rows:, (2 * ATT_NG + g) * ATT_W:(2 * ATT_NG + g + 1) * ATT_W]
    return jnp.stack([k, v], axis=2).reshape(nb, n_rows, 2, ATT_G, ATT_HD).astype(F32)


def kernel(x_prompt, x_sample, state_ret, cache_win0, cache_win1, cache_win2, rel_bias, norm_mix, norm_ffn, w_in_ret, w_out_ret, w_in_att, q_norm, k_norm, w_out_att, w_router_group, b_router_group, w_router_expert, b_router_expert, w_gate, w_up, w_down):
    nb_p, len_p, _ = x_prompt.shape
    nb_s, len_s, _ = x_sample.shape
    n_p, n_s = nb_p * len_p, nb_s * len_s
    n_all = n_p + n_s
    assert n_p % ROW_TILE == 0 and n_p % n_s == 0 and len_p % RET_CHUNK == 0 and len_s <= SAMPLE_PAD
    assert all(len_p % (ATT_SPAN * d) == 0 for _, d in ATT_GROUPS)
    caches = (cache_win0[0], cache_win1[0], cache_win2[0])

    xp = x_prompt.reshape(n_p, D_MODEL)
    xs = x_sample.reshape(n_s, D_MODEL)
    row = lambda v: v.reshape(1, -1).astype(F32)

    def moe_weights(i):
        wr = jnp.zeros((D_MODEL, LANES), F32)
        wr = wr.at[:, :MOE_GROUPS].set(w_router_group[i]).at[:, MOE_GROUPS:MOE_GROUPS + MOE_EXPERTS].set(w_router_expert[i])
        br = jnp.zeros((1, LANES), F32)
        br = br.at[0, :MOE_GROUPS].set(b_router_group[i]).at[0, MOE_GROUPS:MOE_GROUPS + MOE_EXPERTS].set(b_router_expert[i])
        return (row(norm_ffn[i]), br, wr, w_gate[i].astype(BF16), w_up[i].astype(BF16), w_down[i].astype(BF16))

    gain0 = row(norm_mix[0])
    w_in = w_in_ret[0].astype(BF16)
    w_out = w_out_ret[0].astype(BF16)
    cos_p, sin_p = _rope_tables(jnp.arange(len_p))
    pos_s = jnp.tile(PAST_LEN + jnp.arange(SAMPLE_PAD), nb_s)
    cos_s, sin_s = _rope_tables(pos_s)
    xs_pad = jnp.pad(x_sample, ((0, 0), (0, SAMPLE_PAD - len_s), (0, 0))).reshape(nb_s * SAMPLE_PAD, D_MODEL)

    proj_p = _ret_proj(xp, 0, n_p, ROW_TILE, gain0, w_in, cos_p, sin_p)
    proj_s = _ret_proj(xs_pad, 0, nb_s * SAMPLE_PAD, nb_s * SAMPLE_PAD, gain0, w_in, cos_s, sin_s)
    zero_state = jnp.zeros((nb_p, RET_HEADS, RET_DK, RET_DV), F32)
    o_p, ret_p = _ret_core(proj_p, zero_state, nb_p, len_p // RET_CHUNK, RET_CHUNK, RET_CHUNK)
    o_s, ret_s = _ret_core(proj_s, state_ret[0].astype(F32), nb_s, 1, SAMPLE_PAD, len_s)
    o_s = o_s.reshape(nb_s, SAMPLE_PAD, RET_VW)[:, :len_s].reshape(n_s, RET_VW)
    x1 = _mm_res(o_p, w_out, xp, 0, ROW_TILE, tail=_mm_res(o_s, w_out, xs, 0, n_s))
    x2 = _moe(x1, n_p, n_s, *moe_weights(0))

    gain1 = row(norm_mix[1])
    w_in = w_in_att[0].astype(BF16)
    w_out = w_out_att[0].astype(BF16)
    q_gain = row(q_norm[0]) * (ATT_HD ** -0.5)
    k_gain = row(k_norm[0])
    qkv_p = _att_proj(x2, 0, n_p, ROW_TILE, gain1, w_in, q_gain, k_gain, BF16)
    qkv_s = _att_proj(x2, n_p // n_s, n_s, n_s, gain1, w_in, q_gain, k_gain, F32)

    outs, win_p = [], []
    qkv3 = qkv_p.reshape(nb_p, len_p, ATT_IN)
    qkv5 = qkv_p.reshape(nb_p, len_p, 3, ATT_NG, ATT_W)
    for g, (win, dil) in enumerate(ATT_GROUPS):
        tab = rel_bias[:, g * ATT_G:(g + 1) * ATT_G].astype(F32)
        bias = _band_bias(tab, dil)
        if dil == 1:
            o = _att_band(qkv3, (0, ATT_NG, 2 * ATT_NG), bias)
        else:
            ls = len_p // dil
            t = qkv5[:, :, :, g, :].reshape(nb_p, ls, dil, 3 * ATT_W)
            t = jnp.swapaxes(t, 1, 2).reshape(nb_p * dil, ls, 3 * ATT_W)
            o = _att_band(t, (0, 1, 2), bias)
            o = jnp.swapaxes(o.reshape(nb_p, dil, ls, OUT_LSE), 1, 2)
        outs.append(o.reshape(n_p, OUT_LSE))
        win_p.append(_window_rows(qkv3, g, min(win, len_p))[None])
    sample_bias = []
    for g, (win, dil) in enumerate(ATT_GROUPS):
        tab = rel_bias[:, g * ATT_G:(g + 1) * ATT_G].astype(F32)
        b = tab[_t5_bucket((ATT_SPAN - jnp.arange(ATT_SPAN + 1)) * dil)]
        sample_bias.append(jnp.broadcast_to(b[:, :, None], (ATT_SPAN + 1, ATT_G, LANES)))
    o_s, new0, new1, new2 = _att_sample(qkv_s.reshape(nb_s, len_s, 3, ATT_NG, ATT_G, ATT_HD),
                                        [c.astype(F32) for c in caches], sample_bias)
    x3_s = _mm_res(o_s.reshape(n_s, ATT_W).astype(BF16), w_out, x2, n_p // n_s, n_s)
    x3 = _att_merge(outs, w_out, x2, n_p, ROW_TILE, x3_s)
    x4 = _moe(x3, n_p, n_s, *moe_weights(1))

    y_p = x4[:n_p].reshape(nb_p, len_p, D_MODEL)
    y_s = x4[n_p:n_all].reshape(nb_s, len_s, D_MODEL)
    dt = state_ret.dtype
    return (y_p, y_s, ret_p[None].astype(dt), ret_s[None].astype(dt),
            win_p[0], new0[None], win_p[1], new1[None], win_p[2], new2[None])
```

```python
import functools

import jax
import jax.numpy as jnp
from jax import lax
from jax.experimental import pallas as pl
from jax.experimental.pallas import tpu as pltpu

F32 = jnp.float32
BF16 = jnp.bfloat16

D_MODEL = 1024
PAST_LEN = 16384
RET_HEADS = 4
RET_DK = 256
RET_DV = 512
RET_QK = RET_HEADS * RET_DK
RET_VW = RET_HEADS * RET_DV
RET_IN = 2 * RET_QK + 2 * RET_VW
ROPE_BASE = 10000.0
ATT_GROUPS = ((128, 1), (512, 4), (2048, 16))
ATT_NG = len(ATT_GROUPS)
ATT_G = 8
ATT_HD = 128
ATT_W = ATT_G * ATT_HD
ATT_IN = 3 * ATT_NG * ATT_W
ATT_SPAN = 128
N_BUCKETS = 32
MAX_DISTANCE = 2048
MOE_GROUPS = 4
MOE_PER_GROUP = 4
MOE_EXPERTS = MOE_GROUPS * MOE_PER_GROUP
MOE_HIDDEN = 512
EPS = 1e-6
NEG = -1e30

LANES = 128
V7X_VMEM_LIMIT_BYTES = 56 * 1024 * 1024
ROW_TILE = 512
RET_CHUNK = 256
SAMPLE_PAD = 16
MOE_TILE = 512
WINDOW_SHIFT_ROWS = 256
OUT_LSE = ATT_W + LANES


def _params(*sem):
    return pltpu.CompilerParams(dimension_semantics=sem, vmem_limit_bytes=V7X_VMEM_LIMIT_BYTES)


def _resident(shape):
    return pl.BlockSpec(shape, lambda *_: (0,) * len(shape), pipeline_mode=pl.Buffered(1))


def _rmsnorm_rows(x, gain):
    return x * lax.rsqrt(jnp.mean(x * x, axis=-1, keepdims=True) + EPS) * gain


def _dot(a, b):
    return jnp.dot(a, b, preferred_element_type=F32)


def _dot_nt(a, b):
    return lax.dot_general(a, b, (((1,), (1,)), ((), ())), preferred_element_type=F32)


def _dot_tn(a, b):
    return lax.dot_general(a, b, (((0,), (0,)), ((), ())), preferred_element_type=F32)


def _ret_proj_kernel(x_ref, g_ref, w_ref, cos_ref, sin_ref, o_ref):
    xn = _rmsnorm_rows(x_ref[...], g_ref[...]).astype(BF16)
    cos = cos_ref[...]
    sin = sin_ref[...]
    even = (lax.broadcasted_iota(jnp.int32, cos.shape, 1) & 1) == 0
    for c in range(2 * RET_HEADS):
        sl = slice(c * RET_DK, (c + 1) * RET_DK)
        acc = _dot(xn, w_ref[:, sl])
        partner = jnp.where(even, pltpu.roll(acc, RET_DK - 1, 1), pltpu.roll(acc, 1, 1))
        r = acc * cos + partner * sin
        if c >= RET_HEADS:
            r = r * (RET_DK ** -0.5)
        o_ref[:, sl] = r.astype(o_ref.dtype)
    for c in range(2 * RET_QK // RET_DV, RET_IN // RET_DV):
        sl = slice(c * RET_DV, (c + 1) * RET_DV)
        o_ref[:, sl] = _dot(xn, w_ref[:, sl]).astype(o_ref.dtype)


def _ret_proj(x, row_off, rows, tm, gain, w, cos, sin):
    tab_blocks = cos.shape[0] // tm
    return pl.pallas_call(
        _ret_proj_kernel,
        out_shape=jax.ShapeDtypeStruct((rows, RET_IN), BF16),
        grid=(rows // tm,),
        in_specs=[
            pl.BlockSpec((tm, D_MODEL), lambda i: (i + row_off, 0)),
            _resident((1, D_MODEL)),
            _resident((D_MODEL, RET_IN)),
            pl.BlockSpec((tm, RET_DK), lambda i: (i % tab_blocks, 0)),
            pl.BlockSpec((tm, RET_DK), lambda i: (i % tab_blocks, 0)),
        ],
        out_specs=pl.BlockSpec((tm, RET_IN), lambda i: (i, 0)),
        compiler_params=_params("parallel"),
        name="ret_proj",
    )(x, gain, w, cos, sin)


def _ret_core_kernel(cdec_ref, q_ref, k_ref, v_ref, gt_ref, s0_ref, intra_ref, qdec_ref, kdec_ref, o_ref, st_ref):
    @pl.when(pl.program_id(1) == 0)
    def _():
        st_ref[...] = s0_ref[...]

    for h in range(RET_HEADS):
        qk = slice(h * RET_DK, (h + 1) * RET_DK)
        vv = slice(h * RET_DV, (h + 1) * RET_DV)
        qh, kh, vh = q_ref[:, qk], k_ref[:, qk], v_ref[:, vv]
        s = st_ref[0, h]
        sc = _dot_nt(qh, kh) * intra_ref[h]
        o = _dot(sc.astype(BF16), vh)
        o = o + _dot(qh, s.astype(BF16)) * jnp.concatenate([qdec_ref[h]] * (RET_DV // LANES), axis=1)
        kd = (kh.astype(F32) * jnp.concatenate([kdec_ref[h]] * (RET_DK // LANES), axis=1)).astype(BF16)
        st_ref[0, h] = s * cdec_ref[h] + _dot_tn(kd, vh)
        o = o * lax.rsqrt(jnp.mean(o * o, axis=-1, keepdims=True) + EPS)
        g = gt_ref[:, vv].astype(F32)
        o_ref[:, vv] = (o * (g * jax.nn.sigmoid(g))).astype(o_ref.dtype)


def _ret_decay_tables(chunk, n_real):
    log_g = jnp.log(1.0 - 2.0 ** (-5.0 - jnp.arange(RET_HEADS, dtype=F32)))
    idx = jnp.arange(chunk, dtype=F32)
    diff = idx[:, None] - idx[None, :]
    intra = jnp.where(diff >= 0, jnp.exp(jnp.maximum(diff, 0.0)[None] * log_g[:, None, None]), 0.0)
    q_dec = jnp.exp((idx + 1.0)[None, :] * log_g[:, None])
    k_dec = jnp.where(idx[None, :] < n_real, jnp.exp((n_real - 1.0 - idx)[None, :] * log_g[:, None]), 0.0)
    c_dec = jnp.exp(n_real * log_g)
    rep = lambda t: jnp.broadcast_to(t[:, :, None], (RET_HEADS, chunk, LANES))
    return c_dec, intra, rep(q_dec), rep(k_dec)


def _ret_core(proj, s0, nseq, nchunk, chunk, n_real):
    c_dec, intra, q_dec, k_dec = _ret_decay_tables(chunk, n_real)
    rows = nseq * nchunk * chunk
    row = lambda b, c: b * nchunk + c
    st_spec = pl.BlockSpec((1, RET_HEADS, RET_DK, RET_DV), lambda b, c: (b, 0, 0, 0))
    return pl.pallas_call(
        _ret_core_kernel,
        out_shape=(jax.ShapeDtypeStruct((rows, RET_VW), BF16),
                   jax.ShapeDtypeStruct((nseq, RET_HEADS, RET_DK, RET_DV), F32)),
        grid=(nseq, nchunk),
        in_specs=[
            pl.BlockSpec(memory_space=pltpu.SMEM),
            pl.BlockSpec((chunk, RET_QK), lambda b, c: (row(b, c), 0)),
            pl.BlockSpec((chunk, RET_QK), lambda b, c: (row(b, c), 1)),
            pl.BlockSpec((chunk, RET_VW), lambda b, c: (row(b, c), 1)),
            pl.BlockSpec((chunk, RET_VW), lambda b, c: (row(b, c), 2)),
            st_spec,
            _resident((RET_HEADS, chunk, chunk)),
            _resident((RET_HEADS, chunk, LANES)),
            _resident((RET_HEADS, chunk, LANES)),
        ],
        out_specs=(pl.BlockSpec((chunk, RET_VW), lambda b, c: (row(b, c), 0)), st_spec),
        compiler_params=_params("parallel", "arbitrary"),
        name="ret_core",
    )(c_dec, proj, proj, proj, proj, s0, intra, q_dec, k_dec)


def _with_tail(body, tail_ref, o_ref):
    last = pl.num_programs(0) - 1

    @pl.when(pl.program_id(0) < last)
    def _():
        body()

    @pl.when(pl.program_id(0) == last)
    def _():
        o_ref[...] = jnp.zeros(o_ref.shape, o_ref.dtype)
        o_ref[0:tail_ref.shape[0], :] = tail_ref[...]


def _mm_res_kernel(a_ref, w_ref, x_ref, *rest):
    o_ref = rest[-1]

    def body():
        o_ref[...] = x_ref[...] + _dot(a_ref[...], w_ref[...])

    if len(rest) == 2:
        _with_tail(body, rest[0], o_ref)
    else:
        body()


def _mm_res(a, w, x, x_off, tm, tail=None):
    m, k = a.shape
    nblk = m // tm
    clamp = (lambda i: jnp.minimum(i, nblk - 1)) if tail is not None else (lambda i: i)
    in_specs = [
        pl.BlockSpec((tm, k), lambda i: (clamp(i), 0)),
        _resident((k, D_MODEL)),
        pl.BlockSpec((tm, D_MODEL), lambda i: (clamp(i) + x_off, 0)),
    ]
    args = [a, w, x]
    if tail is not None:
        in_specs.append(_resident(tail.shape))
        args.append(tail)
    steps = nblk + (tail is not None)
    return pl.pallas_call(
        _mm_res_kernel,
        out_shape=jax.ShapeDtypeStruct((steps * tm, D_MODEL), F32),
        grid=(steps,),
        in_specs=in_specs,
        out_specs=pl.BlockSpec((tm, D_MODEL), lambda i: (i, 0)),
        compiler_params=_params("arbitrary"),
        name="mm_res",
    )(*args)


ATT_PROJ_CHUNK = 4 * ATT_HD


def _att_proj_chunk(xn, w_ref, qg_ref, kg_ref, c, width):
    acc = _dot(xn, w_ref[:, c * ATT_PROJ_CHUNK:(c + 1) * ATT_PROJ_CHUNK])
    n_qk = 2 * (width // 3) // ATT_PROJ_CHUNK
    if c >= n_qk:
        return acc
    gain = qg_ref[...] if c < n_qk // 2 else kg_ref[...]
    heads = []
    for hh in range(ATT_PROJ_CHUNK // ATT_HD):
        a = acc[:, hh * ATT_HD:(hh + 1) * ATT_HD]
        heads.append(a * lax.rsqrt(jnp.mean(a * a, axis=-1, keepdims=True) + EPS) * gain)
    return jnp.concatenate(heads, axis=1)


def _att_proj_kernel(x_ref, g_ref, w_ref, qg_ref, kg_ref, o_ref):
    xn = _rmsnorm_rows(x_ref[...], g_ref[...]).astype(BF16)
    width = o_ref.shape[1]
    for c in range(width // ATT_PROJ_CHUNK):
        sl = slice(c * ATT_PROJ_CHUNK, (c + 1) * ATT_PROJ_CHUNK)
        o_ref[:, sl] = _att_proj_chunk(xn, w_ref, qg_ref, kg_ref, c, width).astype(o_ref.dtype)


def _att_proj_dil_kernel(x_ref, g_ref, w_ref, qg_ref, kg_ref, o_ref, xn_ref, *, picks):
    rows = x_ref.shape[0]
    k = 0
    for mids in picks:
        for s in mids:
            xn_ref[k * rows:(k + 1) * rows, :] = _rmsnorm_rows(x_ref[:, s, :], g_ref[...]).astype(BF16)
            k += 1
    per = rows * len(picks[0])
    width = o_ref.shape[3]
    xn = xn_ref[...]
    for c in range(width // ATT_PROJ_CHUNK):
        sl = slice(c * ATT_PROJ_CHUNK, (c + 1) * ATT_PROJ_CHUNK)
        val = _att_proj_chunk(xn, w_ref, qg_ref, kg_ref, c, width).astype(o_ref.dtype)
        for r in range(len(picks)):
            o_ref[0, r, :, sl] = val[r * per:(r + 1) * per]


def _residue_picks(dil):
    if dil % 8 == 0:
        return dil, [[r] for r in range(dil)]
    assert 8 % dil == 0 and dil > 1
    return 8, [[r + dil * e for e in range(8 // dil)] for r in range(dil)]


def _att_proj_dil(x, nb, length, dil, gain, w, q_gain, k_gain):
    sub, picks = _residue_picks(dil)
    width = w.shape[1]
    rows = ATT_SPAN // 2
    per = rows * len(picks[0])
    ls = length // dil
    steps_per_seq = ls // per
    return pl.pallas_call(
        functools.partial(_att_proj_dil_kernel, picks=picks),
        out_shape=jax.ShapeDtypeStruct((nb, dil, ls, width), BF16),
        grid=(nb * steps_per_seq,),
        in_specs=[
            pl.BlockSpec((rows, sub, D_MODEL), lambda i: (i, 0, 0)),
            _resident((1, D_MODEL)),
            _resident((D_MODEL, width)),
            _resident((1, ATT_HD)),
            _resident((1, ATT_HD)),
        ],
        out_specs=pl.BlockSpec((1, dil, per, width), lambda i: (i // steps_per_seq, 0, i % steps_per_seq, 0)),
        scratch_shapes=[pltpu.VMEM((dil * per, D_MODEL), BF16)],
        compiler_params=_params("parallel"),
        name="att_proj_dil",
    )(x.reshape(x.shape[0] // sub, sub, D_MODEL), gain, w, q_gain, k_gain)


def _att_proj(x, row_off, rows, tm, gain, w, q_gain, k_gain, out_dtype):
    width = w.shape[1]
    return pl.pallas_call(
        _att_proj_kernel,
        out_shape=jax.ShapeDtypeStruct((rows, width), out_dtype),
        grid=(rows // tm,),
        in_specs=[
            pl.BlockSpec((tm, D_MODEL), lambda i: (i + row_off, 0)),
            _resident((1, D_MODEL)),
            _resident((D_MODEL, width)),
            _resident((1, ATT_HD)),
            _resident((1, ATT_HD)),
        ],
        out_specs=pl.BlockSpec((tm, width), lambda i: (i, 0)),
        compiler_params=_params("parallel"),
        name="att_proj",
    )(x, gain, w, q_gain, k_gain)


def _att_band_kernel(q_ref, kp_ref, kc_ref, vp_ref, vc_ref, bias_ref, o_ref, *, picks):
    n = pl.program_id(1)
    res_per_step = q_ref.shape[1]

    def put(r, val):
        if picks is None:
            o_ref[0] = val
        else:
            per = ATT_SPAN // len(picks[0])
            for e in range(len(picks[0])):
                o_ref[0, :, r + (picks[0][e] - picks[0][0]), :] = val[e * per:(e + 1) * per]

    col = lax.broadcasted_iota(jnp.int32, (ATT_SPAN, 2 * ATT_SPAN), 1)
    no_prev = jnp.logical_and(col < ATT_SPAN, n == 0)
    lane = lax.broadcasted_iota(jnp.int32, (ATT_SPAN, LANES), 1)
    for rr in range(res_per_step):
        r = rr if res_per_step > 1 else pl.program_id(2)
        lse_tile = jnp.zeros((ATT_SPAN, LANES), F32)
        heads = []
        for h in range(ATT_G):
            sl = slice(h * ATT_HD, (h + 1) * ATT_HD)
            kcat = jnp.concatenate([kp_ref[0, rr, :, sl], kc_ref[0, rr, :, sl]], axis=0)
            vcat = jnp.concatenate([vp_ref[0, rr, :, sl], vc_ref[0, rr, :, sl]], axis=0)
            s = _dot_nt(q_ref[0, rr, :, sl], kcat) + bias_ref[h]
            s = jnp.where(no_prev, NEG, s)
            m = jnp.max(s, axis=-1, keepdims=True)
            p = jnp.exp(s - m)
            den = jnp.sum(p, axis=-1, keepdims=True)
            heads.append(_dot(p.astype(BF16), vcat) / den)
            lse_tile = jnp.where(lane == h, m + jnp.log(den), lse_tile)
        put(r, jnp.concatenate(heads + [lse_tile], axis=1))


def _att_band(qkv, dil, bias):
    nb, _, ls, _ = qkv.shape
    if dil == 1:
        picks, res_per_step = None, 1
        out_shape = (nb, ls, OUT_LSE)
        out_spec = pl.BlockSpec((1, ATT_SPAN, OUT_LSE), lambda b, n, r: (b, n, 0))
    else:
        sub, picks = _residue_picks(dil)
        res_per_step = dil if sub == 8 else 1
        rows = ATT_SPAN * dil // sub
        out_shape = (nb, ls * dil // sub, sub, OUT_LSE)
        out_spec = pl.BlockSpec((1, rows, sub, OUT_LSE), lambda b, n, r: (b, n, 0, 0))
    blk = lambda cb, prev: pl.BlockSpec(
        (1, res_per_step, ATT_SPAN, ATT_W),
        (lambda b, n, r: (b, r, jnp.maximum(n - 1, 0), cb)) if prev else (lambda b, n, r: (b, r, n, cb)))
    out = pl.pallas_call(
        functools.partial(_att_band_kernel, picks=picks),
        out_shape=jax.ShapeDtypeStruct(out_shape, F32),
        grid=(nb, ls // ATT_SPAN, dil // res_per_step),
        in_specs=[blk(0, False), blk(1, True), blk(1, False), blk(2, True), blk(2, False),
                  _resident((ATT_G, ATT_SPAN, 2 * ATT_SPAN))],
        out_specs=out_spec,
        compiler_params=_params("parallel", "arbitrary", "arbitrary"),
        name="att_band",
    )(qkv, qkv, qkv, qkv, qkv, bias)
    return out.reshape(nb * ls * dil, OUT_LSE)


def _window_rows_kernel(k_ref, v_ref, o_ref):
    halves = o_ref.shape[2]
    per = o_ref.shape[1]
    for e in range(halves):
        for kv, ref in enumerate((k_ref, v_ref)):
            for h in range(ATT_G):
                o_ref[0, :, e, 0, kv, h, :] = ref[0, 0, e * per:(e + 1) * per, h * ATT_HD:(h + 1) * ATT_HD].astype(o_ref.dtype)


def _window_rows(qkv, dil, dtype):
    nb, _, ls, _ = qkv.shape
    halves = 1 if dil == 1 else len(_residue_picks(dil)[1][0])
    per = ATT_SPAN // halves
    last = ls // ATT_SPAN - 1
    out = pl.pallas_call(
        _window_rows_kernel,
        out_shape=jax.ShapeDtypeStruct((nb, per, halves, dil, 2, ATT_G, ATT_HD), dtype),
        grid=(nb, dil),
        in_specs=[pl.BlockSpec((1, 1, ATT_SPAN, ATT_W), lambda b, r: (b, r, last, 1)),
                  pl.BlockSpec((1, 1, ATT_SPAN, ATT_W), lambda b, r: (b, r, last, 2))],
        out_specs=pl.BlockSpec((1, per, halves, 1, 2, ATT_G, ATT_HD), lambda b, r: (b, 0, 0, r, 0, 0, 0)),
        compiler_params=_params("parallel", "parallel"),
        name="window_rows",
    )(qkv, qkv)
    return out.reshape(nb, ATT_SPAN * dil, 2, ATT_G, ATT_HD)


def _att_merge_kernel(o0_ref, o1_ref, o2_ref, w_ref, x_ref, tail_ref, out_ref):
    def body():
        acc = x_ref[...]
        refs = (o0_ref, o1_ref, o2_ref)
        for h in range(ATT_G):
            sl = slice(h * ATT_HD, (h + 1) * ATT_HD)
            lses = [r[:, ATT_W + h:ATT_W + h + 1] for r in refs]
            m = jnp.maximum(jnp.maximum(lses[0], lses[1]), lses[2])
            es = [jnp.exp(l - m) for l in lses]
            merged = (es[0] * o0_ref[:, sl] + es[1] * o1_ref[:, sl] + es[2] * o2_ref[:, sl]) / (es[0] + es[1] + es[2])
            acc = acc + _dot(merged.astype(BF16), w_ref[sl, :])
        out_ref[...] = acc

    _with_tail(body, tail_ref, out_ref)


def _att_merge(outs, w, x, rows, tm, tail):
    nblk = rows // tm
    row_spec = lambda width: pl.BlockSpec((tm, width), lambda i: (jnp.minimum(i, nblk - 1), 0))
    return pl.pallas_call(
        _att_merge_kernel,
        out_shape=jax.ShapeDtypeStruct(((nblk + 1) * tm, D_MODEL), F32),
        grid=(nblk + 1,),
        in_specs=[row_spec(OUT_LSE), row_spec(OUT_LSE), row_spec(OUT_LSE), _resident((ATT_W, D_MODEL)),
                  row_spec(D_MODEL), _resident(tail.shape)],
        out_specs=pl.BlockSpec((tm, D_MODEL), lambda i: (i, 0)),
        compiler_params=_params("arbitrary"),
        name="att_merge",
    )(*outs, w, x, tail)


def _t5_bucket(dist):
    max_exact = N_BUCKETS // 2
    d32 = jnp.maximum(dist, 1).astype(F32)
    large = max_exact + (jnp.log(d32 / max_exact) / jnp.log(MAX_DISTANCE / max_exact)
                         * (N_BUCKETS - max_exact)).astype(jnp.int32)
    return jnp.where(dist < max_exact, dist, jnp.minimum(large, N_BUCKETS - 1))


def _band_bias(tab, dil):
    a = jnp.arange(ATT_SPAN)[:, None]
    c = jnp.arange(2 * ATT_SPAN)[None, :]
    rel = a - c + ATT_SPAN
    bias = jnp.moveaxis(tab[_t5_bucket(jnp.clip(rel, 0, ATT_SPAN) * dil)], -1, 0)
    return jnp.where(((rel >= 0) & (rel <= ATT_SPAN))[None], bias, NEG)


def _att_sample_kernel(qkv_ref, c0_ref, c1_ref, c2_ref, b0_ref, b1_ref, b2_ref, o_ref, *, steps):
    for t in range(steps):
        outs, lses = [], []
        for g, (c_ref, b_ref) in enumerate(((c0_ref, b0_ref), (c1_ref, b1_ref), (c2_ref, b2_ref))):
            dil = ATT_GROUPS[g][1]
            q = qkv_ref[0, t, 0, g]
            if dil == 1:
                keys = jnp.concatenate([c_ref[0, t:, 0, 0], qkv_ref[0, :t + 1, 1, g]], axis=0)
                vals = jnp.concatenate([c_ref[0, t:, 0, 1], qkv_ref[0, :t + 1, 2, g]], axis=0)
            else:
                keys = jnp.concatenate([c_ref[0, :, t, 0], qkv_ref[0, t:t + 1, 1, g]], axis=0)
                vals = jnp.concatenate([c_ref[0, :, t, 1], qkv_ref[0, t:t + 1, 2, g]], axis=0)
            s = jnp.sum(keys * q[None], axis=-1, keepdims=True) + b_ref[...]
            m = jnp.max(s, axis=0)
            p = jnp.exp(s - m[None])
            den = jnp.sum(p, axis=0)
            outs.append(jnp.sum(p * vals, axis=0) / den)
            lses.append(m + jnp.log(den))
        m = jnp.maximum(jnp.maximum(lses[0], lses[1]), lses[2])
        es = [jnp.exp(l - m) for l in lses]
        o_ref[0, t] = (es[0] * outs[0] + es[1] * outs[1] + es[2] * outs[2]) / (es[0] + es[1] + es[2])


def _att_sample(qkv, caches, biases):
    nb, steps = qkv.shape[:2]
    assert steps <= min(d for _, d in ATT_GROUPS[1:]) and all(c.shape[1] == w for c, (w, _) in zip(caches, ATT_GROUPS))
    views, view_specs = [], []
    for c, (win, dil) in zip(caches, ATT_GROUPS):
        views.append(c.reshape(nb, ATT_SPAN, dil, 2, ATT_G, ATT_HD))
        view_specs.append(pl.BlockSpec((1, ATT_SPAN, min(dil, steps), 2, ATT_G, ATT_HD), lambda b: (b, 0, 0, 0, 0, 0)))
    return pl.pallas_call(
        functools.partial(_att_sample_kernel, steps=steps),
        out_shape=jax.ShapeDtypeStruct((nb, steps, ATT_G, ATT_HD), F32),
        grid=(nb,),
        in_specs=[pl.BlockSpec((1, steps, 3, ATT_NG, ATT_G, ATT_HD), lambda b: (b, 0, 0, 0, 0, 0)),
                  *view_specs,
                  *[_resident((ATT_SPAN + 1, ATT_G, LANES))] * ATT_NG],
        out_specs=pl.BlockSpec((1, steps, ATT_G, ATT_HD), lambda b: (b, 0, 0, 0)),
        compiler_params=_params("parallel"),
        name="att_sample",
    )(qkv, *views, *biases)


def _window_shift_kernel(new_ref, old_hbm, o_ref, sem, *, steps):
    b, c = pl.program_id(0), pl.program_id(1)
    rows = o_ref.shape[1]
    last = pl.num_programs(1) - 1

    @pl.when(c < last)
    def _():
        cp = pltpu.make_async_copy(old_hbm.at[b, pl.ds(c * rows + steps, rows)], o_ref.at[0], sem)
        cp.start()
        cp.wait()

    @pl.when(c == last)
    def _():
        cp = pltpu.make_async_copy(old_hbm.at[b, pl.ds(c * rows + steps, rows - steps)],
                                   o_ref.at[0, pl.ds(0, rows - steps)], sem)
        cp.start()
        o_ref[0, rows - steps:rows] = new_ref[0]
        cp.wait()


def _window_shift(old, new):
    nb, win = old.shape[:2]
    steps = new.shape[1]
    rows = min(win, WINDOW_SHIFT_ROWS)
    tail = old.shape[2:]
    return pl.pallas_call(
        functools.partial(_window_shift_kernel, steps=steps),
        out_shape=jax.ShapeDtypeStruct(old.shape, old.dtype),
        grid=(nb, win // rows),
        in_specs=[pl.BlockSpec((1, steps) + tail, lambda b, c: (b, 0, 0, 0, 0)), pl.BlockSpec(memory_space=pl.ANY)],
        out_specs=pl.BlockSpec((1, rows) + tail, lambda b, c: (b, c, 0, 0, 0)),
        scratch_shapes=[pltpu.SemaphoreType.DMA(())],
        compiler_params=_params("arbitrary", "arbitrary"),
        name="window_shift",
    )(new, old)


def _router_logits(xn, wr_ref, br_ref):
    return jnp.dot(xn, wr_ref[...], preferred_element_type=F32, precision=lax.Precision.HIGHEST) + br_ref[...]


def _router_kernel(x_ref, g_ref, wr_ref, br_ref, o_ref):
    logits = _router_logits(_rmsnorm_rows(x_ref[...], g_ref[...]), wr_ref, br_ref)
    lane = lax.broadcasted_iota(jnp.int32, logits.shape, 1)
    lg = jnp.where(lane < MOE_GROUPS, logits, NEG)
    mx = jnp.max(lg, axis=-1, keepdims=True)
    sel = jnp.min(jnp.where(lg == mx, lane, LANES), axis=-1, keepdims=True)
    o_ref[...] = jnp.broadcast_to(sel, o_ref.shape)


def _router(x, row_off, rows, tm, gain, wr, br):
    return pl.pallas_call(
        _router_kernel,
        out_shape=jax.ShapeDtypeStruct((rows, LANES), jnp.int32),
        grid=(rows // tm,),
        in_specs=[pl.BlockSpec((tm, D_MODEL), lambda i: (i + row_off, 0)), _resident((1, D_MODEL)),
                  _resident((D_MODEL, LANES)), _resident((1, LANES))],
        out_specs=pl.BlockSpec((tm, LANES), lambda i: (i, 0)),
        compiler_params=_params("parallel"),
        name="moe_router",
    )(x, gain, wr, br)


def _moe_kernel(tg_ref, src_ref, dst_ref, x_hbm, gain_ref, wr_ref, br_ref, wg_ref, wu_ref, wd_ref, y_hbm,
                xbuf, ybuf, gsem, ssem):
    t = pl.program_id(0)
    nt = pl.num_programs(0)
    slot = t % 2
    tile = xbuf.shape[1]

    def start_gather(tile_idx, sl):
        def body(k, carry):
            pltpu.make_async_copy(x_hbm.at[pl.ds(src_ref[tile_idx * tile + k], 1)], xbuf.at[sl, pl.ds(k, 1)],
                                  gsem.at[sl]).start()
            return carry
        lax.fori_loop(0, tile, body, 0)

    def wait_scatter(sl):
        pltpu.make_async_copy(ybuf.at[sl], y_hbm.at[pl.ds(0, tile)], ssem.at[sl]).wait()

    @pl.when(t == 0)
    def _():
        start_gather(0, 0)

    @pl.when(t + 1 < nt)
    def _():
        start_gather(t + 1, 1 - slot)

    pltpu.make_async_copy(x_hbm.at[pl.ds(0, tile)], xbuf.at[slot], gsem.at[slot]).wait()

    @pl.when(t >= 2)
    def _():
        wait_scatter(slot)

    grp = tg_ref[t]
    x = xbuf[slot]
    xn = _rmsnorm_rows(x, gain_ref[...])
    logits = _router_logits(xn, wr_ref, br_ref)
    lane = lax.broadcasted_iota(jnp.int32, logits.shape, 1)
    is_group = lane < MOE_GROUPS
    mx = jnp.max(jnp.where(is_group, logits, NEG), axis=-1, keepdims=True)
    den = jnp.sum(jnp.where(is_group, jnp.exp(logits - mx), 0.0), axis=-1, keepdims=True)
    sel = jnp.sum(jnp.where(lane == grp, logits, 0.0), axis=-1, keepdims=True)
    p_group = jnp.exp(sel - mx) / den
    lo = MOE_GROUPS + MOE_PER_GROUP * grp
    le = jnp.where(jnp.logical_and(lane >= lo, lane < lo + MOE_PER_GROUP), logits, NEG)
    v1 = jnp.max(le, axis=-1, keepdims=True)
    i1 = jnp.min(jnp.where(le == v1, lane, LANES), axis=-1, keepdims=True)
    le = jnp.where(lane == i1, NEG, le)
    v2 = jnp.max(le, axis=-1, keepdims=True)
    i2 = jnp.min(jnp.where(le == v2, lane, LANES), axis=-1, keepdims=True)
    e2 = jnp.exp(v2 - v1)
    w1 = p_group / (1.0 + e2)
    w2 = p_group * e2 / (1.0 + e2)

    xb = xn.astype(BF16)
    acc = jnp.zeros(x.shape, F32)
    for e in range(MOE_PER_GROUP):
        ce = jnp.where(i1 == lo + e, w1, 0.0) + jnp.where(i2 == lo + e, w2, 0.0)
        hg = _dot(xb, wg_ref[e])
        hid = hg * jax.nn.sigmoid(hg) * _dot(xb, wu_ref[e])
        acc = acc + ce * _dot(hid.astype(BF16), wd_ref[e])
    ybuf[slot] = x + acc

    def scatter(k, carry):
        pltpu.make_async_copy(ybuf.at[slot, pl.ds(k, 1)], y_hbm.at[pl.ds(dst_ref[t * tile + k], 1)],
                              ssem.at[slot]).start()
        return carry
    lax.fori_loop(0, tile, scatter, 0)

    @pl.when(t == nt - 1)
    def _():
        wait_scatter(slot)

        @pl.when(nt >= 2)
        def _():
            wait_scatter(1 - slot)


def _moe_plan(group, tile, n_tiles):
    n = group.shape[0]
    onehot = (group[:, None] == jnp.arange(MOE_GROUPS)[None, :]).astype(jnp.int32)
    csum = jnp.cumsum(onehot, axis=0)
    rank = jnp.take_along_axis(csum, group[:, None], axis=1)[:, 0] - 1
    tiles_per = (csum[-1] + tile - 1) // tile
    tile_end = jnp.cumsum(tiles_per)
    tile_start = tile_end - tiles_per
    slot_of_row = tile_start[group] * tile + rank
    tile_group = jnp.minimum(jnp.sum(jnp.arange(n_tiles)[:, None] >= tile_end[None, :], axis=1), MOE_GROUPS - 1)
    n_slots = n_tiles * tile
    rows = jnp.arange(n, dtype=jnp.int32)
    src = jnp.zeros((n_slots,), jnp.int32).at[slot_of_row].set(rows)
    real = jnp.zeros((n_slots,), jnp.int32).at[slot_of_row].set(1)
    pad_rank = jnp.cumsum(1 - real) - 1
    dst = jnp.where(real == 1, src, n + pad_rank).astype(jnp.int32)
    return tile_group.astype(jnp.int32), src, dst


def _moe(x, n_prompt, n_sample, gain, br, wr_f32, wg, wu, wd):
    n = n_prompt + n_sample
    tile = MOE_TILE
    n_tiles = n // tile + MOE_GROUPS
    gp = _router(x, 0, n_prompt, ROW_TILE, gain, wr_f32, br)
    gs = _router(x, n_prompt // n_sample, n_sample, n_sample, gain, wr_f32, br)
    group = jnp.concatenate([gp[:, 0], gs[:, 0]])
    tile_group, src, dst = _moe_plan(group, tile, n_tiles)
    w_spec = lambda shape: pl.BlockSpec((MOE_PER_GROUP,) + shape, lambda t, tg, s, d: (tg[t], 0, 0))
    const = lambda shape: pl.BlockSpec(shape, lambda t, tg, s, d: (0,) * len(shape), pipeline_mode=pl.Buffered(1))
    return pl.pallas_call(
        _moe_kernel,
        out_shape=jax.ShapeDtypeStruct((n_tiles * tile, D_MODEL), F32),
        grid_spec=pltpu.PrefetchScalarGridSpec(
            num_scalar_prefetch=3,
            grid=(n_tiles,),
            in_specs=[pl.BlockSpec(memory_space=pl.ANY), const((1, D_MODEL)), const((D_MODEL, LANES)), const((1, LANES)),
                      w_spec((D_MODEL, MOE_HIDDEN)), w_spec((D_MODEL, MOE_HIDDEN)), w_spec((MOE_HIDDEN, D_MODEL))],
            out_specs=pl.BlockSpec(memory_space=pl.ANY),
            scratch_shapes=[pltpu.VMEM((2, tile, D_MODEL), F32), pltpu.VMEM((2, tile, D_MODEL), F32),
                            pltpu.SemaphoreType.DMA((2,)), pltpu.SemaphoreType.DMA((2,))]),
        compiler_params=_params("arbitrary"),
        name="moe_experts",
    )(tile_group, src, dst, x, gain, wr_f32, br, wg, wu, wd)


def _rope_tables(pos):
    half = RET_DK // 2
    inv = 1.0 / (ROPE_BASE ** jnp.linspace(0.0, 1.0, half, dtype=F32))
    ang = jnp.repeat(pos.astype(F32)[:, None] * inv[None, :], 2, axis=-1)
    sign = jnp.where(jnp.arange(RET_DK) % 2 == 0, -1.0, 1.0).astype(F32)
    return jnp.cos(ang), jnp.sin(ang) * sign


def kernel(x_prompt, x_sample, state_ret, cache_win0, cache_win1, cache_win2, rel_bias, norm_mix, norm_ffn, w_in_ret, w_out_ret, w_in_att, q_norm, k_norm, w_out_att, w_router_group, b_router_group, w_router_expert, b_router_expert, w_gate, w_up, w_down):
    nb_p, len_p, _ = x_prompt.shape
    nb_s, len_s, _ = x_sample.shape
    n_p, n_s = nb_p * len_p, nb_s * len_s
    n_all = n_p + n_s
    assert n_p % ROW_TILE == 0 and n_p % n_s == 0 and len_p % RET_CHUNK == 0 and len_s <= SAMPLE_PAD
    assert all(len_p % (ATT_SPAN * d) == 0 for _, d in ATT_GROUPS)
    caches = (cache_win0[0], cache_win1[0], cache_win2[0])

    xp = x_prompt.reshape(n_p, D_MODEL)
    xs = x_sample.reshape(n_s, D_MODEL)
    row = lambda v: v.reshape(1, -1).astype(F32)

    def moe_weights(i):
        wr = jnp.zeros((D_MODEL, LANES), F32)
        wr = wr.at[:, :MOE_GROUPS].set(w_router_group[i]).at[:, MOE_GROUPS:MOE_GROUPS + MOE_EXPERTS].set(w_router_expert[i])
        br = jnp.zeros((1, LANES), F32)
        br = br.at[0, :MOE_GROUPS].set(b_router_group[i]).at[0, MOE_GROUPS:MOE_GROUPS + MOE_EXPERTS].set(b_router_expert[i])
        return (row(norm_ffn[i]), br, wr, w_gate[i].astype(BF16), w_up[i].astype(BF16), w_down[i].astype(BF16))

    gain0 = row(norm_mix[0])
    w_in = w_in_ret[0].astype(BF16)
    w_out = w_out_ret[0].astype(BF16)
    cos_p, sin_p = _rope_tables(jnp.arange(len_p))
    pos_s = jnp.tile(PAST_LEN + jnp.arange(SAMPLE_PAD), nb_s)
    cos_s, sin_s = _rope_tables(pos_s)
    xs_pad = jnp.pad(x_sample, ((0, 0), (0, SAMPLE_PAD - len_s), (0, 0))).reshape(nb_s * SAMPLE_PAD, D_MODEL)

    proj_p = _ret_proj(xp, 0, n_p, ROW_TILE, gain0, w_in, cos_p, sin_p)
    proj_s = _ret_proj(xs_pad, 0, nb_s * SAMPLE_PAD, nb_s * SAMPLE_PAD, gain0, w_in, cos_s, sin_s)
    zero_state = jnp.zeros((nb_p, RET_HEADS, RET_DK, RET_DV), F32)
    o_p, ret_p = _ret_core(proj_p, zero_state, nb_p, len_p // RET_CHUNK, RET_CHUNK, RET_CHUNK)
    o_s, ret_s = _ret_core(proj_s, state_ret[0].astype(F32), nb_s, 1, SAMPLE_PAD, len_s)
    o_s = o_s.reshape(nb_s, SAMPLE_PAD, RET_VW)[:, :len_s].reshape(n_s, RET_VW)
    x1 = _mm_res(o_p, w_out, xp, 0, ROW_TILE, tail=_mm_res(o_s, w_out, xs, 0, n_s))
    x2 = _moe(x1, n_p, n_s, *moe_weights(0))

    gain1 = row(norm_mix[1])
    w_in = w_in_att[0].astype(BF16)
    w_out = w_out_att[0].astype(BF16)
    q_gain = row(q_norm[0]) * (ATT_HD ** -0.5)
    k_gain = row(k_norm[0])
    qkv_s = _att_proj(x2, n_p // n_s, n_s, n_s, gain1, w_in, q_gain, k_gain, F32)
    qkv_s = qkv_s.reshape(nb_s, len_s, 3, ATT_NG, ATT_G, ATT_HD)

    outs, win_p, win_s, sample_bias = [], [], [], []
    for g, (win, dil) in enumerate(ATT_GROUPS):
        tab = rel_bias[:, g * ATT_G:(g + 1) * ATT_G].astype(F32)
        w_g = jnp.concatenate([w_in[:, (i * ATT_NG + g) * ATT_W:(i * ATT_NG + g + 1) * ATT_W] for i in range(3)], axis=1)
        bias = _band_bias(tab, dil)
        if dil == 1:
            qkv_g = _att_proj(x2, 0, n_p, ROW_TILE, gain1, w_g, q_gain, k_gain, BF16).reshape(nb_p, 1, len_p, 3 * ATT_W)
        else:
            qkv_g = _att_proj_dil(x2, nb_p, len_p, dil, gain1, w_g, q_gain, k_gain)
            halves = len(_residue_picks(dil)[1][0])
            if halves > 1:
                pos = jnp.arange(ATT_SPAN).reshape(ATT_SPAN // halves, halves).T.reshape(-1)
                bias = bias[:, pos][:, :, jnp.concatenate([pos, ATT_SPAN + pos])]
        outs.append(_att_band(qkv_g, dil, bias))
        win_p.append(_window_rows(qkv_g, dil, caches[g].dtype)[None])
        win_s.append(_window_shift(caches[g], qkv_s[:, :, 1:, g].astype(caches[g].dtype))[None])
        b = tab[_t5_bucket((ATT_SPAN - jnp.arange(ATT_SPAN + 1)) * dil)]
        sample_bias.append(jnp.broadcast_to(b[:, :, None], (ATT_SPAN + 1, ATT_G, LANES)))
    o_s = _att_sample(qkv_s, caches, sample_bias)
    x3_s = _mm_res(o_s.reshape(n_s, ATT_W).astype(BF16), w_out, x2, n_p // n_s, n_s)
    x3 = _att_merge(outs, w_out, x2, n_p, ROW_TILE, x3_s)
    x4 = _moe(x3, n_p, n_s, *moe_weights(1))

    y_p = x4[:n_p].reshape(nb_p, len_p, D_MODEL)
    y_s = x4[n_p:n_all].reshape(nb_s, len_s, D_MODEL)
    dt = state_ret.dtype
    return (y_p, y_s, ret_p[None].astype(dt), ret_s[None].astype(dt),
            win_p[0], win_s[0], win_p[1], win_s[1], win_p[2], win_s[2])
```

```python
import functools

import jax
import jax.numpy as jnp
from jax import lax
from jax.experimental import pallas as pl
from jax.experimental.pallas import tpu as pltpu

F32 = jnp.float32
BF16 = jnp.bfloat16

D_MODEL = 1024
PAST_LEN = 16384
RET_HEADS = 4
RET_DK = 256
RET_DV = 512
RET_QK = RET_HEADS * RET_DK
RET_VW = RET_HEADS * RET_DV
RET_IN = 2 * RET_QK + 2 * RET_VW
ROPE_BASE = 10000.0
ATT_GROUPS = ((128, 1), (512, 4), (2048, 16))
ATT_NG = len(ATT_GROUPS)
ATT_G = 8
ATT_HD = 128
ATT_W = ATT_G * ATT_HD
ATT_IN = 3 * ATT_NG * ATT_W
ATT_SPAN = 128
N_BUCKETS = 32
MAX_DISTANCE = 2048
MOE_GROUPS = 4
MOE_PER_GROUP = 4
MOE_EXPERTS = MOE_GROUPS * MOE_PER_GROUP
MOE_HIDDEN = 512
EPS = 1e-6
NEG = -1e30

LANES = 128
V7X_VMEM_LIMIT_BYTES = 56 * 1024 * 1024
ROW_TILE = 512
RET_CHUNK = 256
SAMPLE_PAD = 16
MOE_TILE = 512
WINDOW_SHIFT_ROWS = 256
DMA_ISSUE_UNROLL = 8
OUT_LSE = ATT_W + LANES


def _params(*sem):
    return pltpu.CompilerParams(dimension_semantics=sem, vmem_limit_bytes=V7X_VMEM_LIMIT_BYTES)


def _resident(shape):
    return pl.BlockSpec(shape, lambda *_: (0,) * len(shape), pipeline_mode=pl.Buffered(1))


def _rmsnorm_rows(x, gain):
    return x * lax.rsqrt(jnp.mean(x * x, axis=-1, keepdims=True) + EPS) * gain


def _dot(a, b):
    return jnp.dot(a, b, preferred_element_type=F32)


def _dot_nt(a, b):
    return lax.dot_general(a, b, (((1,), (1,)), ((), ())), preferred_element_type=F32)


def _dot_tn(a, b):
    return lax.dot_general(a, b, (((0,), (0,)), ((), ())), preferred_element_type=F32)


def _ret_proj_kernel(x_ref, g_ref, w_ref, cos_ref, sin_ref, o_ref):
    xn = _rmsnorm_rows(x_ref[...], g_ref[...]).astype(BF16)
    cos = cos_ref[...]
    sin = sin_ref[...]
    even = (lax.broadcasted_iota(jnp.int32, cos.shape, 1) & 1) == 0
    for c in range(2 * RET_HEADS):
        sl = slice(c * RET_DK, (c + 1) * RET_DK)
        acc = _dot(xn, w_ref[:, sl])
        partner = jnp.where(even, pltpu.roll(acc, RET_DK - 1, 1), pltpu.roll(acc, 1, 1))
        r = acc * cos + partner * sin
        if c >= RET_HEADS:
            r = r * (RET_DK ** -0.5)
        o_ref[:, sl] = r.astype(o_ref.dtype)
    for c in range(2 * RET_QK // RET_DV, RET_IN // RET_DV):
        sl = slice(c * RET_DV, (c + 1) * RET_DV)
        o_ref[:, sl] = _dot(xn, w_ref[:, sl]).astype(o_ref.dtype)


def _ret_proj(x, row_off, rows, tm, gain, w, cos, sin):
    tab_blocks = cos.shape[0] // tm
    return pl.pallas_call(
        _ret_proj_kernel,
        out_shape=jax.ShapeDtypeStruct((rows, RET_IN), BF16),
        grid=(rows // tm,),
        in_specs=[
            pl.BlockSpec((tm, D_MODEL), lambda i: (i + row_off, 0)),
            _resident((1, D_MODEL)),
            _resident((D_MODEL, RET_IN)),
            pl.BlockSpec((tm, RET_DK), lambda i: (i % tab_blocks, 0)),
            pl.BlockSpec((tm, RET_DK), lambda i: (i % tab_blocks, 0)),
        ],
        out_specs=pl.BlockSpec((tm, RET_IN), lambda i: (i, 0)),
        compiler_params=_params("parallel"),
        name="ret_proj",
    )(x, gain, w, cos, sin)


def _ret_core_kernel(cdec_ref, q_ref, k_ref, v_ref, gt_ref, s0_ref, intra_ref, qdec_ref, kdec_ref, o_ref, st_ref):
    @pl.when(pl.program_id(1) == 0)
    def _():
        st_ref[...] = s0_ref[...]

    for h in range(RET_HEADS):
        qk = slice(h * RET_DK, (h + 1) * RET_DK)
        vv = slice(h * RET_DV, (h + 1) * RET_DV)
        qh, kh, vh = q_ref[:, qk], k_ref[:, qk], v_ref[:, vv]
        s = st_ref[0, h]
        sc = _dot_nt(qh, kh) * intra_ref[h]
        o = _dot(sc.astype(BF16), vh)
        o = o + _dot(qh, s.astype(BF16)) * jnp.concatenate([qdec_ref[h]] * (RET_DV // LANES), axis=1)
        kd = (kh.astype(F32) * jnp.concatenate([kdec_ref[h]] * (RET_DK // LANES), axis=1)).astype(BF16)
        st_ref[0, h] = s * cdec_ref[h] + _dot_tn(kd, vh)
        o = o * lax.rsqrt(jnp.mean(o * o, axis=-1, keepdims=True) + EPS)
        g = gt_ref[:, vv].astype(F32)
        o_ref[:, vv] = (o * (g * jax.nn.sigmoid(g))).astype(o_ref.dtype)


def _ret_decay_tables(chunk, n_real):
    log_g = jnp.log(1.0 - 2.0 ** (-5.0 - jnp.arange(RET_HEADS, dtype=F32)))
    idx = jnp.arange(chunk, dtype=F32)
    diff = idx[:, None] - idx[None, :]
    intra = jnp.where(diff >= 0, jnp.exp(jnp.maximum(diff, 0.0)[None] * log_g[:, None, None]), 0.0)
    q_dec = jnp.exp((idx + 1.0)[None, :] * log_g[:, None])
    k_dec = jnp.where(idx[None, :] < n_real, jnp.exp((n_real - 1.0 - idx)[None, :] * log_g[:, None]), 0.0)
    c_dec = jnp.exp(n_real * log_g)
    rep = lambda t: jnp.broadcast_to(t[:, :, None], (RET_HEADS, chunk, LANES))
    return c_dec, intra, rep(q_dec), rep(k_dec)


def _ret_core(proj, s0, nseq, nchunk, chunk, n_real):
    c_dec, intra, q_dec, k_dec = _ret_decay_tables(chunk, n_real)
    rows = nseq * nchunk * chunk
    row = lambda b, c: b * nchunk + c
    st_spec = pl.BlockSpec((1, RET_HEADS, RET_DK, RET_DV), lambda b, c: (b, 0, 0, 0))
    return pl.pallas_call(
        _ret_core_kernel,
        out_shape=(jax.ShapeDtypeStruct((rows, RET_VW), BF16),
                   jax.ShapeDtypeStruct((nseq, RET_HEADS, RET_DK, RET_DV), F32)),
        grid=(nseq, nchunk),
        in_specs=[
            pl.BlockSpec(memory_space=pltpu.SMEM),
            pl.BlockSpec((chunk, RET_QK), lambda b, c: (row(b, c), 0)),
            pl.BlockSpec((chunk, RET_QK), lambda b, c: (row(b, c), 1)),
            pl.BlockSpec((chunk, RET_VW), lambda b, c: (row(b, c), 1)),
            pl.BlockSpec((chunk, RET_VW), lambda b, c: (row(b, c), 2)),
            st_spec,
            _resident((RET_HEADS, chunk, chunk)),
            _resident((RET_HEADS, chunk, LANES)),
            _resident((RET_HEADS, chunk, LANES)),
        ],
        out_specs=(pl.BlockSpec((chunk, RET_VW), lambda b, c: (row(b, c), 0)), st_spec),
        compiler_params=_params("parallel", "arbitrary"),
        name="ret_core",
    )(c_dec, proj, proj, proj, proj, s0, intra, q_dec, k_dec)


def _with_tail(body, tail_ref, o_ref):
    last = pl.num_programs(0) - 1

    @pl.when(pl.program_id(0) < last)
    def _():
        body()

    @pl.when(pl.program_id(0) == last)
    def _():
        o_ref[...] = jnp.zeros(o_ref.shape, o_ref.dtype)
        o_ref[0:tail_ref.shape[0], :] = tail_ref[...]


def _mm_res_kernel(a_ref, w_ref, x_ref, *rest):
    o_ref = rest[-1]

    def body():
        o_ref[...] = x_ref[...] + _dot(a_ref[...], w_ref[...])

    if len(rest) == 2:
        _with_tail(body, rest[0], o_ref)
    else:
        body()


def _mm_res(a, w, x, x_off, tm, tail=None):
    m, k = a.shape
    nblk = m // tm
    clamp = (lambda i: jnp.minimum(i, nblk - 1)) if tail is not None else (lambda i: i)
    in_specs = [
        pl.BlockSpec((tm, k), lambda i: (clamp(i), 0)),
        _resident((k, D_MODEL)),
        pl.BlockSpec((tm, D_MODEL), lambda i: (clamp(i) + x_off, 0)),
    ]
    args = [a, w, x]
    if tail is not None:
        in_specs.append(_resident(tail.shape))
        args.append(tail)
    steps = nblk + (tail is not None)
    return pl.pallas_call(
        _mm_res_kernel,
        out_shape=jax.ShapeDtypeStruct((steps * tm, D_MODEL), F32),
        grid=(steps,),
        in_specs=in_specs,
        out_specs=pl.BlockSpec((tm, D_MODEL), lambda i: (i, 0)),
        compiler_params=_params("arbitrary"),
        name="mm_res",
    )(*args)


ATT_PROJ_CHUNK = 4 * ATT_HD


def _att_proj_chunk(xn, w_ref, qg_ref, kg_ref, c, width):
    acc = _dot(xn, w_ref[:, c * ATT_PROJ_CHUNK:(c + 1) * ATT_PROJ_CHUNK])
    n_qk = 2 * (width // 3) // ATT_PROJ_CHUNK
    if c >= n_qk:
        return acc
    gain = qg_ref[...] if c < n_qk // 2 else kg_ref[...]
    heads = []
    for hh in range(ATT_PROJ_CHUNK // ATT_HD):
        a = acc[:, hh * ATT_HD:(hh + 1) * ATT_HD]
        heads.append(a * lax.rsqrt(jnp.mean(a * a, axis=-1, keepdims=True) + EPS) * gain)
    return jnp.concatenate(heads, axis=1)


def _att_proj_kernel(x_ref, g_ref, w_ref, qg_ref, kg_ref, o_ref):
    xn = _rmsnorm_rows(x_ref[...], g_ref[...]).astype(BF16)
    width = o_ref.shape[1]
    for c in range(width // ATT_PROJ_CHUNK):
        sl = slice(c * ATT_PROJ_CHUNK, (c + 1) * ATT_PROJ_CHUNK)
        o_ref[:, sl] = _att_proj_chunk(xn, w_ref, qg_ref, kg_ref, c, width).astype(o_ref.dtype)


def _att_proj_dil_kernel(x_ref, g_ref, w_ref, qg_ref, kg_ref, o_ref, xn_ref, *, picks):
    rows = x_ref.shape[0]
    k = 0
    for mids in picks:
        for s in mids:
            xn_ref[k * rows:(k + 1) * rows, :] = _rmsnorm_rows(x_ref[:, s, :], g_ref[...]).astype(BF16)
            k += 1
    per = rows * len(picks[0])
    width = o_ref.shape[3]
    xn = xn_ref[...]
    for c in range(width // ATT_PROJ_CHUNK):
        sl = slice(c * ATT_PROJ_CHUNK, (c + 1) * ATT_PROJ_CHUNK)
        val = _att_proj_chunk(xn, w_ref, qg_ref, kg_ref, c, width).astype(o_ref.dtype)
        for r in range(len(picks)):
            o_ref[0, r, :, sl] = val[r * per:(r + 1) * per]


def _residue_picks(dil):
    if dil % 8 == 0:
        return dil, [[r] for r in range(dil)]
    assert 8 % dil == 0 and dil > 1
    return 8, [[r + dil * e for e in range(8 // dil)] for r in range(dil)]


def _att_proj_dil(x, nb, length, dil, gain, w, q_gain, k_gain):
    sub, picks = _residue_picks(dil)
    width = w.shape[1]
    rows = ATT_SPAN // 2
    per = rows * len(picks[0])
    ls = length // dil
    steps_per_seq = ls // per
    return pl.pallas_call(
        functools.partial(_att_proj_dil_kernel, picks=picks),
        out_shape=jax.ShapeDtypeStruct((nb, dil, ls, width), BF16),
        grid=(nb * steps_per_seq,),
        in_specs=[
            pl.BlockSpec((rows, sub, D_MODEL), lambda i: (i, 0, 0)),
            _resident((1, D_MODEL)),
            _resident((D_MODEL, width)),
            _resident((1, ATT_HD)),
            _resident((1, ATT_HD)),
        ],
        out_specs=pl.BlockSpec((1, dil, per, width), lambda i: (i // steps_per_seq, 0, i % steps_per_seq, 0)),
        scratch_shapes=[pltpu.VMEM((dil * per, D_MODEL), BF16)],
        compiler_params=_params("parallel"),
        name="att_proj_dil",
    )(x.reshape(x.shape[0] // sub, sub, D_MODEL), gain, w, q_gain, k_gain)


def _att_proj(x, row_off, rows, tm, gain, w, q_gain, k_gain, out_dtype):
    width = w.shape[1]
    return pl.pallas_call(
        _att_proj_kernel,
        out_shape=jax.ShapeDtypeStruct((rows, width), out_dtype),
        grid=(rows // tm,),
        in_specs=[
            pl.BlockSpec((tm, D_MODEL), lambda i: (i + row_off, 0)),
            _resident((1, D_MODEL)),
            _resident((D_MODEL, width)),
            _resident((1, ATT_HD)),
            _resident((1, ATT_HD)),
        ],
        out_specs=pl.BlockSpec((tm, width), lambda i: (i, 0)),
        compiler_params=_params("parallel"),
        name="att_proj",
    )(x, gain, w, q_gain, k_gain)


def _att_band_kernel(q_ref, kp_ref, kc_ref, vp_ref, vc_ref, bias_ref, o_ref, *, picks):
    n = pl.program_id(1)
    res_per_step = q_ref.shape[1]

    def put(r, val):
        if picks is None:
            o_ref[0] = val
        else:
            per = ATT_SPAN // len(picks[0])
            for e in range(len(picks[0])):
                o_ref[0, :, r + (picks[0][e] - picks[0][0]), :] = val[e * per:(e + 1) * per]

    col = lax.broadcasted_iota(jnp.int32, (ATT_SPAN, 2 * ATT_SPAN), 1)
    no_prev = jnp.logical_and(col < ATT_SPAN, n == 0)
    lane = lax.broadcasted_iota(jnp.int32, (ATT_SPAN, LANES), 1)
    for rr in range(res_per_step):
        r = rr if res_per_step > 1 else pl.program_id(2)
        lse_tile = jnp.zeros((ATT_SPAN, LANES), F32)
        heads = []
        for h in range(ATT_G):
            sl = slice(h * ATT_HD, (h + 1) * ATT_HD)
            kcat = jnp.concatenate([kp_ref[0, rr, :, sl], kc_ref[0, rr, :, sl]], axis=0)
            vcat = jnp.concatenate([vp_ref[0, rr, :, sl], vc_ref[0, rr, :, sl]], axis=0)
            s = _dot_nt(q_ref[0, rr, :, sl], kcat) + bias_ref[h]
            s = jnp.where(no_prev, NEG, s)
            m = jnp.max(s, axis=-1, keepdims=True)
            p = jnp.exp(s - m)
            den = jnp.sum(p, axis=-1, keepdims=True)
            heads.append(_dot(p.astype(BF16), vcat) / den)
            lse_tile = jnp.where(lane == h, m + jnp.log(den), lse_tile)
        put(r, jnp.concatenate(heads + [lse_tile], axis=1))


def _att_band(qkv, dil, bias):
    nb, _, ls, _ = qkv.shape
    if dil == 1:
        picks, res_per_step = None, 1
        out_shape = (nb, ls, OUT_LSE)
        out_spec = pl.BlockSpec((1, ATT_SPAN, OUT_LSE), lambda b, n, r: (b, n, 0))
    else:
        sub, picks = _residue_picks(dil)
        res_per_step = dil if sub == 8 else 1
        rows = ATT_SPAN * dil // sub
        out_shape = (nb, ls * dil // sub, sub, OUT_LSE)
        out_spec = pl.BlockSpec((1, rows, sub, OUT_LSE), lambda b, n, r: (b, n, 0, 0))
    blk = lambda cb, prev: pl.BlockSpec(
        (1, res_per_step, ATT_SPAN, ATT_W),
        (lambda b, n, r: (b, r, jnp.maximum(n - 1, 0), cb)) if prev else (lambda b, n, r: (b, r, n, cb)))
    out = pl.pallas_call(
        functools.partial(_att_band_kernel, picks=picks),
        out_shape=jax.ShapeDtypeStruct(out_shape, F32),
        grid=(nb, ls // ATT_SPAN, dil // res_per_step),
        in_specs=[blk(0, False), blk(1, True), blk(1, False), blk(2, True), blk(2, False),
                  _resident((ATT_G, ATT_SPAN, 2 * ATT_SPAN))],
        out_specs=out_spec,
        compiler_params=_params("parallel", "arbitrary", "arbitrary"),
        name="att_band",
    )(qkv, qkv, qkv, qkv, qkv, bias)
    return out.reshape(nb * ls * dil, OUT_LSE)


def _window_rows_kernel(k_ref, v_ref, o_ref):
    halves = o_ref.shape[2]
    per = o_ref.shape[1]
    for e in range(halves):
        for kv, ref in enumerate((k_ref, v_ref)):
            for h in range(ATT_G):
                o_ref[0, :, e, 0, kv, h, :] = ref[0, 0, e * per:(e + 1) * per, h * ATT_HD:(h + 1) * ATT_HD].astype(o_ref.dtype)


def _window_rows(qkv, dil, dtype):
    nb, _, ls, _ = qkv.shape
    halves = 1 if dil == 1 else len(_residue_picks(dil)[1][0])
    per = ATT_SPAN // halves
    last = ls // ATT_SPAN - 1
    out = pl.pallas_call(
        _window_rows_kernel,
        out_shape=jax.ShapeDtypeStruct((nb, per, halves, dil, 2, ATT_G, ATT_HD), dtype),
        grid=(nb, dil),
        in_specs=[pl.BlockSpec((1, 1, ATT_SPAN, ATT_W), lambda b, r: (b, r, last, 1)),
                  pl.BlockSpec((1, 1, ATT_SPAN, ATT_W), lambda b, r: (b, r, last, 2))],
        out_specs=pl.BlockSpec((1, per, halves, 1, 2, ATT_G, ATT_HD), lambda b, r: (b, 0, 0, r, 0, 0, 0)),
        compiler_params=_params("parallel", "parallel"),
        name="window_rows",
    )(qkv, qkv)
    return out.reshape(nb, ATT_SPAN * dil, 2, ATT_G, ATT_HD)


def _att_merge_kernel(o0_ref, o1_ref, o2_ref, w_ref, x_ref, tail_ref, out_ref):
    def body():
        acc = x_ref[...]
        refs = (o0_ref, o1_ref, o2_ref)
        for h in range(ATT_G):
            sl = slice(h * ATT_HD, (h + 1) * ATT_HD)
            lses = [r[:, ATT_W + h:ATT_W + h + 1] for r in refs]
            m = jnp.maximum(jnp.maximum(lses[0], lses[1]), lses[2])
            es = [jnp.exp(l - m) for l in lses]
            merged = (es[0] * o0_ref[:, sl] + es[1] * o1_ref[:, sl] + es[2] * o2_ref[:, sl]) / (es[0] + es[1] + es[2])
            acc = acc + _dot(merged.astype(BF16), w_ref[sl, :])
        out_ref[...] = acc

    _with_tail(body, tail_ref, out_ref)


def _att_merge(outs, w, x, rows, tm, tail):
    nblk = rows // tm
    row_spec = lambda width: pl.BlockSpec((tm, width), lambda i: (jnp.minimum(i, nblk - 1), 0))
    return pl.pallas_call(
        _att_merge_kernel,
        out_shape=jax.ShapeDtypeStruct(((nblk + 1) * tm, D_MODEL), F32),
        grid=(nblk + 1,),
        in_specs=[row_spec(OUT_LSE), row_spec(OUT_LSE), row_spec(OUT_LSE), _resident((ATT_W, D_MODEL)),
                  row_spec(D_MODEL), _resident(tail.shape)],
        out_specs=pl.BlockSpec((tm, D_MODEL), lambda i: (i, 0)),
        compiler_params=_params("arbitrary"),
        name="att_merge",
    )(*outs, w, x, tail)


def _t5_bucket(dist):
    max_exact = N_BUCKETS // 2
    d32 = jnp.maximum(dist, 1).astype(F32)
    large = max_exact + (jnp.log(d32 / max_exact) / jnp.log(MAX_DISTANCE / max_exact)
                         * (N_BUCKETS - max_exact)).astype(jnp.int32)
    return jnp.where(dist < max_exact, dist, jnp.minimum(large, N_BUCKETS - 1))


def _band_bias(tab, dil):
    a = jnp.arange(ATT_SPAN)[:, None]
    c = jnp.arange(2 * ATT_SPAN)[None, :]
    rel = a - c + ATT_SPAN
    bias = jnp.moveaxis(tab[_t5_bucket(jnp.clip(rel, 0, ATT_SPAN) * dil)], -1, 0)
    return jnp.where(((rel >= 0) & (rel <= ATT_SPAN))[None], bias, NEG)


def _att_sample_kernel(qkv_ref, c0_ref, c1_ref, c2_ref, b0_ref, b1_ref, b2_ref, o_ref, *, steps):
    for t in range(steps):
        outs, lses = [], []
        for g, (c_ref, b_ref) in enumerate(((c0_ref, b0_ref), (c1_ref, b1_ref), (c2_ref, b2_ref))):
            dil = ATT_GROUPS[g][1]
            q = qkv_ref[0, t, 0, g]
            if dil == 1:
                keys = jnp.concatenate([c_ref[0, t:, 0, 0], qkv_ref[0, :t + 1, 1, g]], axis=0)
                vals = jnp.concatenate([c_ref[0, t:, 0, 1], qkv_ref[0, :t + 1, 2, g]], axis=0)
            else:
                keys = jnp.concatenate([c_ref[0, :, t, 0], qkv_ref[0, t:t + 1, 1, g]], axis=0)
                vals = jnp.concatenate([c_ref[0, :, t, 1], qkv_ref[0, t:t + 1, 2, g]], axis=0)
            s = jnp.sum(keys * q[None], axis=-1, keepdims=True) + b_ref[...]
            m = jnp.max(s, axis=0)
            p = jnp.exp(s - m[None])
            den = jnp.sum(p, axis=0)
            outs.append(jnp.sum(p * vals, axis=0) / den)
            lses.append(m + jnp.log(den))
        m = jnp.maximum(jnp.maximum(lses[0], lses[1]), lses[2])
        es = [jnp.exp(l - m) for l in lses]
        o_ref[0, t] = (es[0] * outs[0] + es[1] * outs[1] + es[2] * outs[2]) / (es[0] + es[1] + es[2])


def _att_sample(qkv, caches, biases):
    nb, steps = qkv.shape[:2]
    assert steps <= min(d for _, d in ATT_GROUPS[1:]) and all(c.shape[1] == w for c, (w, _) in zip(caches, ATT_GROUPS))
    views, view_specs = [], []
    for c, (win, dil) in zip(caches, ATT_GROUPS):
        views.append(c.reshape(nb, ATT_SPAN, dil, 2, ATT_G, ATT_HD))
        view_specs.append(pl.BlockSpec((1, ATT_SPAN, min(dil, steps), 2, ATT_G, ATT_HD), lambda b: (b, 0, 0, 0, 0, 0)))
    return pl.pallas_call(
        functools.partial(_att_sample_kernel, steps=steps),
        out_shape=jax.ShapeDtypeStruct((nb, steps, ATT_G, ATT_HD), F32),
        grid=(nb,),
        in_specs=[pl.BlockSpec((1, steps, 3, ATT_NG, ATT_G, ATT_HD), lambda b: (b, 0, 0, 0, 0, 0)),
                  *view_specs,
                  *[_resident((ATT_SPAN + 1, ATT_G, LANES))] * ATT_NG],
        out_specs=pl.BlockSpec((1, steps, ATT_G, ATT_HD), lambda b: (b, 0, 0, 0)),
        compiler_params=_params("parallel"),
        name="att_sample",
    )(qkv, *views, *biases)


def _window_shift_kernel(new_ref, old_hbm, o_ref, sem, *, steps):
    b, c = pl.program_id(0), pl.program_id(1)
    rows = o_ref.shape[1]
    last = pl.num_programs(1) - 1

    @pl.when(c < last)
    def _():
        cp = pltpu.make_async_copy(old_hbm.at[b, pl.ds(c * rows + steps, rows)], o_ref.at[0], sem)
        cp.start()
        cp.wait()

    @pl.when(c == last)
    def _():
        cp = pltpu.make_async_copy(old_hbm.at[b, pl.ds(c * rows + steps, rows - steps)],
                                   o_ref.at[0, pl.ds(0, rows - steps)], sem)
        cp.start()
        o_ref[0, rows - steps:rows] = new_ref[0]
        cp.wait()


def _window_shift(old, new):
    nb, win = old.shape[:2]
    steps = new.shape[1]
    rows = min(win, WINDOW_SHIFT_ROWS)
    tail = old.shape[2:]
    return pl.pallas_call(
        functools.partial(_window_shift_kernel, steps=steps),
        out_shape=jax.ShapeDtypeStruct(old.shape, old.dtype),
        grid=(nb, win // rows),
        in_specs=[pl.BlockSpec((1, steps) + tail, lambda b, c: (b, 0, 0, 0, 0)), pl.BlockSpec(memory_space=pl.ANY)],
        out_specs=pl.BlockSpec((1, rows) + tail, lambda b, c: (b, c, 0, 0, 0)),
        scratch_shapes=[pltpu.SemaphoreType.DMA(())],
        compiler_params=_params("arbitrary", "arbitrary"),
        name="window_shift",
    )(new, old)


def _router_logits(xn, wr_ref, br_ref):
    return jnp.dot(xn, wr_ref[...], preferred_element_type=F32, precision=lax.Precision.HIGHEST) + br_ref[...]


def _router_kernel(x_ref, g_ref, wr_ref, br_ref, o_ref):
    logits = _router_logits(_rmsnorm_rows(x_ref[...], g_ref[...]), wr_ref, br_ref)
    lane = lax.broadcasted_iota(jnp.int32, logits.shape, 1)
    lg = jnp.where(lane < MOE_GROUPS, logits, NEG)
    mx = jnp.max(lg, axis=-1, keepdims=True)
    sel = jnp.min(jnp.where(lg == mx, lane, LANES), axis=-1, keepdims=True)
    o_ref[...] = jnp.broadcast_to(sel, o_ref.shape)


def _router(x, row_off, rows, tm, gain, wr, br):
    return pl.pallas_call(
        _router_kernel,
        out_shape=jax.ShapeDtypeStruct((rows, LANES), jnp.int32),
        grid=(rows // tm,),
        in_specs=[pl.BlockSpec((tm, D_MODEL), lambda i: (i + row_off, 0)), _resident((1, D_MODEL)),
                  _resident((D_MODEL, LANES)), _resident((1, LANES))],
        out_specs=pl.BlockSpec((tm, LANES), lambda i: (i, 0)),
        compiler_params=_params("parallel"),
        name="moe_router",
    )(x, gain, wr, br)


def _moe_dispatch_kernel(slot_ref, pad_ref, x_ref, xs_hbm, zero_ref, sem, *, n_rows):
    i = pl.program_id(0)
    tm = x_ref.shape[0]
    count = jnp.minimum(tm, n_rows - i * tm)

    def start(k, carry):
        pltpu.make_async_copy(x_ref.at[pl.ds(k, 1)], xs_hbm.at[pl.ds(slot_ref[i * tm + k], 1)], sem).start()
        return carry

    def wait(k, carry):
        pltpu.make_async_copy(x_ref.at[pl.ds(0, 1)], xs_hbm.at[pl.ds(0, 1)], sem).wait()
        return carry

    @pl.when(count == tm)
    def _():
        lax.fori_loop(0, tm, start, 0, unroll=DMA_ISSUE_UNROLL)

    @pl.when(count < tm)
    def _():
        lax.fori_loop(0, count, start, 0)

    @pl.when(i == pl.num_programs(0) - 1)
    def _():
        zero_ref[...] = jnp.zeros(zero_ref.shape, zero_ref.dtype)
        n_ranges = pad_ref.shape[0] // 2
        for g in range(n_ranges):
            def start_pad(k, carry):
                pltpu.make_async_copy(zero_ref.at[pl.ds(0, 1)], xs_hbm.at[pl.ds(pad_ref[g] + k, 1)], sem).start()
                return carry
            lax.fori_loop(0, pad_ref[n_ranges + g], start_pad, 0)
            lax.fori_loop(0, pad_ref[n_ranges + g], wait, 0)

    @pl.when(count == tm)
    def _():
        pltpu.make_async_copy(x_ref, xs_hbm.at[pl.ds(0, tm)], sem).wait()

    @pl.when(count < tm)
    def _():
        lax.fori_loop(0, count, wait, 0)


def _moe_dispatch(x, slot_of_row, pads, n_rows, n_slots, tm):
    return pl.pallas_call(
        functools.partial(_moe_dispatch_kernel, n_rows=n_rows),
        out_shape=jax.ShapeDtypeStruct((n_slots, D_MODEL), F32),
        grid_spec=pltpu.PrefetchScalarGridSpec(
            num_scalar_prefetch=2,
            grid=(pl.cdiv(n_rows, tm),),
            in_specs=[pl.BlockSpec((tm, D_MODEL), lambda i, s, p: (i, 0))],
            out_specs=pl.BlockSpec(memory_space=pl.ANY),
            scratch_shapes=[pltpu.VMEM((8, D_MODEL), F32), pltpu.SemaphoreType.DMA(())]),
        compiler_params=_params("arbitrary"),
        name="moe_dispatch",
    )(slot_of_row, pads, x)


def _moe_collect_kernel(slot_ref, ys_hbm, o_ref, sem, *, row_off):
    i = pl.program_id(0)
    tm = o_ref.shape[0]

    def start(k, carry):
        pltpu.make_async_copy(ys_hbm.at[pl.ds(slot_ref[row_off + i * tm + k], 1)], o_ref.at[pl.ds(k, 1)], sem).start()
        return carry

    lax.fori_loop(0, tm, start, 0, unroll=DMA_ISSUE_UNROLL)
    pltpu.make_async_copy(ys_hbm.at[pl.ds(0, tm)], o_ref, sem).wait()


def _moe_collect(ys, slot_of_row, row_off, rows, tm):
    return pl.pallas_call(
        functools.partial(_moe_collect_kernel, row_off=row_off),
        out_shape=jax.ShapeDtypeStruct((rows, D_MODEL), F32),
        grid_spec=pltpu.PrefetchScalarGridSpec(
            num_scalar_prefetch=1,
            grid=(rows // tm,),
            in_specs=[pl.BlockSpec(memory_space=pl.ANY)],
            out_specs=pl.BlockSpec((tm, D_MODEL), lambda i, s: (i, 0)),
            scratch_shapes=[pltpu.SemaphoreType.DMA(())]),
        compiler_params=_params("arbitrary"),
        name="moe_collect",
    )(slot_of_row, ys)


def _moe_expert_kernel(tg_ref, used_ref, x_ref, gain_ref, wr_ref, br_ref, wg_ref, wu_ref, wd_ref, y_ref):
    t = pl.program_id(0)

    @pl.when(t >= used_ref[0])
    def _():
        y_ref[...] = jnp.zeros(y_ref.shape, y_ref.dtype)

    @pl.when(t < used_ref[0])
    def _():
        _moe_expert_tile(tg_ref[t], x_ref, gain_ref, wr_ref, br_ref, wg_ref, wu_ref, wd_ref, y_ref)


def _moe_expert_tile(grp, x_ref, gain_ref, wr_ref, br_ref, wg_ref, wu_ref, wd_ref, y_ref):
    x = x_ref[...]
    xn = _rmsnorm_rows(x, gain_ref[...])
    logits = _router_logits(xn, wr_ref, br_ref)
    lane = lax.broadcasted_iota(jnp.int32, logits.shape, 1)
    is_group = lane < MOE_GROUPS
    mx = jnp.max(jnp.where(is_group, logits, NEG), axis=-1, keepdims=True)
    den = jnp.sum(jnp.where(is_group, jnp.exp(logits - mx), 0.0), axis=-1, keepdims=True)
    sel = jnp.sum(jnp.where(lane == grp, logits, 0.0), axis=-1, keepdims=True)
    p_group = jnp.exp(sel - mx) / den
    lo = MOE_GROUPS + MOE_PER_GROUP * grp
    le = jnp.where(jnp.logical_and(lane >= lo, lane < lo + MOE_PER_GROUP), logits, NEG)
    v1 = jnp.max(le, axis=-1, keepdims=True)
    i1 = jnp.min(jnp.where(le == v1, lane, LANES), axis=-1, keepdims=True)
    le = jnp.where(lane == i1, NEG, le)
    v2 = jnp.max(le, axis=-1, keepdims=True)
    i2 = jnp.min(jnp.where(le == v2, lane, LANES), axis=-1, keepdims=True)
    e2 = jnp.exp(v2 - v1)
    w1 = p_group / (1.0 + e2)
    w2 = p_group * e2 / (1.0 + e2)

    xb = xn.astype(BF16)
    acc = jnp.zeros(x.shape, F32)
    for e in range(MOE_PER_GROUP):
        ce = jnp.where(i1 == lo + e, w1, 0.0) + jnp.where(i2 == lo + e, w2, 0.0)
        hg = _dot(xb, wg_ref[e])
        hid = hg * jax.nn.sigmoid(hg) * _dot(xb, wu_ref[e])
        acc = acc + ce * _dot(hid.astype(BF16), wd_ref[e])
    y_ref[...] = x + acc


def _moe_plan(group, tile, n_tiles):
    onehot = (group[:, None] == jnp.arange(MOE_GROUPS)[None, :]).astype(jnp.int32)
    csum = jnp.cumsum(onehot, axis=0)
    counts = csum[-1]
    tiles_per = (counts + tile - 1) // tile
    tile_end = jnp.cumsum(tiles_per)
    tile_start = tile_end - tiles_per
    slot_of_row = jnp.sum(onehot * (tile_start[None, :] * tile + csum - 1), axis=1)
    tile_group = jnp.minimum(jnp.sum(jnp.arange(n_tiles)[:, None] >= tile_end[None, :], axis=1), MOE_GROUPS - 1)
    used = tile_end[-1:]
    pads = jnp.concatenate([tile_start * tile + counts, used * tile, tiles_per * tile - counts, (n_tiles - used) * tile])
    return (slot_of_row.astype(jnp.int32), tile_group.astype(jnp.int32), tile_end[-1:].astype(jnp.int32),
            pads.astype(jnp.int32))


def _moe(x, n_prompt, n_sample, gain, br, wr_f32, wg, wu, wd):
    n = n_prompt + n_sample
    tile = MOE_TILE
    n_tiles = n // tile + MOE_GROUPS
    group = _router(x, 0, x.shape[0], ROW_TILE, gain, wr_f32, br)[:n, 0]
    slot_of_row, tile_group, n_used, pads = _moe_plan(group, tile, n_tiles)
    xs = _moe_dispatch(x, slot_of_row, pads, n, n_tiles * tile, ROW_TILE)
    w_spec = lambda shape: pl.BlockSpec((MOE_PER_GROUP,) + shape, lambda t, tg, u: (tg[t], 0, 0))
    const = lambda shape: pl.BlockSpec(shape, lambda t, tg, u: (0,) * len(shape), pipeline_mode=pl.Buffered(1))
    ys = pl.pallas_call(
        _moe_expert_kernel,
        out_shape=jax.ShapeDtypeStruct((n_tiles * tile, D_MODEL), F32),
        grid_spec=pltpu.PrefetchScalarGridSpec(
            num_scalar_prefetch=2,
            grid=(n_tiles,),
            in_specs=[pl.BlockSpec((tile, D_MODEL), lambda t, tg, u: (jnp.minimum(t, u[0] - 1), 0)),
                      const((1, D_MODEL)), const((D_MODEL, LANES)), const((1, LANES)),
                      w_spec((D_MODEL, MOE_HIDDEN)), w_spec((D_MODEL, MOE_HIDDEN)), w_spec((MOE_HIDDEN, D_MODEL))],
            out_specs=pl.BlockSpec((tile, D_MODEL), lambda t, tg, u: (t, 0))),
        compiler_params=_params("arbitrary"),
        name="moe_experts",
    )(tile_group, n_used, xs, gain, wr_f32, br, wg, wu, wd)
    return (_moe_collect(ys, slot_of_row, 0, n_prompt, ROW_TILE),
            _moe_collect(ys, slot_of_row, n_prompt, n_sample, n_sample))


def _rope_tables(pos):
    half = RET_DK // 2
    inv = 1.0 / (ROPE_BASE ** jnp.linspace(0.0, 1.0, half, dtype=F32))
    ang = jnp.repeat(pos.astype(F32)[:, None] * inv[None, :], 2, axis=-1)
    sign = jnp.where(jnp.arange(RET_DK) % 2 == 0, -1.0, 1.0).astype(F32)
    return jnp.cos(ang), jnp.sin(ang) * sign


def kernel(x_prompt, x_sample, state_ret, cache_win0, cache_win1, cache_win2, rel_bias, norm_mix, norm_ffn, w_in_ret, w_out_ret, w_in_att, q_norm, k_norm, w_out_att, w_router_group, b_router_group, w_router_expert, b_router_expert, w_gate, w_up, w_down):
    nb_p, len_p, _ = x_prompt.shape
    nb_s, len_s, _ = x_sample.shape
    n_p, n_s = nb_p * len_p, nb_s * len_s
    n_all = n_p + n_s
    assert n_p % ROW_TILE == 0 and n_p % n_s == 0 and len_p % RET_CHUNK == 0 and len_s <= SAMPLE_PAD
    assert all(len_p % (ATT_SPAN * d) == 0 for _, d in ATT_GROUPS)
    caches = (cache_win0[0], cache_win1[0], cache_win2[0])

    xp = x_prompt.reshape(n_p, D_MODEL)
    xs = x_sample.reshape(n_s, D_MODEL)
    row = lambda v: v.reshape(1, -1).astype(F32)

    def moe_weights(i):
        wr = jnp.zeros((D_MODEL, LANES), F32)
        wr = wr.at[:, :MOE_GROUPS].set(w_router_group[i]).at[:, MOE_GROUPS:MOE_GROUPS + MOE_EXPERTS].set(w_router_expert[i])
        br = jnp.zeros((1, LANES), F32)
        br = br.at[0, :MOE_GROUPS].set(b_router_group[i]).at[0, MOE_GROUPS:MOE_GROUPS + MOE_EXPERTS].set(b_router_expert[i])
        return (row(norm_ffn[i]), br, wr, w_gate[i].astype(BF16), w_up[i].astype(BF16), w_down[i].astype(BF16))

    gain0 = row(norm_mix[0])
    w_in = w_in_ret[0].astype(BF16)
    w_out = w_out_ret[0].astype(BF16)
    cos_p, sin_p = _rope_tables(jnp.arange(len_p))
    pos_s = jnp.tile(PAST_LEN + jnp.arange(SAMPLE_PAD), nb_s)
    cos_s, sin_s = _rope_tables(pos_s)
    xs_pad = jnp.pad(x_sample, ((0, 0), (0, SAMPLE_PAD - len_s), (0, 0))).reshape(nb_s * SAMPLE_PAD, D_MODEL)

    proj_p = _ret_proj(xp, 0, n_p, ROW_TILE, gain0, w_in, cos_p, sin_p)
    proj_s = _ret_proj(xs_pad, 0, nb_s * SAMPLE_PAD, nb_s * SAMPLE_PAD, gain0, w_in, cos_s, sin_s)
    zero_state = jnp.zeros((nb_p, RET_HEADS, RET_DK, RET_DV), F32)
    o_p, ret_p = _ret_core(proj_p, zero_state, nb_p, len_p // RET_CHUNK, RET_CHUNK, RET_CHUNK)
    o_s, ret_s = _ret_core(proj_s, state_ret[0].astype(F32), nb_s, 1, SAMPLE_PAD, len_s)
    o_s = o_s.reshape(nb_s, SAMPLE_PAD, RET_VW)[:, :len_s].reshape(n_s, RET_VW)
    x1 = _mm_res(o_p, w_out, xp, 0, ROW_TILE, tail=_mm_res(o_s, w_out, xs, 0, n_s))
    x2, x2_s = _moe(x1, n_p, n_s, *moe_weights(0))

    gain1 = row(norm_mix[1])
    w_in = w_in_att[0].astype(BF16)
    w_out = w_out_att[0].astype(BF16)
    q_gain = row(q_norm[0]) * (ATT_HD ** -0.5)
    k_gain = row(k_norm[0])
    qkv_s = _att_proj(x2_s, 0, n_s, n_s, gain1, w_in, q_gain, k_gain, F32)
    qkv_s = qkv_s.reshape(nb_s, len_s, 3, ATT_NG, ATT_G, ATT_HD)

    outs, win_p, win_s, sample_bias = [], [], [], []
    for g, (win, dil) in enumerate(ATT_GROUPS):
        tab = rel_bias[:, g * ATT_G:(g + 1) * ATT_G].astype(F32)
        w_g = jnp.concatenate([w_in[:, (i * ATT_NG + g) * ATT_W:(i * ATT_NG + g + 1) * ATT_W] for i in range(3)], axis=1)
        bias = _band_bias(tab, dil)
        if dil == 1:
            qkv_g = _att_proj(x2, 0, n_p, ROW_TILE, gain1, w_g, q_gain, k_gain, BF16).reshape(nb_p, 1, len_p, 3 * ATT_W)
        else:
            qkv_g = _att_proj_dil(x2, nb_p, len_p, dil, gain1, w_g, q_gain, k_gain)
            halves = len(_residue_picks(dil)[1][0])
            if halves > 1:
                pos = jnp.arange(ATT_SPAN).reshape(ATT_SPAN // halves, halves).T.reshape(-1)
                bias = bias[:, pos][:, :, jnp.concatenate([pos, ATT_SPAN + pos])]
        outs.append(_att_band(qkv_g, dil, bias))
        win_p.append(_window_rows(qkv_g, dil, caches[g].dtype)[None])
        win_s.append(_window_shift(caches[g], qkv_s[:, :, 1:, g].astype(caches[g].dtype))[None])
        b = tab[_t5_bucket((ATT_SPAN - jnp.arange(ATT_SPAN + 1)) * dil)]
        sample_bias.append(jnp.broadcast_to(b[:, :, None], (ATT_SPAN + 1, ATT_G, LANES)))
    o_s = _att_sample(qkv_s, caches, sample_bias)
    x3_s = _mm_res(o_s.reshape(n_s, ATT_W).astype(BF16), w_out, x2_s, 0, n_s)
    x3 = _att_merge(outs, w_out, x2, n_p, ROW_TILE, x3_s)
    y_p, y_s = _moe(x3, n_p, n_s, *moe_weights(1))

    y_p = y_p.reshape(nb_p, len_p, D_MODEL)
    y_s = y_s.reshape(nb_s, len_s, D_MODEL)
    dt = state_ret.dtype
    return (y_p, y_s, ret_p[None].astype(dt), ret_s[None].astype(dt),
            win_p[0], win_s[0], win_p[1], win_s[1], win_p[2], win_s[2])
```

```python
import functools

import jax
import jax.numpy as jnp
from jax import lax
from jax.experimental import pallas as pl
from jax.experimental.pallas import tpu as pltpu

F32 = jnp.float32
BF16 = jnp.bfloat16

D_MODEL = 1024
PAST_LEN = 16384
RET_HEADS = 4
RET_DK = 256
RET_DV = 512
RET_QK = RET_HEADS * RET_DK
RET_VW = RET_HEADS * RET_DV
RET_IN = 2 * RET_QK + 2 * RET_VW
ROPE_BASE = 10000.0
ATT_GROUPS = ((128, 1), (512, 4), (2048, 16))
ATT_NG = len(ATT_GROUPS)
ATT_G = 8
ATT_HD = 128
ATT_W = ATT_G * ATT_HD
ATT_IN = 3 * ATT_NG * ATT_W
ATT_SPAN = 128
N_BUCKETS = 32
MAX_DISTANCE = 2048
MOE_GROUPS = 4
MOE_PER_GROUP = 4
MOE_EXPERTS = MOE_GROUPS * MOE_PER_GROUP
MOE_HIDDEN = 512
EPS = 1e-6
NEG = -1e30

LANES = 128
V7X_VMEM_LIMIT_BYTES = 56 * 1024 * 1024
ROW_TILE = 512
RET_CHUNK = 256
SAMPLE_PAD = 16
MOE_TILE = 512
WINDOW_SHIFT_ROWS = 256
WINDOW_SHIFT_DEPTH = 3
DMA_ISSUE_UNROLL = 8
ATT_BAND_UNITS = 4
OUT_LSE = ATT_W + LANES


def _params(*sem):
    return pltpu.CompilerParams(dimension_semantics=sem, vmem_limit_bytes=V7X_VMEM_LIMIT_BYTES)


def _resident(shape):
    return pl.BlockSpec(shape, lambda *_: (0,) * len(shape), pipeline_mode=pl.Buffered(1))


def _rmsnorm_rows(x, gain):
    return x * lax.rsqrt(jnp.mean(x * x, axis=-1, keepdims=True) + EPS) * gain


def _dot(a, b):
    return jnp.dot(a, b, preferred_element_type=F32)


def _dot_nt(a, b):
    return lax.dot_general(a, b, (((1,), (1,)), ((), ())), preferred_element_type=F32)


def _dot_tn(a, b):
    return lax.dot_general(a, b, (((0,), (0,)), ((), ())), preferred_element_type=F32)


def _ret_proj_kernel(x_ref, g_ref, w_ref, cos_ref, sin_ref, o_ref):
    xn = _rmsnorm_rows(x_ref[...], g_ref[...]).astype(BF16)
    cos = cos_ref[...]
    sin = sin_ref[...]
    even = (lax.broadcasted_iota(jnp.int32, cos.shape, 1) & 1) == 0
    for c in range(2 * RET_HEADS):
        sl = slice(c * RET_DK, (c + 1) * RET_DK)
        acc = _dot(xn, w_ref[:, sl])
        partner = jnp.where(even, pltpu.roll(acc, RET_DK - 1, 1), pltpu.roll(acc, 1, 1))
        r = acc * cos + partner * sin
        if c >= RET_HEADS:
            r = r * (RET_DK ** -0.5)
        o_ref[:, sl] = r.astype(o_ref.dtype)
    for c in range(2 * RET_QK // RET_DV, RET_IN // RET_DV):
        sl = slice(c * RET_DV, (c + 1) * RET_DV)
        o_ref[:, sl] = _dot(xn, w_ref[:, sl]).astype(o_ref.dtype)


def _ret_proj(x, row_off, rows, tm, gain, w, cos, sin):
    tab_blocks = cos.shape[0] // tm
    return pl.pallas_call(
        _ret_proj_kernel,
        out_shape=jax.ShapeDtypeStruct((rows, RET_IN), BF16),
        grid=(rows // tm,),
        in_specs=[
            pl.BlockSpec((tm, D_MODEL), lambda i: (i + row_off, 0)),
            _resident((1, D_MODEL)),
            _resident((D_MODEL, RET_IN)),
            pl.BlockSpec((tm, RET_DK), lambda i: (i % tab_blocks, 0)),
            pl.BlockSpec((tm, RET_DK), lambda i: (i % tab_blocks, 0)),
        ],
        out_specs=pl.BlockSpec((tm, RET_IN), lambda i: (i, 0)),
        compiler_params=_params("parallel"),
        name="ret_proj",
    )(x, gain, w, cos, sin)


def _ret_core_kernel(cdec_ref, q_ref, k_ref, v_ref, gt_ref, s0_ref, intra_ref, qdec_ref, kdec_ref, o_ref, st_ref):
    @pl.when(pl.program_id(1) == 0)
    def _():
        st_ref[...] = s0_ref[...]

    for h in range(RET_HEADS):
        qk = slice(h * RET_DK, (h + 1) * RET_DK)
        vv = slice(h * RET_DV, (h + 1) * RET_DV)
        qh, kh, vh = q_ref[:, qk], k_ref[:, qk], v_ref[:, vv]
        s = st_ref[0, h]
        sc = _dot_nt(qh, kh) * intra_ref[h]
        o = _dot(sc.astype(BF16), vh)
        o = o + _dot(qh, s.astype(BF16)) * jnp.concatenate([qdec_ref[h]] * (RET_DV // LANES), axis=1)
        kd = (kh.astype(F32) * jnp.concatenate([kdec_ref[h]] * (RET_DK // LANES), axis=1)).astype(BF16)
        st_ref[0, h] = s * cdec_ref[h] + _dot_tn(kd, vh)
        o = o * lax.rsqrt(jnp.mean(o * o, axis=-1, keepdims=True) + EPS)
        g = gt_ref[:, vv].astype(F32)
        o_ref[:, vv] = (o * (g * jax.nn.sigmoid(g))).astype(o_ref.dtype)


def _ret_decay_tables(chunk, n_real):
    log_g = jnp.log(1.0 - 2.0 ** (-5.0 - jnp.arange(RET_HEADS, dtype=F32)))
    idx = jnp.arange(chunk, dtype=F32)
    diff = idx[:, None] - idx[None, :]
    intra = jnp.where(diff >= 0, jnp.exp(jnp.maximum(diff, 0.0)[None] * log_g[:, None, None]), 0.0)
    q_dec = jnp.exp((idx + 1.0)[None, :] * log_g[:, None])
    k_dec = jnp.where(idx[None, :] < n_real, jnp.exp((n_real - 1.0 - idx)[None, :] * log_g[:, None]), 0.0)
    c_dec = jnp.exp(n_real * log_g)
    rep = lambda t: jnp.broadcast_to(t[:, :, None], (RET_HEADS, chunk, LANES))
    return c_dec, intra, rep(q_dec), rep(k_dec)


def _ret_core(proj, s0, nseq, nchunk, chunk, n_real):
    c_dec, intra, q_dec, k_dec = _ret_decay_tables(chunk, n_real)
    rows = nseq * nchunk * chunk
    row = lambda b, c: b * nchunk + c
    st_spec = pl.BlockSpec((1, RET_HEADS, RET_DK, RET_DV), lambda b, c: (b, 0, 0, 0))
    return pl.pallas_call(
        _ret_core_kernel,
        out_shape=(jax.ShapeDtypeStruct((rows, RET_VW), BF16),
                   jax.ShapeDtypeStruct((nseq, RET_HEADS, RET_DK, RET_DV), F32)),
        grid=(nseq, nchunk),
        in_specs=[
            pl.BlockSpec(memory_space=pltpu.SMEM),
            pl.BlockSpec((chunk, RET_QK), lambda b, c: (row(b, c), 0)),
            pl.BlockSpec((chunk, RET_QK), lambda b, c: (row(b, c), 1)),
            pl.BlockSpec((chunk, RET_VW), lambda b, c: (row(b, c), 1)),
            pl.BlockSpec((chunk, RET_VW), lambda b, c: (row(b, c), 2)),
            st_spec,
            _resident((RET_HEADS, chunk, chunk)),
            _resident((RET_HEADS, chunk, LANES)),
            _resident((RET_HEADS, chunk, LANES)),
        ],
        out_specs=(pl.BlockSpec((chunk, RET_VW), lambda b, c: (row(b, c), 0)), st_spec),
        compiler_params=_params("parallel", "arbitrary"),
        name="ret_core",
    )(c_dec, proj, proj, proj, proj, s0, intra, q_dec, k_dec)


def _with_tail(body, tail_ref, o_ref):
    last = pl.num_programs(0) - 1

    @pl.when(pl.program_id(0) < last)
    def _():
        body()

    @pl.when(pl.program_id(0) == last)
    def _():
        o_ref[...] = jnp.zeros(o_ref.shape, o_ref.dtype)
        o_ref[0:tail_ref.shape[0], :] = tail_ref[...]


def _mm_res_kernel(a_ref, w_ref, x_ref, *rest):
    o_ref = rest[-1]

    def body():
        o_ref[...] = x_ref[...] + _dot(a_ref[...], w_ref[...])

    if len(rest) == 2:
        _with_tail(body, rest[0], o_ref)
    else:
        body()


def _mm_res(a, w, x, x_off, tm, tail=None):
    m, k = a.shape
    nblk = m // tm
    clamp = (lambda i: jnp.minimum(i, nblk - 1)) if tail is not None else (lambda i: i)
    in_specs = [
        pl.BlockSpec((tm, k), lambda i: (clamp(i), 0)),
        _resident((k, D_MODEL)),
        pl.BlockSpec((tm, D_MODEL), lambda i: (clamp(i) + x_off, 0)),
    ]
    args = [a, w, x]
    if tail is not None:
        in_specs.append(_resident(tail.shape))
        args.append(tail)
    steps = nblk + (tail is not None)
    return pl.pallas_call(
        _mm_res_kernel,
        out_shape=jax.ShapeDtypeStruct((steps * tm, D_MODEL), F32),
        grid=(steps,),
        in_specs=in_specs,
        out_specs=pl.BlockSpec((tm, D_MODEL), lambda i: (i, 0)),
        compiler_params=_params("arbitrary"),
        name="mm_res",
    )(*args)


ATT_PROJ_CHUNK = 4 * ATT_HD


def _att_proj_chunk(xn, w_ref, qg_ref, kg_ref, c, width):
    acc = _dot(xn, w_ref[:, c * ATT_PROJ_CHUNK:(c + 1) * ATT_PROJ_CHUNK])
    n_qk = 2 * (width // 3) // ATT_PROJ_CHUNK
    if c >= n_qk:
        return acc
    gain = qg_ref[...] if c < n_qk // 2 else kg_ref[...]
    heads = []
    for hh in range(ATT_PROJ_CHUNK // ATT_HD):
        a = acc[:, hh * ATT_HD:(hh + 1) * ATT_HD]
        heads.append(a * lax.rsqrt(jnp.mean(a * a, axis=-1, keepdims=True) + EPS) * gain)
    return jnp.concatenate(heads, axis=1)


def _att_proj_kernel(x_ref, g_ref, w_ref, qg_ref, kg_ref, o_ref):
    xn = _rmsnorm_rows(x_ref[...], g_ref[...]).astype(BF16)
    width = o_ref.shape[1]
    for c in range(width // ATT_PROJ_CHUNK):
        sl = slice(c * ATT_PROJ_CHUNK, (c + 1) * ATT_PROJ_CHUNK)
        o_ref[:, sl] = _att_proj_chunk(xn, w_ref, qg_ref, kg_ref, c, width).astype(o_ref.dtype)


def _att_proj_dil_kernel(x_hbm, g_ref, w_ref, qg_ref, kg_ref, o_ref, xbuf, sem, *, picks, rows):
    i = pl.program_id(0)
    slot = i % 2

    def fetch(step, sl, action):
        k = 0
        for mids in picks:
            for s in mids:
                cp = pltpu.make_async_copy(x_hbm.at[pl.ds(step * rows, rows), s], xbuf.at[sl, pl.ds(k * rows, rows)],
                                           sem.at[sl])
                getattr(cp, action)()
                k += 1

    @pl.when(i == 0)
    def _():
        fetch(0, 0, "start")

    @pl.when(i + 1 < pl.num_programs(0))
    def _():
        fetch(i + 1, 1 - slot, "start")

    fetch(i, slot, "wait")
    per = rows * len(picks[0])
    width = o_ref.shape[3]
    xn = _rmsnorm_rows(xbuf[slot], g_ref[...]).astype(BF16)
    for c in range(width // ATT_PROJ_CHUNK):
        sl = slice(c * ATT_PROJ_CHUNK, (c + 1) * ATT_PROJ_CHUNK)
        val = _att_proj_chunk(xn, w_ref, qg_ref, kg_ref, c, width).astype(o_ref.dtype)
        for r in range(len(picks)):
            o_ref[0, r, :, sl] = val[r * per:(r + 1) * per]


def _residue_picks(dil):
    if dil % 8 == 0:
        return dil, [[r] for r in range(dil)]
    assert 8 % dil == 0 and dil > 1
    return 8, [[r + dil * e for e in range(8 // dil)] for r in range(dil)]


def _att_proj_dil(x, nb, length, dil, gain, w, q_gain, k_gain):
    sub, picks = _residue_picks(dil)
    width = w.shape[1]
    rows = ATT_SPAN // 2
    per = rows * len(picks[0])
    ls = length // dil
    steps_per_seq = ls // per
    return pl.pallas_call(
        functools.partial(_att_proj_dil_kernel, picks=picks, rows=rows),
        out_shape=jax.ShapeDtypeStruct((nb, dil, ls, width), BF16),
        grid=(nb * steps_per_seq,),
        in_specs=[
            pl.BlockSpec(memory_space=pl.ANY),
            _resident((1, D_MODEL)),
            _resident((D_MODEL, width)),
            _resident((1, ATT_HD)),
            _resident((1, ATT_HD)),
        ],
        out_specs=pl.BlockSpec((1, dil, per, width), lambda i: (i // steps_per_seq, 0, i % steps_per_seq, 0)),
        scratch_shapes=[pltpu.VMEM((2, dil * per, D_MODEL), F32), pltpu.SemaphoreType.DMA((2,))],
        compiler_params=_params("arbitrary"),
        name="att_proj_dil",
    )(x.reshape(x.shape[0] // sub, sub, D_MODEL), gain, w, q_gain, k_gain)


def _att_proj(x, row_off, rows, tm, gain, w, q_gain, k_gain, out_dtype):
    width = w.shape[1]
    return pl.pallas_call(
        _att_proj_kernel,
        out_shape=jax.ShapeDtypeStruct((rows, width), out_dtype),
        grid=(rows // tm,),
        in_specs=[
            pl.BlockSpec((tm, D_MODEL), lambda i: (i + row_off, 0)),
            _resident((1, D_MODEL)),
            _resident((D_MODEL, width)),
            _resident((1, ATT_HD)),
            _resident((1, ATT_HD)),
        ],
        out_specs=pl.BlockSpec((tm, width), lambda i: (i, 0)),
        compiler_params=_params("parallel"),
        name="att_proj",
    )(x, gain, w, q_gain, k_gain)


def _att_band_kernel(q_ref, kp_ref, kc_ref, vp_ref, vc_ref, bias_ref, o_ref, *scratch, picks, blocks):
    n = pl.program_id(1)
    res = q_ref.shape[1]
    col = lax.broadcasted_iota(jnp.int32, (ATT_SPAN, 2 * ATT_SPAN), 1)
    no_prev = jnp.logical_and(col < ATT_SPAN, n == 0)
    lane = lax.broadcasted_iota(jnp.int32, (ATT_SPAN, LANES), 1)

    def unit(rr, j):
        lse_tile = jnp.zeros((ATT_SPAN, LANES), F32)
        heads = []
        for h in range(ATT_G):
            sl = slice(h * ATT_HD, (h + 1) * ATT_HD)
            if j == 0:
                kcat = jnp.concatenate([kp_ref[0, rr, :, sl], kc_ref[0, rr, 0:ATT_SPAN, sl]], axis=0)
                vcat = jnp.concatenate([vp_ref[0, rr, :, sl], vc_ref[0, rr, 0:ATT_SPAN, sl]], axis=0)
            else:
                kcat = kc_ref[0, rr, (j - 1) * ATT_SPAN:(j + 1) * ATT_SPAN, sl]
                vcat = vc_ref[0, rr, (j - 1) * ATT_SPAN:(j + 1) * ATT_SPAN, sl]
            s = _dot_nt(q_ref[0, rr, j * ATT_SPAN:(j + 1) * ATT_SPAN, sl], kcat) + bias_ref[h]
            if j == 0:
                s = jnp.where(no_prev, NEG, s)
            m = jnp.max(s, axis=-1, keepdims=True)
            p = jnp.exp(s - m)
            den = jnp.sum(p, axis=-1, keepdims=True)
            heads.append(_dot(p.astype(BF16), vcat) / den)
            lse_tile = jnp.where(lane == h, m + jnp.log(den), lse_tile)
        return jnp.concatenate(heads + [lse_tile], axis=1)

    if picks is None:
        for j in range(blocks):
            o_ref[0, j * ATT_SPAN:(j + 1) * ATT_SPAN, :] = unit(0, j)
        return

    obuf, sem = scratch
    b, g = pl.program_id(0), pl.program_id(2)
    steps = pl.num_programs(0) * pl.num_programs(1) * pl.num_programs(2)
    flat = (b * pl.num_programs(1) + n) * pl.num_programs(2) + g
    slot = flat % 2
    per = ATT_SPAN // len(picks[0])

    def writeback(sl, action):
        for rr in range(res):
            for e, mid in enumerate(picks[0]):
                cp = pltpu.make_async_copy(obuf.at[sl, rr, pl.ds(e * per, per)],
                                           o_ref.at[b, pl.ds(n * per, per), g * res + rr + mid], sem.at[sl])
                getattr(cp, action)()

    @pl.when(flat >= 2)
    def _():
        writeback(slot, "wait")

    for rr in range(res):
        obuf[slot, rr] = unit(rr, 0)
    writeback(slot, "start")

    @pl.when(flat == steps - 1)
    def _():
        writeback(slot, "wait")

        @pl.when(steps >= 2)
        def _():
            writeback(1 - slot, "wait")


def _att_band(qkv, dil, bias):
    nb, _, ls, _ = qkv.shape
    units = ATT_BAND_UNITS
    if dil == 1:
        picks, res, blocks = None, 1, units
        out_shape = (nb, ls, OUT_LSE)
        out_spec = pl.BlockSpec((1, blocks * ATT_SPAN, OUT_LSE), lambda b, n, g: (b, n, 0))
        scratch = []
    else:
        sub, picks = _residue_picks(dil)
        res, blocks = units, 1
        out_shape = (nb, ls * dil // sub, sub, OUT_LSE)
        out_spec = pl.BlockSpec(memory_space=pl.ANY)
        scratch = [pltpu.VMEM((2, res, ATT_SPAN, OUT_LSE), F32), pltpu.SemaphoreType.DMA((2,))]
    cur = lambda cb: pl.BlockSpec((1, res, blocks * ATT_SPAN, ATT_W), lambda b, n, g: (b, g, n, cb))
    prev = lambda cb: pl.BlockSpec((1, res, ATT_SPAN, ATT_W), lambda b, n, g: (b, g, jnp.maximum(n * blocks - 1, 0), cb))
    out = pl.pallas_call(
        functools.partial(_att_band_kernel, picks=picks, blocks=blocks),
        out_shape=jax.ShapeDtypeStruct(out_shape, F32),
        grid=(nb, ls // (blocks * ATT_SPAN), dil // res),
        in_specs=[cur(0), prev(1), cur(1), prev(2), cur(2), _resident((ATT_G, ATT_SPAN, 2 * ATT_SPAN))],
        out_specs=out_spec,
        scratch_shapes=scratch,
        compiler_params=_params("arbitrary", "arbitrary", "arbitrary"),
        name="att_band",
    )(qkv, qkv, qkv, qkv, qkv, bias)
    return out.reshape(nb * ls * dil, OUT_LSE)


def _window_rows_kernel(k_ref, v_ref, o_ref):
    halves = o_ref.shape[2]
    per = o_ref.shape[1]
    for e in range(halves):
        for kv, ref in enumerate((k_ref, v_ref)):
            for h in range(ATT_G):
                o_ref[0, :, e, 0, kv, h, :] = ref[0, 0, e * per:(e + 1) * per, h * ATT_HD:(h + 1) * ATT_HD].astype(o_ref.dtype)


def _window_rows(qkv, dil, dtype):
    nb, _, ls, _ = qkv.shape
    halves = 1 if dil == 1 else len(_residue_picks(dil)[1][0])
    per = ATT_SPAN // halves
    last = ls // ATT_SPAN - 1
    out = pl.pallas_call(
        _window_rows_kernel,
        out_shape=jax.ShapeDtypeStruct((nb, per, halves, dil, 2, ATT_G, ATT_HD), dtype),
        grid=(nb, dil),
        in_specs=[pl.BlockSpec((1, 1, ATT_SPAN, ATT_W), lambda b, r: (b, r, last, 1)),
                  pl.BlockSpec((1, 1, ATT_SPAN, ATT_W), lambda b, r: (b, r, last, 2))],
        out_specs=pl.BlockSpec((1, per, halves, 1, 2, ATT_G, ATT_HD), lambda b, r: (b, 0, 0, r, 0, 0, 0)),
        compiler_params=_params("parallel", "parallel"),
        name="window_rows",
    )(qkv, qkv)
    return out.reshape(nb, ATT_SPAN * dil, 2, ATT_G, ATT_HD)


def _att_merge_kernel(o0_ref, o1_ref, o2_ref, w_ref, x_ref, tail_ref, out_ref):
    def body():
        acc = x_ref[...]
        refs = (o0_ref, o1_ref, o2_ref)
        for h in range(ATT_G):
            sl = slice(h * ATT_HD, (h + 1) * ATT_HD)
            lses = [r[:, ATT_W + h:ATT_W + h + 1] for r in refs]
            m = jnp.maximum(jnp.maximum(lses[0], lses[1]), lses[2])
            es = [jnp.exp(l - m) for l in lses]
            merged = (es[0] * o0_ref[:, sl] + es[1] * o1_ref[:, sl] + es[2] * o2_ref[:, sl]) / (es[0] + es[1] + es[2])
            acc = acc + _dot(merged.astype(BF16), w_ref[sl, :])
        out_ref[...] = acc

    _with_tail(body, tail_ref, out_ref)


def _att_merge(outs, w, x, rows, tm, tail):
    nblk = rows // tm
    row_spec = lambda width: pl.BlockSpec((tm, width), lambda i: (jnp.minimum(i, nblk - 1), 0))
    return pl.pallas_call(
        _att_merge_kernel,
        out_shape=jax.ShapeDtypeStruct(((nblk + 1) * tm, D_MODEL), F32),
        grid=(nblk + 1,),
        in_specs=[row_spec(OUT_LSE), row_spec(OUT_LSE), row_spec(OUT_LSE), _resident((ATT_W, D_MODEL)),
                  row_spec(D_MODEL), _resident(tail.shape)],
        out_specs=pl.BlockSpec((tm, D_MODEL), lambda i: (i, 0)),
        compiler_params=_params("arbitrary"),
        name="att_merge",
    )(*outs, w, x, tail)


def _t5_bucket(dist):
    max_exact = N_BUCKETS // 2
    d32 = jnp.maximum(dist, 1).astype(F32)
    large = max_exact + (jnp.log(d32 / max_exact) / jnp.log(MAX_DISTANCE / max_exact)
                         * (N_BUCKETS - max_exact)).astype(jnp.int32)
    return jnp.where(dist < max_exact, dist, jnp.minimum(large, N_BUCKETS - 1))


def _bucket_rows(tab, dist):
    onehot = jax.nn.one_hot(_t5_bucket(dist), N_BUCKETS, dtype=F32)
    return jnp.dot(onehot, tab, precision=lax.Precision.HIGHEST)


def _band_bias(tab, dil, halves):
    pos = jnp.arange(ATT_SPAN).reshape(ATT_SPAN // halves, halves).T.reshape(-1)
    a = pos[:, None]
    c = jnp.concatenate([pos, ATT_SPAN + pos])[None, :]
    rel = a - c + ATT_SPAN
    bias = jnp.moveaxis(_bucket_rows(tab, jnp.clip(rel, 0, ATT_SPAN) * dil), -1, 0)
    return jnp.where(((rel >= 0) & (rel <= ATT_SPAN))[None], bias, NEG)


def _att_sample_kernel(qkv_ref, c0_ref, c1_ref, c2_ref, b0_ref, b1_ref, b2_ref, o_ref, *, steps):
    for t in range(steps):
        outs, lses = [], []
        for g, (c_ref, b_ref) in enumerate(((c0_ref, b0_ref), (c1_ref, b1_ref), (c2_ref, b2_ref))):
            dil = ATT_GROUPS[g][1]
            q = qkv_ref[0, t, 0, g]
            if dil == 1:
                keys = jnp.concatenate([c_ref[0, t:, 0, 0], qkv_ref[0, :t + 1, 1, g]], axis=0)
                vals = jnp.concatenate([c_ref[0, t:, 0, 1], qkv_ref[0, :t + 1, 2, g]], axis=0)
            else:
                keys = jnp.concatenate([c_ref[0, :, t, 0], qkv_ref[0, t:t + 1, 1, g]], axis=0)
                vals = jnp.concatenate([c_ref[0, :, t, 1], qkv_ref[0, t:t + 1, 2, g]], axis=0)
            s = jnp.sum(keys * q[None], axis=-1, keepdims=True) + b_ref[...]
            m = jnp.max(s, axis=0)
            p = jnp.exp(s - m[None])
            den = jnp.sum(p, axis=0)
            outs.append(jnp.sum(p * vals, axis=0) / den)
            lses.append(m + jnp.log(den))
        m = jnp.maximum(jnp.maximum(lses[0], lses[1]), lses[2])
        es = [jnp.exp(l - m) for l in lses]
        o_ref[0, t] = (es[0] * outs[0] + es[1] * outs[1] + es[2] * outs[2]) / (es[0] + es[1] + es[2])


def _att_sample(qkv, caches, biases):
    nb, steps = qkv.shape[:2]
    assert steps <= min(d for _, d in ATT_GROUPS[1:]) and all(c.shape[1] == w for c, (w, _) in zip(caches, ATT_GROUPS))
    views, view_specs = [], []
    for c, (win, dil) in zip(caches, ATT_GROUPS):
        views.append(c.reshape(nb, ATT_SPAN, dil, 2, ATT_G, ATT_HD))
        view_specs.append(pl.BlockSpec((1, ATT_SPAN, min(dil, steps), 2, ATT_G, ATT_HD), lambda b: (b, 0, 0, 0, 0, 0)))
    return pl.pallas_call(
        functools.partial(_att_sample_kernel, steps=steps),
        out_shape=jax.ShapeDtypeStruct((nb, steps, ATT_G, ATT_HD), F32),
        grid=(nb,),
        in_specs=[pl.BlockSpec((1, steps, 3, ATT_NG, ATT_G, ATT_HD), lambda b: (b, 0, 0, 0, 0, 0)),
                  *view_specs,
                  *[_resident((ATT_SPAN + 1, ATT_G, LANES))] * ATT_NG],
        out_specs=pl.BlockSpec((1, steps, ATT_G, ATT_HD), lambda b: (b, 0, 0, 0)),
        compiler_params=_params("parallel"),
        name="att_sample",
    )(qkv, *views, *biases)


def _window_shift_kernel(new_ref, old_hbm, out_hbm, buf, rsem, wsem, *, steps, chunks):
    s = pl.program_id(0)
    total = pl.num_programs(0)
    depth, rows = buf.shape[:2]

    def read(step, action):
        b, c, sl = step // chunks, step % chunks, step % depth

        @pl.when(c < chunks - 1)
        def _():
            getattr(pltpu.make_async_copy(old_hbm.at[b, pl.ds(c * rows + steps, rows)], buf.at[sl], rsem.at[sl]), action)()

        @pl.when(c == chunks - 1)
        def _():
            getattr(pltpu.make_async_copy(old_hbm.at[b, pl.ds(c * rows + steps, rows - steps)],
                                          buf.at[sl, pl.ds(0, rows - steps)], rsem.at[sl]), action)()

    def write(step, action):
        b, c, sl = step // chunks, step % chunks, step % depth
        getattr(pltpu.make_async_copy(buf.at[sl], out_hbm.at[b, pl.ds(c * rows, rows)], wsem.at[sl]), action)()

    @pl.when(s == 0)
    def _():
        for ahead in range(depth - 1):
            @pl.when(ahead < total)
            def _():
                read(ahead, "start")

    @pl.when(s >= 1)
    def _():
        write(s - 1, "wait")

    @pl.when(s + depth - 1 < total)
    def _():
        read(s + depth - 1, "start")

    read(s, "wait")

    @pl.when(s % chunks == chunks - 1)
    def _():
        buf[s % depth, rows - steps:rows] = new_ref[0]

    write(s, "start")

    @pl.when(s == total - 1)
    def _():
        write(s, "wait")


def _window_shift(old, new):
    nb, win = old.shape[:2]
    steps = new.shape[1]
    rows = min(win, WINDOW_SHIFT_ROWS)
    chunks = win // rows
    tail = old.shape[2:]
    any_spec = pl.BlockSpec(memory_space=pl.ANY)
    return pl.pallas_call(
        functools.partial(_window_shift_kernel, steps=steps, chunks=chunks),
        out_shape=jax.ShapeDtypeStruct(old.shape, old.dtype),
        grid=(nb * chunks,),
        in_specs=[pl.BlockSpec((1, steps) + tail, lambda s: (s // chunks, 0, 0, 0, 0)), any_spec],
        out_specs=any_spec,
        scratch_shapes=[pltpu.VMEM((WINDOW_SHIFT_DEPTH, rows) + tail, old.dtype),
                        pltpu.SemaphoreType.DMA((WINDOW_SHIFT_DEPTH,)), pltpu.SemaphoreType.DMA((WINDOW_SHIFT_DEPTH,))],
        compiler_params=_params("arbitrary"),
        name="window_shift",
    )(new, old)


def _router_logits(xn, wr_ref, br_ref):
    xh = xn.astype(BF16)
    xl = (xn - xh.astype(F32)).astype(BF16)
    return _dot(xh, wr_ref[0]) + (_dot(xh, wr_ref[1]) + _dot(xl, wr_ref[0])) + br_ref[...]


def _router_kernel(x_ref, g_ref, wr_ref, br_ref, o_ref):
    logits = _router_logits(_rmsnorm_rows(x_ref[...], g_ref[...]), wr_ref, br_ref)
    lane = lax.broadcasted_iota(jnp.int32, logits.shape, 1)
    lg = jnp.where(lane < MOE_GROUPS, logits, NEG)
    mx = jnp.max(lg, axis=-1, keepdims=True)
    sel = jnp.min(jnp.where(lg == mx, lane, LANES), axis=-1, keepdims=True)
    o_ref[...] = jnp.broadcast_to(sel, o_ref.shape)


def _router(x, row_off, rows, tm, gain, wr, br):
    return pl.pallas_call(
        _router_kernel,
        out_shape=jax.ShapeDtypeStruct((rows, LANES), jnp.int32),
        grid=(rows // tm,),
        in_specs=[pl.BlockSpec((tm, D_MODEL), lambda i: (i + row_off, 0)), _resident((1, D_MODEL)),
                  _resident((2, D_MODEL, LANES)), _resident((1, LANES))],
        out_specs=pl.BlockSpec((tm, LANES), lambda i: (i, 0)),
        compiler_params=_params("parallel"),
        name="moe_router",
    )(x, gain, wr, br)


def _moe_dispatch_kernel(slot_ref, pad_ref, x_ref, xs_hbm, zero_ref, sem, *, n_rows):
    i = pl.program_id(0)
    tm = x_ref.shape[0]
    count = jnp.minimum(tm, n_rows - i * tm)

    def start(k, carry, priority=0):
        pltpu.make_async_copy(x_ref.at[pl.ds(k, 1)], xs_hbm.at[pl.ds(slot_ref[i * tm + k], 1)], sem).start(priority=priority)
        return carry

    def start_group(kg, carry):
        for u in range(DMA_ISSUE_UNROLL):
            start(kg * DMA_ISSUE_UNROLL + u, carry, priority=u % 2)
        return carry

    def wait(k, carry):
        pltpu.make_async_copy(x_ref.at[pl.ds(0, 1)], xs_hbm.at[pl.ds(0, 1)], sem).wait()
        return carry

    @pl.when(count == tm)
    def _():
        lax.fori_loop(0, tm // DMA_ISSUE_UNROLL, start_group, 0)

    @pl.when(count < tm)
    def _():
        lax.fori_loop(0, count, start, 0)

    @pl.when(i == pl.num_programs(0) - 1)
    def _():
        zero_ref[...] = jnp.zeros(zero_ref.shape, zero_ref.dtype)
        n_ranges = pad_ref.shape[0] // 2
        for g in range(n_ranges):
            def start_pad(k, carry):
                pltpu.make_async_copy(zero_ref.at[pl.ds(0, 1)], xs_hbm.at[pl.ds(pad_ref[g] + k, 1)], sem).start()
                return carry
            lax.fori_loop(0, pad_ref[n_ranges + g], start_pad, 0)
            lax.fori_loop(0, pad_ref[n_ranges + g], wait, 0)

    @pl.when(count == tm)
    def _():
        pltpu.make_async_copy(x_ref, xs_hbm.at[pl.ds(0, tm)], sem).wait()

    @pl.when(count < tm)
    def _():
        lax.fori_loop(0, count, wait, 0)


def _moe_dispatch(x, slot_of_row, pads, n_rows, n_slots, tm):
    return pl.pallas_call(
        functools.partial(_moe_dispatch_kernel, n_rows=n_rows),
        out_shape=jax.ShapeDtypeStruct((n_slots, D_MODEL), F32),
        grid_spec=pltpu.PrefetchScalarGridSpec(
            num_scalar_prefetch=2,
            grid=(pl.cdiv(n_rows, tm),),
            in_specs=[pl.BlockSpec((tm, D_MODEL), lambda i, s, p: (i, 0))],
            out_specs=pl.BlockSpec(memory_space=pl.ANY),
            scratch_shapes=[pltpu.VMEM((8, D_MODEL), F32), pltpu.SemaphoreType.DMA(())]),
        compiler_params=_params("arbitrary"),
        name="moe_dispatch",
    )(slot_of_row, pads, x)


def _moe_collect_kernel(slot_ref, ys_hbm, o_ref, sem, *, row_off):
    i = pl.program_id(0)
    tm = o_ref.shape[0]

    def start_group(kg, carry):
        for u in range(DMA_ISSUE_UNROLL):
            k = kg * DMA_ISSUE_UNROLL + u
            pltpu.make_async_copy(ys_hbm.at[pl.ds(slot_ref[row_off + i * tm + k], 1)], o_ref.at[pl.ds(k, 1)],
                                  sem).start(priority=u % 2)
        return carry

    lax.fori_loop(0, tm // DMA_ISSUE_UNROLL, start_group, 0)
    pltpu.make_async_copy(ys_hbm.at[pl.ds(0, tm)], o_ref, sem).wait()


def _moe_collect(ys, slot_of_row, row_off, rows, tm):
    return pl.pallas_call(
        functools.partial(_moe_collect_kernel, row_off=row_off),
        out_shape=jax.ShapeDtypeStruct((rows, D_MODEL), F32),
        grid_spec=pltpu.PrefetchScalarGridSpec(
            num_scalar_prefetch=1,
            grid=(rows // tm,),
            in_specs=[pl.BlockSpec(memory_space=pl.ANY)],
            out_specs=pl.BlockSpec((tm, D_MODEL), lambda i, s: (i, 0)),
            scratch_shapes=[pltpu.SemaphoreType.DMA(())]),
        compiler_params=_params("arbitrary"),
        name="moe_collect",
    )(slot_of_row, ys)


def _moe_expert_kernel(tg_ref, used_ref, x_ref, gain_ref, wr_ref, br_ref, wg_ref, wu_ref, wd_ref, y_ref):
    t = pl.program_id(0)

    @pl.when(t >= used_ref[0])
    def _():
        y_ref[...] = jnp.zeros(y_ref.shape, y_ref.dtype)

    @pl.when(t < used_ref[0])
    def _():
        _moe_expert_tile(tg_ref[t], x_ref, gain_ref, wr_ref, br_ref, wg_ref, wu_ref, wd_ref, y_ref)


def _moe_expert_tile(grp, x_ref, gain_ref, wr_ref, br_ref, wg_ref, wu_ref, wd_ref, y_ref):
    x = x_ref[...]
    xn = _rmsnorm_rows(x, gain_ref[...])
    logits = _router_logits(xn, wr_ref, br_ref)
    lane = lax.broadcasted_iota(jnp.int32, logits.shape, 1)
    is_group = lane < MOE_GROUPS
    mx = jnp.max(jnp.where(is_group, logits, NEG), axis=-1, keepdims=True)
    den = jnp.sum(jnp.where(is_group, jnp.exp(logits - mx), 0.0), axis=-1, keepdims=True)
    sel = jnp.sum(jnp.where(lane == grp, logits, 0.0), axis=-1, keepdims=True)
    p_group = jnp.exp(sel - mx) / den
    lo = MOE_GROUPS + MOE_PER_GROUP * grp
    le = jnp.where(jnp.logical_and(lane >= lo, lane < lo + MOE_PER_GROUP), logits, NEG)
    v1 = jnp.max(le, axis=-1, keepdims=True)
    i1 = jnp.min(jnp.where(le == v1, lane, LANES), axis=-1, keepdims=True)
    le = jnp.where(lane == i1, NEG, le)
    v2 = jnp.max(le, axis=-1, keepdims=True)
    i2 = jnp.min(jnp.where(le == v2, lane, LANES), axis=-1, keepdims=True)
    e2 = jnp.exp(v2 - v1)
    w1 = p_group / (1.0 + e2)
    w2 = p_group * e2 / (1.0 + e2)

    xb = xn.astype(BF16)
    acc = jnp.zeros(x.shape, F32)
    for e in range(MOE_PER_GROUP):
        ce = jnp.where(i1 == lo + e, w1, 0.0) + jnp.where(i2 == lo + e, w2, 0.0)
        hg = _dot(xb, wg_ref[e])
        hid = hg * jax.nn.sigmoid(hg) * _dot(xb, wu_ref[e])
        acc = acc + ce * _dot(hid.astype(BF16), wd_ref[e])
    y_ref[...] = x + acc


def _moe_plan(group, tile, n_tiles):
    onehot = (group[:, None] == jnp.arange(MOE_GROUPS)[None, :]).astype(jnp.int32)
    csum = jnp.cumsum(onehot, axis=0)
    counts = csum[-1]
    tiles_per = (counts + tile - 1) // tile
    tile_end = jnp.cumsum(tiles_per)
    tile_start = tile_end - tiles_per
    slot_of_row = jnp.sum(onehot * (tile_start[None, :] * tile + csum - 1), axis=1)
    tile_group = jnp.minimum(jnp.sum(jnp.arange(n_tiles)[:, None] >= tile_end[None, :], axis=1), MOE_GROUPS - 1)
    used = tile_end[-1:]
    pads = jnp.concatenate([tile_start * tile + counts, used * tile, tiles_per * tile - counts, (n_tiles - used) * tile])
    return (slot_of_row.astype(jnp.int32), tile_group.astype(jnp.int32), tile_end[-1:].astype(jnp.int32),
            pads.astype(jnp.int32))


def _moe(x, n_prompt, n_sample, gain, br, wr_f32, wg, wu, wd):
    n = n_prompt + n_sample
    tile = MOE_TILE
    n_tiles = n // tile + MOE_GROUPS
    group = _router(x, 0, x.shape[0], ROW_TILE, gain, wr_f32, br)[:n, 0]
    slot_of_row, tile_group, n_used, pads = _moe_plan(group, tile, n_tiles)
    xs = _moe_dispatch(x, slot_of_row, pads, n, n_tiles * tile, ROW_TILE)
    w_spec = lambda shape: pl.BlockSpec((MOE_PER_GROUP,) + shape, lambda t, tg, u: (tg[t], 0, 0))
    const = lambda shape: pl.BlockSpec(shape, lambda t, tg, u: (0,) * len(shape), pipeline_mode=pl.Buffered(1))
    ys = pl.pallas_call(
        _moe_expert_kernel,
        out_shape=jax.ShapeDtypeStruct((n_tiles * tile, D_MODEL), F32),
        grid_spec=pltpu.PrefetchScalarGridSpec(
            num_scalar_prefetch=2,
            grid=(n_tiles,),
            in_specs=[pl.BlockSpec((tile, D_MODEL), lambda t, tg, u: (jnp.minimum(t, u[0] - 1), 0)),
                      const((1, D_MODEL)), const((2, D_MODEL, LANES)), const((1, LANES)),
                      w_spec((D_MODEL, MOE_HIDDEN)), w_spec((D_MODEL, MOE_HIDDEN)), w_spec((MOE_HIDDEN, D_MODEL))],
            out_specs=pl.BlockSpec((tile, D_MODEL), lambda t, tg, u: (t, 0))),
        compiler_params=_params("arbitrary"),
        name="moe_experts",
    )(tile_group, n_used, xs, gain, wr_f32, br, wg, wu, wd)
    return (_moe_collect(ys, slot_of_row, 0, n_prompt, ROW_TILE),
            _moe_collect(ys, slot_of_row, n_prompt, n_sample, n_sample))


def _rope_tables(pos):
    half = RET_DK // 2
    inv = 1.0 / (ROPE_BASE ** jnp.linspace(0.0, 1.0, half, dtype=F32))
    ang = jnp.repeat(pos.astype(F32)[:, None] * inv[None, :], 2, axis=-1)
    sign = jnp.where(jnp.arange(RET_DK) % 2 == 0, -1.0, 1.0).astype(F32)
    return jnp.cos(ang), jnp.sin(ang) * sign


def kernel(x_prompt, x_sample, state_ret, cache_win0, cache_win1, cache_win2, rel_bias, norm_mix, norm_ffn, w_in_ret, w_out_ret, w_in_att, q_norm, k_norm, w_out_att, w_router_group, b_router_group, w_router_expert, b_router_expert, w_gate, w_up, w_down):
    nb_p, len_p, _ = x_prompt.shape
    nb_s, len_s, _ = x_sample.shape
    n_p, n_s = nb_p * len_p, nb_s * len_s
    n_all = n_p + n_s
    assert n_p % ROW_TILE == 0 and n_p % n_s == 0 and len_p % RET_CHUNK == 0 and len_s <= SAMPLE_PAD
    assert all(len_p % (ATT_SPAN * d) == 0 for _, d in ATT_GROUPS)
    caches = (cache_win0[0], cache_win1[0], cache_win2[0])

    xp = x_prompt.reshape(n_p, D_MODEL)
    xs = x_sample.reshape(n_s, D_MODEL)
    row = lambda v: v.reshape(1, -1).astype(F32)

    def moe_weights(i):
        wr = jnp.zeros((D_MODEL, LANES), F32)
        wr = wr.at[:, :MOE_GROUPS].set(w_router_group[i]).at[:, MOE_GROUPS:MOE_GROUPS + MOE_EXPERTS].set(w_router_expert[i])
        br = jnp.zeros((1, LANES), F32)
        br = br.at[0, :MOE_GROUPS].set(b_router_group[i]).at[0, MOE_GROUPS:MOE_GROUPS + MOE_EXPERTS].set(b_router_expert[i])
        wr_hi = wr.astype(BF16)
        wr_pair = jnp.stack([wr_hi, (wr - wr_hi.astype(F32)).astype(BF16)])
        return (row(norm_ffn[i]), br, wr_pair, w_gate[i].astype(BF16), w_up[i].astype(BF16), w_down[i].astype(BF16))

    gain0 = row(norm_mix[0])
    w_in = w_in_ret[0].astype(BF16)
    w_out = w_out_ret[0].astype(BF16)
    cos_p, sin_p = _rope_tables(jnp.arange(len_p))
    pos_s = jnp.tile(PAST_LEN + jnp.arange(SAMPLE_PAD), nb_s)
    cos_s, sin_s = _rope_tables(pos_s)
    xs_pad = jnp.pad(x_sample, ((0, 0), (0, SAMPLE_PAD - len_s), (0, 0))).reshape(nb_s * SAMPLE_PAD, D_MODEL)

    proj_p = _ret_proj(xp, 0, n_p, ROW_TILE, gain0, w_in, cos_p, sin_p)
    proj_s = _ret_proj(xs_pad, 0, nb_s * SAMPLE_PAD, nb_s * SAMPLE_PAD, gain0, w_in, cos_s, sin_s)
    zero_state = jnp.zeros((nb_p, RET_HEADS, RET_DK, RET_DV), F32)
    o_p, ret_p = _ret_core(proj_p, zero_state, nb_p, len_p // RET_CHUNK, RET_CHUNK, RET_CHUNK)
    o_s, ret_s = _ret_core(proj_s, state_ret[0].astype(F32), nb_s, 1, SAMPLE_PAD, len_s)
    o_s = o_s.reshape(nb_s, SAMPLE_PAD, RET_VW)[:, :len_s].reshape(n_s, RET_VW)
    x1 = _mm_res(o_p, w_out, xp, 0, ROW_TILE, tail=_mm_res(o_s, w_out, xs, 0, n_s))
    x2, x2_s = _moe(x1, n_p, n_s, *moe_weights(0))

    gain1 = row(norm_mix[1])
    w_in = w_in_att[0].astype(BF16)
    w_out = w_out_att[0].astype(BF16)
    q_gain = row(q_norm[0]) * (ATT_HD ** -0.5)
    k_gain = row(k_norm[0])
    qkv_s = _att_proj(x2_s, 0, n_s, n_s, gain1, w_in, q_gain, k_gain, F32)
    qkv_s = qkv_s.reshape(nb_s, len_s, 3, ATT_NG, ATT_G, ATT_HD)

    outs, win_p, win_s, sample_bias = [], [], [], []
    for g, (win, dil) in enumerate(ATT_GROUPS):
        tab = rel_bias[:, g * ATT_G:(g + 1) * ATT_G].astype(F32)
        w_g = jnp.concatenate([w_in[:, (i * ATT_NG + g) * ATT_W:(i * ATT_NG + g + 1) * ATT_W] for i in range(3)], axis=1)
        if dil == 1:
            halves = 1
            qkv_g = _att_proj(x2, 0, n_p, ROW_TILE, gain1, w_g, q_gain, k_gain, BF16).reshape(nb_p, 1, len_p, 3 * ATT_W)
        else:
            halves = len(_residue_picks(dil)[1][0])
            qkv_g = _att_proj_dil(x2, nb_p, len_p, dil, gain1, w_g, q_gain, k_gain)
        outs.append(_att_band(qkv_g, dil, _band_bias(tab, dil, halves)))
        win_p.append(_window_rows(qkv_g, dil, caches[g].dtype)[None])
        win_s.append(_window_shift(caches[g], qkv_s[:, :, 1:, g].astype(caches[g].dtype))[None])
        b = _bucket_rows(tab, (ATT_SPAN - jnp.arange(ATT_SPAN + 1)) * dil)
        sample_bias.append(jnp.broadcast_to(b[:, :, None], (ATT_SPAN + 1, ATT_G, LANES)))
    o_s = _att_sample(qkv_s, caches, sample_bias)
    x3_s = _mm_res(o_s.reshape(n_s, ATT_W).astype(BF16), w_out, x2_s, 0, n_s)
    x3 = _att_merge(outs, w_out, x2, n_p, ROW_TILE, x3_s)
    y_p, y_s = _moe(x3, n_p, n_s, *moe_weights(1))

    y_p = y_p.reshape(nb_p, len_p, D_MODEL)
    y_s = y_s.reshape(nb_s, len_s, D_MODEL)
    dt = state_ret.dtype
    return (y_p, y_s, ret_p[None].astype(dt), ret_s[None].astype(dt),
            win_p[0], win_s[0], win_p[1], win_s[1], win_p[2], win_s[2])
```

```python
import functools

import jax
import jax.numpy as jnp
from jax import lax
from jax.experimental import pallas as pl
from jax.experimental.pallas import tpu as pltpu

F32 = jnp.float32
BF16 = jnp.bfloat16

D_MODEL = 1024
PAST_LEN = 16384
RET_HEADS = 4
RET_DK = 256
RET_DV = 512
RET_QK = RET_HEADS * RET_DK
RET_VW = RET_HEADS * RET_DV
RET_IN = 2 * RET_QK + 2 * RET_VW
ROPE_BASE = 10000.0
ATT_GROUPS = ((128, 1), (512, 4), (2048, 16))
ATT_NG = len(ATT_GROUPS)
ATT_G = 8
ATT_HD = 128
ATT_W = ATT_G * ATT_HD
ATT_IN = 3 * ATT_NG * ATT_W
ATT_SPAN = 128
N_BUCKETS = 32
MAX_DISTANCE = 2048
MOE_GROUPS = 4
MOE_PER_GROUP = 4
MOE_EXPERTS = MOE_GROUPS * MOE_PER_GROUP
MOE_HIDDEN = 512
EPS = 1e-6
NEG = -1e30

LANES = 128
V7X_VMEM_LIMIT_BYTES = 56 * 1024 * 1024
ROW_TILE = 512
RET_CHUNK = 256
SAMPLE_PAD = 16
MOE_TILE = 512
WINDOW_SHIFT_ROWS = 256
WINDOW_SHIFT_DEPTH = 3
DMA_ISSUE_UNROLL = 8
ATT_BAND_UNITS = 4
OUT_LSE = ATT_W + LANES


def _params(*sem):
    return pltpu.CompilerParams(dimension_semantics=sem, vmem_limit_bytes=V7X_VMEM_LIMIT_BYTES)


def _resident(shape):
    return pl.BlockSpec(shape, lambda *_: (0,) * len(shape), pipeline_mode=pl.Buffered(1))


def _rmsnorm_rows(x, gain):
    return x * lax.rsqrt(jnp.mean(x * x, axis=-1, keepdims=True) + EPS) * gain


def _dot(a, b):
    return jnp.dot(a, b, preferred_element_type=F32)


def _dot_nt(a, b):
    return lax.dot_general(a, b, (((1,), (1,)), ((), ())), preferred_element_type=F32)


def _dot_tn(a, b):
    return lax.dot_general(a, b, (((0,), (0,)), ((), ())), preferred_element_type=F32)


def _ret_proj_kernel(x_ref, g_ref, w_ref, cos_ref, sin_ref, o_ref):
    xn = _rmsnorm_rows(x_ref[...], g_ref[...]).astype(BF16)
    cos = cos_ref[...]
    sin = sin_ref[...]
    even = (lax.broadcasted_iota(jnp.int32, cos.shape, 1) & 1) == 0
    for c in range(2 * RET_HEADS):
        sl = slice(c * RET_DK, (c + 1) * RET_DK)
        acc = _dot(xn, w_ref[:, sl])
        partner = jnp.where(even, pltpu.roll(acc, RET_DK - 1, 1), pltpu.roll(acc, 1, 1))
        r = acc * cos + partner * sin
        if c >= RET_HEADS:
            r = r * (RET_DK ** -0.5)
        o_ref[:, sl] = r.astype(o_ref.dtype)
    for c in range(2 * RET_QK // RET_DV, RET_IN // RET_DV):
        sl = slice(c * RET_DV, (c + 1) * RET_DV)
        o_ref[:, sl] = _dot(xn, w_ref[:, sl]).astype(o_ref.dtype)


def _ret_proj(x, row_off, rows, tm, gain, w, cos, sin):
    tab_blocks = cos.shape[0] // tm
    return pl.pallas_call(
        _ret_proj_kernel,
        out_shape=jax.ShapeDtypeStruct((rows, RET_IN), BF16),
        grid=(rows // tm,),
        in_specs=[
            pl.BlockSpec((tm, D_MODEL), lambda i: (i + row_off, 0)),
            _resident((1, D_MODEL)),
            _resident((D_MODEL, RET_IN)),
            pl.BlockSpec((tm, RET_DK), lambda i: (i % tab_blocks, 0)),
            pl.BlockSpec((tm, RET_DK), lambda i: (i % tab_blocks, 0)),
        ],
        out_specs=pl.BlockSpec((tm, RET_IN), lambda i: (i, 0)),
        compiler_params=_params("parallel"),
        name="ret_proj",
    )(x, gain, w, cos, sin)


def _ret_core_kernel(cdec_ref, q_ref, k_ref, v_ref, gt_ref, s0_ref, intra_ref, qdec_ref, kdec_ref, o_ref, st_ref):
    @pl.when(pl.program_id(1) == 0)
    def _():
        st_ref[...] = s0_ref[...]

    for h in range(RET_HEADS):
        qk = slice(h * RET_DK, (h + 1) * RET_DK)
        vv = slice(h * RET_DV, (h + 1) * RET_DV)
        qh, kh, vh = q_ref[:, qk], k_ref[:, qk], v_ref[:, vv]
        s = st_ref[0, h]
        sc = _dot_nt(qh, kh) * intra_ref[h]
        o = _dot(sc.astype(BF16), vh)
        o = o + _dot(qh, s.astype(BF16)) * jnp.concatenate([qdec_ref[h]] * (RET_DV // LANES), axis=1)
        kd = (kh.astype(F32) * jnp.concatenate([kdec_ref[h]] * (RET_DK // LANES), axis=1)).astype(BF16)
        st_ref[0, h] = s * cdec_ref[h] + _dot_tn(kd, vh)
        o = o * lax.rsqrt(jnp.mean(o * o, axis=-1, keepdims=True) + EPS)
        g = gt_ref[:, vv].astype(F32)
        o_ref[:, vv] = (o * (g * jax.nn.sigmoid(g))).astype(o_ref.dtype)


def _ret_decay_tables(chunk, n_real):
    log_g = jnp.log(1.0 - 2.0 ** (-5.0 - jnp.arange(RET_HEADS, dtype=F32)))
    idx = jnp.arange(chunk, dtype=F32)
    diff = idx[:, None] - idx[None, :]
    intra = jnp.where(diff >= 0, jnp.exp(jnp.maximum(diff, 0.0)[None] * log_g[:, None, None]), 0.0)
    q_dec = jnp.exp((idx + 1.0)[None, :] * log_g[:, None])
    k_dec = jnp.where(idx[None, :] < n_real, jnp.exp((n_real - 1.0 - idx)[None, :] * log_g[:, None]), 0.0)
    c_dec = jnp.exp(n_real * log_g)
    rep = lambda t: jnp.broadcast_to(t[:, :, None], (RET_HEADS, chunk, LANES))
    return c_dec, intra, rep(q_dec), rep(k_dec)


def _ret_core(proj, s0, nseq, nchunk, chunk, n_real):
    c_dec, intra, q_dec, k_dec = _ret_decay_tables(chunk, n_real)
    rows = nseq * nchunk * chunk
    row = lambda b, c: b * nchunk + c
    st_spec = pl.BlockSpec((1, RET_HEADS, RET_DK, RET_DV), lambda b, c: (b, 0, 0, 0))
    return pl.pallas_call(
        _ret_core_kernel,
        out_shape=(jax.ShapeDtypeStruct((rows, RET_VW), BF16),
                   jax.ShapeDtypeStruct((nseq, RET_HEADS, RET_DK, RET_DV), F32)),
        grid=(nseq, nchunk),
        in_specs=[
            pl.BlockSpec(memory_space=pltpu.SMEM),
            pl.BlockSpec((chunk, RET_QK), lambda b, c: (row(b, c), 0)),
            pl.BlockSpec((chunk, RET_QK), lambda b, c: (row(b, c), 1)),
            pl.BlockSpec((chunk, RET_VW), lambda b, c: (row(b, c), 1)),
            pl.BlockSpec((chunk, RET_VW), lambda b, c: (row(b, c), 2)),
            st_spec,
            _resident((RET_HEADS, chunk, chunk)),
            _resident((RET_HEADS, chunk, LANES)),
            _resident((RET_HEADS, chunk, LANES)),
        ],
        out_specs=(pl.BlockSpec((chunk, RET_VW), lambda b, c: (row(b, c), 0)), st_spec),
        compiler_params=_params("parallel", "arbitrary"),
        name="ret_core",
    )(c_dec, proj, proj, proj, proj, s0, intra, q_dec, k_dec)


def _with_tail_and_buckets(rows_fn, tail_ref, route_refs, o_ref, b_ref):
    last = pl.num_programs(0) - 1

    def emit(rows):
        o_ref[...] = rows
        b_ref[...] = _bucket_ids(rows, *route_refs)

    @pl.when(pl.program_id(0) < last)
    def _():
        emit(rows_fn())

    @pl.when(pl.program_id(0) == last)
    def _():
        pad = jnp.zeros((o_ref.shape[0] - tail_ref.shape[0], o_ref.shape[1]), o_ref.dtype)
        emit(jnp.concatenate([tail_ref[...], pad], axis=0))


def _route_specs():
    return [_resident((1, D_MODEL)), _resident((2, D_MODEL, LANES)), _resident((1, LANES))]


def _mm_res_kernel(a_ref, w_ref, x_ref, *rest):
    rows_fn = lambda: x_ref[...] + _dot(a_ref[...], w_ref[...])
    if len(rest) == 1:
        rest[0][...] = rows_fn()
    else:
        tail_ref, *route_refs, o_ref, b_ref = rest
        _with_tail_and_buckets(rows_fn, tail_ref, route_refs, o_ref, b_ref)


def _mm_res(a, w, x, x_off, tm, tail=None, route=None):
    m, k = a.shape
    nblk = m // tm
    clamp = (lambda i: jnp.minimum(i, nblk - 1)) if tail is not None else (lambda i: i)
    in_specs = [
        pl.BlockSpec((tm, k), lambda i: (clamp(i), 0)),
        _resident((k, D_MODEL)),
        pl.BlockSpec((tm, D_MODEL), lambda i: (clamp(i) + x_off, 0)),
    ]
    args = [a, w, x]
    steps = nblk
    out_shape = jax.ShapeDtypeStruct((nblk * tm, D_MODEL), F32)
    out_specs = pl.BlockSpec((tm, D_MODEL), lambda i: (i, 0))
    if tail is not None:
        steps = nblk + 1
        in_specs += [_resident(tail.shape)] + _route_specs()
        args += [tail, *route]
        out_shape = (jax.ShapeDtypeStruct((steps * tm, D_MODEL), F32), jax.ShapeDtypeStruct((steps * tm, LANES), jnp.int32))
        out_specs = (out_specs, pl.BlockSpec((tm, LANES), lambda i: (i, 0)))
    return pl.pallas_call(
        _mm_res_kernel,
        out_shape=out_shape,
        grid=(steps,),
        in_specs=in_specs,
        out_specs=out_specs,
        compiler_params=_params("arbitrary"),
        name="mm_res",
    )(*args)


ATT_PROJ_CHUNK = 4 * ATT_HD


def _att_proj_chunk(xn, w_ref, qg_ref, kg_ref, c, width):
    acc = _dot(xn, w_ref[:, c * ATT_PROJ_CHUNK:(c + 1) * ATT_PROJ_CHUNK])
    n_qk = 2 * (width // 3) // ATT_PROJ_CHUNK
    if c >= n_qk:
        return acc
    gain = qg_ref[...] if c < n_qk // 2 else kg_ref[...]
    heads = []
    for hh in range(ATT_PROJ_CHUNK // ATT_HD):
        a = acc[:, hh * ATT_HD:(hh + 1) * ATT_HD]
        heads.append(a * lax.rsqrt(jnp.mean(a * a, axis=-1, keepdims=True) + EPS) * gain)
    return jnp.concatenate(heads, axis=1)


def _att_proj_kernel(x_ref, g_ref, w_ref, qg_ref, kg_ref, o_ref):
    xn = _rmsnorm_rows(x_ref[...], g_ref[...]).astype(BF16)
    width = o_ref.shape[1]
    for c in range(width // ATT_PROJ_CHUNK):
        sl = slice(c * ATT_PROJ_CHUNK, (c + 1) * ATT_PROJ_CHUNK)
        o_ref[:, sl] = _att_proj_chunk(xn, w_ref, qg_ref, kg_ref, c, width).astype(o_ref.dtype)


def _att_proj_dil_kernel(x_hbm, g_ref, w_ref, qg_ref, kg_ref, o_ref, xbuf, sem, *, picks, rows):
    i = pl.program_id(0)
    slot = i % 2

    def fetch(step, sl, action):
        k = 0
        for mids in picks:
            for s in mids:
                cp = pltpu.make_async_copy(x_hbm.at[pl.ds(step * rows, rows), s], xbuf.at[sl, pl.ds(k * rows, rows)],
                                           sem.at[sl])
                getattr(cp, action)()
                k += 1

    @pl.when(i == 0)
    def _():
        fetch(0, 0, "start")

    @pl.when(i + 1 < pl.num_programs(0))
    def _():
        fetch(i + 1, 1 - slot, "start")

    fetch(i, slot, "wait")
    per = rows * len(picks[0])
    width = o_ref.shape[3]
    xn = _rmsnorm_rows(xbuf[slot], g_ref[...]).astype(BF16)
    for c in range(width // ATT_PROJ_CHUNK):
        sl = slice(c * ATT_PROJ_CHUNK, (c + 1) * ATT_PROJ_CHUNK)
        val = _att_proj_chunk(xn, w_ref, qg_ref, kg_ref, c, width).astype(o_ref.dtype)
        for r in range(len(picks)):
            o_ref[0, r, :, sl] = val[r * per:(r + 1) * per]


def _residue_picks(dil):
    if dil % 8 == 0:
        return dil, [[r] for r in range(dil)]
    assert 8 % dil == 0 and dil > 1
    return 8, [[r + dil * e for e in range(8 // dil)] for r in range(dil)]


def _att_proj_dil(x, nb, length, dil, gain, w, q_gain, k_gain):
    sub, picks = _residue_picks(dil)
    width = w.shape[1]
    rows = ATT_SPAN // 2
    per = rows * len(picks[0])
    ls = length // dil
    steps_per_seq = ls // per
    return pl.pallas_call(
        functools.partial(_att_proj_dil_kernel, picks=picks, rows=rows),
        out_shape=jax.ShapeDtypeStruct((nb, dil, ls, width), BF16),
        grid=(nb * steps_per_seq,),
        in_specs=[
            pl.BlockSpec(memory_space=pl.ANY),
            _resident((1, D_MODEL)),
            _resident((D_MODEL, width)),
            _resident((1, ATT_HD)),
            _resident((1, ATT_HD)),
        ],
        out_specs=pl.BlockSpec((1, dil, per, width), lambda i: (i // steps_per_seq, 0, i % steps_per_seq, 0)),
        scratch_shapes=[pltpu.VMEM((2, dil * per, D_MODEL), F32), pltpu.SemaphoreType.DMA((2,))],
        compiler_params=_params("arbitrary"),
        name="att_proj_dil",
    )(x.reshape(x.shape[0] // sub, sub, D_MODEL), gain, w, q_gain, k_gain)


def _att_proj(x, row_off, rows, tm, gain, w, q_gain, k_gain, out_dtype):
    width = w.shape[1]
    return pl.pallas_call(
        _att_proj_kernel,
        out_shape=jax.ShapeDtypeStruct((rows, width), out_dtype),
        grid=(rows // tm,),
        in_specs=[
            pl.BlockSpec((tm, D_MODEL), lambda i: (i + row_off, 0)),
            _resident((1, D_MODEL)),
            _resident((D_MODEL, width)),
            _resident((1, ATT_HD)),
            _resident((1, ATT_HD)),
        ],
        out_specs=pl.BlockSpec((tm, width), lambda i: (i, 0)),
        compiler_params=_params("parallel"),
        name="att_proj",
    )(x, gain, w, q_gain, k_gain)


def _att_band_kernel(q_ref, kp_ref, kc_ref, vp_ref, vc_ref, bias_ref, o_ref, *scratch, picks, blocks):
    n = pl.program_id(1)
    res = q_ref.shape[1]
    col = lax.broadcasted_iota(jnp.int32, (ATT_SPAN, 2 * ATT_SPAN), 1)
    no_prev = jnp.logical_and(col < ATT_SPAN, n == 0)
    lane = lax.broadcasted_iota(jnp.int32, (ATT_SPAN, LANES), 1)

    def unit(rr, j):
        lse_tile = jnp.zeros((ATT_SPAN, LANES), F32)
        heads = []
        for h in range(ATT_G):
            sl = slice(h * ATT_HD, (h + 1) * ATT_HD)
            if j == 0:
                kcat = jnp.concatenate([kp_ref[0, rr, :, sl], kc_ref[0, rr, 0:ATT_SPAN, sl]], axis=0)
                vcat = jnp.concatenate([vp_ref[0, rr, :, sl], vc_ref[0, rr, 0:ATT_SPAN, sl]], axis=0)
            else:
                kcat = kc_ref[0, rr, (j - 1) * ATT_SPAN:(j + 1) * ATT_SPAN, sl]
                vcat = vc_ref[0, rr, (j - 1) * ATT_SPAN:(j + 1) * ATT_SPAN, sl]
            s = _dot_nt(q_ref[0, rr, j * ATT_SPAN:(j + 1) * ATT_SPAN, sl], kcat) + bias_ref[h]
            if j == 0:
                s = jnp.where(no_prev, NEG, s)
            m = jnp.max(s, axis=-1, keepdims=True)
            p = jnp.exp(s - m)
            den = jnp.sum(p, axis=-1, keepdims=True)
            heads.append(_dot(p.astype(BF16), vcat) / den)
            lse_tile = jnp.where(lane == h, m + jnp.log(den), lse_tile)
        return jnp.concatenate(heads + [lse_tile], axis=1)

    if picks is None:
        for j in range(blocks):
            o_ref[0, j * ATT_SPAN:(j + 1) * ATT_SPAN, :] = unit(0, j)
        return

    obuf, sem = scratch
    b, g = pl.program_id(0), pl.program_id(2)
    steps = pl.num_programs(0) * pl.num_programs(1) * pl.num_programs(2)
    flat = (b * pl.num_programs(1) + n) * pl.num_programs(2) + g
    slot = flat % 2
    per = ATT_SPAN // len(picks[0])

    def writeback(sl, action):
        for rr in range(res):
            for e, mid in enumerate(picks[0]):
                cp = pltpu.make_async_copy(obuf.at[sl, rr, pl.ds(e * per, per)],
                                           o_ref.at[b, pl.ds(n * per, per), g * res + rr + mid], sem.at[sl])
                getattr(cp, action)()

    @pl.when(flat >= 2)
    def _():
        writeback(slot, "wait")

    for rr in range(res):
        obuf[slot, rr] = unit(rr, 0)
    writeback(slot, "start")

    @pl.when(flat == steps - 1)
    def _():
        writeback(slot, "wait")

        @pl.when(steps >= 2)
        def _():
            writeback(1 - slot, "wait")


def _att_band(qkv, dil, bias):
    nb, _, ls, _ = qkv.shape
    units = ATT_BAND_UNITS
    if dil == 1:
        picks, res, blocks = None, 1, units
        out_shape = (nb, ls, OUT_LSE)
        out_spec = pl.BlockSpec((1, blocks * ATT_SPAN, OUT_LSE), lambda b, n, g: (b, n, 0))
        scratch = []
    else:
        sub, picks = _residue_picks(dil)
        res, blocks = units, 1
        out_shape = (nb, ls * dil // sub, sub, OUT_LSE)
        out_spec = pl.BlockSpec(memory_space=pl.ANY)
        scratch = [pltpu.VMEM((2, res, ATT_SPAN, OUT_LSE), F32), pltpu.SemaphoreType.DMA((2,))]
    cur = lambda cb: pl.BlockSpec((1, res, blocks * ATT_SPAN, ATT_W), lambda b, n, g: (b, g, n, cb))
    prev = lambda cb: pl.BlockSpec((1, res, ATT_SPAN, ATT_W), lambda b, n, g: (b, g, jnp.maximum(n * blocks - 1, 0), cb))
    out = pl.pallas_call(
        functools.partial(_att_band_kernel, picks=picks, blocks=blocks),
        out_shape=jax.ShapeDtypeStruct(out_shape, F32),
        grid=(nb, ls // (blocks * ATT_SPAN), dil // res),
        in_specs=[cur(0), prev(1), cur(1), prev(2), cur(2), _resident((ATT_G, ATT_SPAN, 2 * ATT_SPAN))],
        out_specs=out_spec,
        scratch_shapes=scratch,
        compiler_params=_params("arbitrary", "arbitrary", "arbitrary"),
        name="att_band",
    )(qkv, qkv, qkv, qkv, qkv, bias)
    return out.reshape(nb * ls * dil, OUT_LSE)


def _window_rows_kernel(k_ref, v_ref, o_ref):
    halves = o_ref.shape[2]
    per = o_ref.shape[1]
    for e in range(halves):
        for kv, ref in enumerate((k_ref, v_ref)):
            for h in range(ATT_G):
                o_ref[0, :, e, 0, kv, h, :] = ref[0, 0, e * per:(e + 1) * per, h * ATT_HD:(h + 1) * ATT_HD].astype(o_ref.dtype)


def _window_rows(qkv, dil, dtype):
    nb, _, ls, _ = qkv.shape
    halves = 1 if dil == 1 else len(_residue_picks(dil)[1][0])
    per = ATT_SPAN // halves
    last = ls // ATT_SPAN - 1
    out = pl.pallas_call(
        _window_rows_kernel,
        out_shape=jax.ShapeDtypeStruct((nb, per, halves, dil, 2, ATT_G, ATT_HD), dtype),
        grid=(nb, dil),
        in_specs=[pl.BlockSpec((1, 1, ATT_SPAN, ATT_W), lambda b, r: (b, r, last, 1)),
                  pl.BlockSpec((1, 1, ATT_SPAN, ATT_W), lambda b, r: (b, r, last, 2))],
        out_specs=pl.BlockSpec((1, per, halves, 1, 2, ATT_G, ATT_HD), lambda b, r: (b, 0, 0, r, 0, 0, 0)),
        compiler_params=_params("parallel", "parallel"),
        name="window_rows",
    )(qkv, qkv)
    return out.reshape(nb, ATT_SPAN * dil, 2, ATT_G, ATT_HD)


def _att_merge_kernel(o0_ref, o1_ref, o2_ref, w_ref, x_ref, tail_ref, *rest):
    *route_refs, out_ref, b_ref = rest

    def rows_fn():
        acc = x_ref[...]
        refs = (o0_ref, o1_ref, o2_ref)
        for h in range(ATT_G):
            sl = slice(h * ATT_HD, (h + 1) * ATT_HD)
            lses = [r[:, ATT_W + h:ATT_W + h + 1] for r in refs]
            m = jnp.maximum(jnp.maximum(lses[0], lses[1]), lses[2])
            es = [jnp.exp(l - m) for l in lses]
            merged = (es[0] * o0_ref[:, sl] + es[1] * o1_ref[:, sl] + es[2] * o2_ref[:, sl]) / (es[0] + es[1] + es[2])
            acc = acc + _dot(merged.astype(BF16), w_ref[sl, :])
        return acc

    _with_tail_and_buckets(rows_fn, tail_ref, route_refs, out_ref, b_ref)


def _att_merge(outs, w, x, rows, tm, tail, route):
    nblk = rows // tm
    row_spec = lambda width: pl.BlockSpec((tm, width), lambda i: (jnp.minimum(i, nblk - 1), 0))
    return pl.pallas_call(
        _att_merge_kernel,
        out_shape=(jax.ShapeDtypeStruct(((nblk + 1) * tm, D_MODEL), F32),
                   jax.ShapeDtypeStruct(((nblk + 1) * tm, LANES), jnp.int32)),
        grid=(nblk + 1,),
        in_specs=[row_spec(OUT_LSE), row_spec(OUT_LSE), row_spec(OUT_LSE), _resident((ATT_W, D_MODEL)),
                  row_spec(D_MODEL), _resident(tail.shape)] + _route_specs(),
        out_specs=(pl.BlockSpec((tm, D_MODEL), lambda i: (i, 0)), pl.BlockSpec((tm, LANES), lambda i: (i, 0))),
        compiler_params=_params("arbitrary"),
        name="att_merge",
    )(*outs, w, x, tail, *route)


def _t5_bucket(dist):
    max_exact = N_BUCKETS // 2
    d32 = jnp.maximum(dist, 1).astype(F32)
    large = max_exact + (jnp.log(d32 / max_exact) / jnp.log(MAX_DISTANCE / max_exact)
                         * (N_BUCKETS - max_exact)).astype(jnp.int32)
    return jnp.where(dist < max_exact, dist, jnp.minimum(large, N_BUCKETS - 1))


def _bucket_rows(tab, dist):
    onehot = jax.nn.one_hot(_t5_bucket(dist), N_BUCKETS, dtype=F32)
    return jnp.dot(onehot, tab, precision=lax.Precision.HIGHEST)


def _band_bias(tab, dil, halves):
    pos = jnp.arange(ATT_SPAN).reshape(ATT_SPAN // halves, halves).T.reshape(-1)
    a = pos[:, None]
    c = jnp.concatenate([pos, ATT_SPAN + pos])[None, :]
    rel = a - c + ATT_SPAN
    bias = jnp.moveaxis(_bucket_rows(tab, jnp.clip(rel, 0, ATT_SPAN) * dil), -1, 0)
    return jnp.where(((rel >= 0) & (rel <= ATT_SPAN))[None], bias, NEG)


def _att_sample_kernel(qkv_ref, c0_ref, c1_ref, c2_ref, b0_ref, b1_ref, b2_ref, o_ref, *, steps):
    for t in range(steps):
        outs, lses = [], []
        for g, (c_ref, b_ref) in enumerate(((c0_ref, b0_ref), (c1_ref, b1_ref), (c2_ref, b2_ref))):
            dil = ATT_GROUPS[g][1]
            q = qkv_ref[0, t, 0, g]
            if dil == 1:
                keys = jnp.concatenate([c_ref[0, t:, 0, 0], qkv_ref[0, :t + 1, 1, g]], axis=0)
                vals = jnp.concatenate([c_ref[0, t:, 0, 1], qkv_ref[0, :t + 1, 2, g]], axis=0)
            else:
                keys = jnp.concatenate([c_ref[0, :, t, 0], qkv_ref[0, t:t + 1, 1, g]], axis=0)
                vals = jnp.concatenate([c_ref[0, :, t, 1], qkv_ref[0, t:t + 1, 2, g]], axis=0)
            s = jnp.sum(keys * q[None], axis=-1, keepdims=True) + b_ref[...]
            m = jnp.max(s, axis=0)
            p = jnp.exp(s - m[None])
            den = jnp.sum(p, axis=0)
            outs.append(jnp.sum(p * vals, axis=0) / den)
            lses.append(m + jnp.log(den))
        m = jnp.maximum(jnp.maximum(lses[0], lses[1]), lses[2])
        es = [jnp.exp(l - m) for l in lses]
        o_ref[0, t] = (es[0] * outs[0] + es[1] * outs[1] + es[2] * outs[2]) / (es[0] + es[1] + es[2])


def _att_sample(qkv, caches, biases):
    nb, steps = qkv.shape[:2]
    assert steps <= min(d for _, d in ATT_GROUPS[1:]) and all(c.shape[1] == w for c, (w, _) in zip(caches, ATT_GROUPS))
    views, view_specs = [], []
    for c, (win, dil) in zip(caches, ATT_GROUPS):
        views.append(c.reshape(nb, ATT_SPAN, dil, 2, ATT_G, ATT_HD))
        view_specs.append(pl.BlockSpec((1, ATT_SPAN, min(dil, steps), 2, ATT_G, ATT_HD), lambda b: (b, 0, 0, 0, 0, 0)))
    return pl.pallas_call(
        functools.partial(_att_sample_kernel, steps=steps),
        out_shape=jax.ShapeDtypeStruct((nb, steps, ATT_G, ATT_HD), F32),
        grid=(nb,),
        in_specs=[pl.BlockSpec((1, steps, 3, ATT_NG, ATT_G, ATT_HD), lambda b: (b, 0, 0, 0, 0, 0)),
                  *view_specs,
                  *[_resident((ATT_SPAN + 1, ATT_G, LANES))] * ATT_NG],
        out_specs=pl.BlockSpec((1, steps, ATT_G, ATT_HD), lambda b: (b, 0, 0, 0)),
        compiler_params=_params("parallel"),
        name="att_sample",
    )(qkv, *views, *biases)


def _window_shift_kernel(new_ref, old_hbm, out_hbm, buf, rsem, wsem, *, steps, chunks):
    s = pl.program_id(0)
    total = pl.num_programs(0)
    depth, rows = buf.shape[:2]

    def read(step, action):
        b, c, sl = step // chunks, step % chunks, step % depth

        @pl.when(c < chunks - 1)
        def _():
            getattr(pltpu.make_async_copy(old_hbm.at[b, pl.ds(c * rows + steps, rows)], buf.at[sl], rsem.at[sl]), action)()

        @pl.when(c == chunks - 1)
        def _():
            getattr(pltpu.make_async_copy(old_hbm.at[b, pl.ds(c * rows + steps, rows - steps)],
                                          buf.at[sl, pl.ds(0, rows - steps)], rsem.at[sl]), action)()

    def write(step, action):
        b, c, sl = step // chunks, step % chunks, step % depth
        getattr(pltpu.make_async_copy(buf.at[sl], out_hbm.at[b, pl.ds(c * rows, rows)], wsem.at[sl]), action)()

    @pl.when(s == 0)
    def _():
        for ahead in range(depth - 1):
            @pl.when(ahead < total)
            def _():
                read(ahead, "start")

    @pl.when(s >= 1)
    def _():
        write(s - 1, "wait")

    @pl.when(s + depth - 1 < total)
    def _():
        read(s + depth - 1, "start")

    read(s, "wait")

    @pl.when(s % chunks == chunks - 1)
    def _():
        buf[s % depth, rows - steps:rows] = new_ref[0]

    write(s, "start")

    @pl.when(s == total - 1)
    def _():
        write(s, "wait")


def _window_shift(old, new):
    nb, win = old.shape[:2]
    steps = new.shape[1]
    rows = min(win, WINDOW_SHIFT_ROWS)
    chunks = win // rows
    tail = old.shape[2:]
    any_spec = pl.BlockSpec(memory_space=pl.ANY)
    return pl.pallas_call(
        functools.partial(_window_shift_kernel, steps=steps, chunks=chunks),
        out_shape=jax.ShapeDtypeStruct(old.shape, old.dtype),
        grid=(nb * chunks,),
        in_specs=[pl.BlockSpec((1, steps) + tail, lambda s: (s // chunks, 0, 0, 0, 0)), any_spec],
        out_specs=any_spec,
        scratch_shapes=[pltpu.VMEM((WINDOW_SHIFT_DEPTH, rows) + tail, old.dtype),
                        pltpu.SemaphoreType.DMA((WINDOW_SHIFT_DEPTH,)), pltpu.SemaphoreType.DMA((WINDOW_SHIFT_DEPTH,))],
        compiler_params=_params("arbitrary"),
        name="window_shift",
    )(new, old)


def _router_logits(xn, wr_ref, br_ref):
    xh = xn.astype(BF16)
    xl = (xn - xh.astype(F32)).astype(BF16)
    return _dot(xh, wr_ref[0]) + (_dot(xh, wr_ref[1]) + _dot(xl, wr_ref[0])) + br_ref[...]


def _route(logits, grp):
    lane = lax.broadcasted_iota(jnp.int32, logits.shape, 1)
    is_group = lane < MOE_GROUPS
    mx = jnp.max(jnp.where(is_group, logits, NEG), axis=-1, keepdims=True)
    if grp is None:
        grp = jnp.min(jnp.where(jnp.logical_and(is_group, logits == mx), lane, LANES), axis=-1, keepdims=True)
    den = jnp.sum(jnp.where(is_group, jnp.exp(logits - mx), 0.0), axis=-1, keepdims=True)
    sel = jnp.sum(jnp.where(lane == grp, logits, 0.0), axis=-1, keepdims=True)
    p_group = jnp.exp(sel - mx) / den
    lo = MOE_GROUPS + MOE_PER_GROUP * grp
    le = jnp.where(jnp.logical_and(lane >= lo, lane < lo + MOE_PER_GROUP), logits, NEG)
    v1 = jnp.max(le, axis=-1, keepdims=True)
    i1 = jnp.min(jnp.where(le == v1, lane, LANES), axis=-1, keepdims=True)
    le = jnp.where(lane == i1, NEG, le)
    v2 = jnp.max(le, axis=-1, keepdims=True)
    i2 = jnp.min(jnp.where(le == v2, lane, LANES), axis=-1, keepdims=True)
    e2 = jnp.exp(v2 - v1)
    return grp, i1, i2, p_group / (1.0 + e2), p_group * e2 / (1.0 + e2)


MOE_PAIRS = MOE_PER_GROUP * (MOE_PER_GROUP - 1) // 2
MOE_BUCKETS = MOE_GROUPS * MOE_PAIRS


def _bucket_ids(x, g_ref, wr_ref, br_ref):
    logits = _router_logits(_rmsnorm_rows(x, g_ref[...]), wr_ref, br_ref)
    grp, i1, i2, _, _ = _route(logits, None)
    lo = MOE_GROUPS + MOE_PER_GROUP * grp
    a = jnp.minimum(i1, i2) - lo
    b = jnp.maximum(i1, i2) - lo
    pair = ((a * (2 * MOE_PER_GROUP - 1 - a)) >> 1) + b - a - 1
    return jnp.broadcast_to(grp * MOE_PAIRS + pair, logits.shape)


def _moe_dispatch_kernel(slot_ref, pad_ref, x_ref, xs_hbm, zero_ref, sem, *, n_rows):
    i = pl.program_id(0)
    tm = x_ref.shape[0]
    count = jnp.minimum(tm, n_rows - i * tm)

    def start(k, carry, priority=0):
        pltpu.make_async_copy(x_ref.at[pl.ds(k, 1)], xs_hbm.at[pl.ds(slot_ref[i * tm + k], 1)], sem).start(priority=priority)
        return carry

    def start_group(kg, carry):
        for u in range(DMA_ISSUE_UNROLL):
            start(kg * DMA_ISSUE_UNROLL + u, carry, priority=u % 2)
        return carry

    def wait(k, carry):
        pltpu.make_async_copy(x_ref.at[pl.ds(0, 1)], xs_hbm.at[pl.ds(0, 1)], sem).wait()
        return carry

    @pl.when(count == tm)
    def _():
        lax.fori_loop(0, tm // DMA_ISSUE_UNROLL, start_group, 0)

    @pl.when(count < tm)
    def _():
        lax.fori_loop(0, count, start, 0)

    @pl.when(i == pl.num_programs(0) - 1)
    def _():
        zero_ref[...] = jnp.zeros(zero_ref.shape, zero_ref.dtype)
        n_ranges = pad_ref.shape[0] // 2
        for g in range(n_ranges):
            def start_pad(k, carry):
                pltpu.make_async_copy(zero_ref.at[pl.ds(0, 1)], xs_hbm.at[pl.ds(pad_ref[g] + k, 1)], sem).start()
                return carry
            lax.fori_loop(0, pad_ref[n_ranges + g], start_pad, 0)
            lax.fori_loop(0, pad_ref[n_ranges + g], wait, 0)

    @pl.when(count == tm)
    def _():
        pltpu.make_async_copy(x_ref, xs_hbm.at[pl.ds(0, tm)], sem).wait()

    @pl.when(count < tm)
    def _():
        lax.fori_loop(0, count, wait, 0)


def _moe_dispatch(x, slot_of_row, pads, n_rows, n_slots, tm):
    return pl.pallas_call(
        functools.partial(_moe_dispatch_kernel, n_rows=n_rows),
        out_shape=jax.ShapeDtypeStruct((n_slots, D_MODEL), F32),
        grid_spec=pltpu.PrefetchScalarGridSpec(
            num_scalar_prefetch=2,
            grid=(pl.cdiv(n_rows, tm),),
            in_specs=[pl.BlockSpec((tm, D_MODEL), lambda i, s, p: (i, 0))],
            out_specs=pl.BlockSpec(memory_space=pl.ANY),
            scratch_shapes=[pltpu.VMEM((8, D_MODEL), F32), pltpu.SemaphoreType.DMA(())]),
        compiler_params=_params("arbitrary"),
        name="moe_dispatch",
    )(slot_of_row, pads, x)


def _moe_collect_kernel(slot_ref, ys_hbm, o_ref, sem, *, row_off):
    i = pl.program_id(0)
    tm = o_ref.shape[0]

    def start_group(kg, carry):
        for u in range(DMA_ISSUE_UNROLL):
            k = kg * DMA_ISSUE_UNROLL + u
            pltpu.make_async_copy(ys_hbm.at[pl.ds(slot_ref[row_off + i * tm + k], 1)], o_ref.at[pl.ds(k, 1)],
                                  sem).start(priority=u % 2)
        return carry

    lax.fori_loop(0, tm // DMA_ISSUE_UNROLL, start_group, 0)
    pltpu.make_async_copy(ys_hbm.at[pl.ds(0, tm)], o_ref, sem).wait()


def _moe_collect(ys, slot_of_row, row_off, rows, tm):
    return pl.pallas_call(
        functools.partial(_moe_collect_kernel, row_off=row_off),
        out_shape=jax.ShapeDtypeStruct((rows, D_MODEL), F32),
        grid_spec=pltpu.PrefetchScalarGridSpec(
            num_scalar_prefetch=1,
            grid=(rows // tm,),
            in_specs=[pl.BlockSpec(memory_space=pl.ANY)],
            out_specs=pl.BlockSpec((tm, D_MODEL), lambda i, s: (i, 0)),
            scratch_shapes=[pltpu.SemaphoreType.DMA(())]),
        compiler_params=_params("arbitrary"),
        name="moe_collect",
    )(slot_of_row, ys)


def _moe_expert_kernel(ea_ref, eb_ref, used_ref, x_ref, gain_ref, wr_ref, br_ref,
                       wga_ref, wua_ref, wda_ref, wgb_ref, wub_ref, wdb_ref, y_ref):
    t = pl.program_id(0)

    @pl.when(t >= used_ref[0])
    def _():
        y_ref[...] = jnp.zeros(y_ref.shape, y_ref.dtype)

    @pl.when(t < used_ref[0])
    def _():
        x = x_ref[...]
        xn = _rmsnorm_rows(x, gain_ref[...])
        _, i1, i2, w1, w2 = _route(_router_logits(xn, wr_ref, br_ref), ea_ref[t] // MOE_PER_GROUP)
        xb = xn.astype(BF16)
        acc = x
        for e_ref, wg_ref, wu_ref, wd_ref in ((ea_ref, wga_ref, wua_ref, wda_ref), (eb_ref, wgb_ref, wub_ref, wdb_ref)):
            e_lane = MOE_GROUPS + e_ref[t]
            ce = jnp.where(i1 == e_lane, w1, 0.0) + jnp.where(i2 == e_lane, w2, 0.0)
            hg = _dot(xb, wg_ref[0])
            hid = hg * jax.nn.sigmoid(hg) * _dot(xb, wu_ref[0])
            acc = acc + ce * _dot(hid.astype(BF16), wd_ref[0])
        y_ref[...] = acc


def _moe_plan(bucket, tile, n_tiles):
    onehot = (bucket[:, None] == jnp.arange(MOE_BUCKETS)[None, :]).astype(jnp.int32)
    csum = jnp.cumsum(onehot, axis=0)
    counts = csum[-1]
    tiles_per = (counts + tile - 1) // tile
    tile_end = jnp.cumsum(tiles_per)
    tile_start = tile_end - tiles_per
    slot_of_row = jnp.sum(onehot * (tile_start[None, :] * tile + csum - 1), axis=1)
    tile_bucket = jnp.minimum(jnp.sum(jnp.arange(n_tiles)[:, None] >= tile_end[None, :], axis=1), MOE_BUCKETS - 1)
    pair = tile_bucket % MOE_PAIRS
    first = jnp.sum(pair[:, None] >= jnp.cumsum(jnp.arange(MOE_PER_GROUP - 1, 0, -1))[None, :], axis=1)
    second = pair - ((first * (2 * MOE_PER_GROUP - 1 - first)) >> 1) + first + 1
    base = (tile_bucket // MOE_PAIRS) * MOE_PER_GROUP
    used = tile_end[-1:]
    pads = jnp.concatenate([tile_start * tile + counts, used * tile, tiles_per * tile - counts, (n_tiles - used) * tile])
    i32 = lambda v: v.astype(jnp.int32)
    return i32(slot_of_row), i32(base + first), i32(base + second), i32(used), i32(pads)


def _moe(x, buckets, n_prompt, n_sample, gain, wr_pair, br, wg, wu, wd):
    n = n_prompt + n_sample
    tile = MOE_TILE
    n_tiles = n // tile + MOE_BUCKETS
    slot_of_row, expert_a, expert_b, n_used, pads = _moe_plan(buckets[:n, 0], tile, n_tiles)
    xs = _moe_dispatch(x, slot_of_row, pads, n, n_tiles * tile, ROW_TILE)
    w_spec = lambda shape, which: pl.BlockSpec((1,) + shape, lambda t, ea, eb, u: ((ea, eb)[which][t], 0, 0))
    const = lambda shape: pl.BlockSpec(shape, lambda t, ea, eb, u: (0,) * len(shape), pipeline_mode=pl.Buffered(1))
    w_specs = [w_spec(shape, which) for which in range(2)
               for shape in ((D_MODEL, MOE_HIDDEN), (D_MODEL, MOE_HIDDEN), (MOE_HIDDEN, D_MODEL))]
    ys = pl.pallas_call(
        _moe_expert_kernel,
        out_shape=jax.ShapeDtypeStruct((n_tiles * tile, D_MODEL), F32),
        grid_spec=pltpu.PrefetchScalarGridSpec(
            num_scalar_prefetch=3,
            grid=(n_tiles,),
            in_specs=[pl.BlockSpec((tile, D_MODEL), lambda t, ea, eb, u: (jnp.minimum(t, u[0] - 1), 0)),
                      const((1, D_MODEL)), const((2, D_MODEL, LANES)), const((1, LANES)), *w_specs],
            out_specs=pl.BlockSpec((tile, D_MODEL), lambda t, ea, eb, u: (t, 0))),
        compiler_params=_params("arbitrary"),
        name="moe_experts",
    )(expert_a, expert_b, n_used, xs, gain, wr_pair, br, wg, wu, wd, wg, wu, wd)
    return (_moe_collect(ys, slot_of_row, 0, n_prompt, ROW_TILE),
            _moe_collect(ys, slot_of_row, n_prompt, n_sample, n_sample))


def _rope_tables(pos):
    half = RET_DK // 2
    inv = 1.0 / (ROPE_BASE ** jnp.linspace(0.0, 1.0, half, dtype=F32))
    ang = jnp.repeat(pos.astype(F32)[:, None] * inv[None, :], 2, axis=-1)
    sign = jnp.where(jnp.arange(RET_DK) % 2 == 0, -1.0, 1.0).astype(F32)
    return jnp.cos(ang), jnp.sin(ang) * sign


def kernel(x_prompt, x_sample, state_ret, cache_win0, cache_win1, cache_win2, rel_bias, norm_mix, norm_ffn, w_in_ret, w_out_ret, w_in_att, q_norm, k_norm, w_out_att, w_router_group, b_router_group, w_router_expert, b_router_expert, w_gate, w_up, w_down):
    nb_p, len_p, _ = x_prompt.shape
    nb_s, len_s, _ = x_sample.shape
    n_p, n_s = nb_p * len_p, nb_s * len_s
    n_all = n_p + n_s
    assert n_p % ROW_TILE == 0 and n_p % n_s == 0 and len_p % RET_CHUNK == 0 and len_s <= SAMPLE_PAD
    assert all(len_p % (ATT_SPAN * d) == 0 for _, d in ATT_GROUPS)
    caches = (cache_win0[0], cache_win1[0], cache_win2[0])

    xp = x_prompt.reshape(n_p, D_MODEL)
    xs = x_sample.reshape(n_s, D_MODEL)
    row = lambda v: v.reshape(1, -1).astype(F32)

    def moe_weights(i):
        wr = jnp.zeros((D_MODEL, LANES), F32)
        wr = wr.at[:, :MOE_GROUPS].set(w_router_group[i]).at[:, MOE_GROUPS:MOE_GROUPS + MOE_EXPERTS].set(w_router_expert[i])
        br = jnp.zeros((1, LANES), F32)
        br = br.at[0, :MOE_GROUPS].set(b_router_group[i]).at[0, MOE_GROUPS:MOE_GROUPS + MOE_EXPERTS].set(b_router_expert[i])
        wr_hi = wr.astype(BF16)
        wr_pair = jnp.stack([wr_hi, (wr - wr_hi.astype(F32)).astype(BF16)])
        return (row(norm_ffn[i]), wr_pair, br, w_gate[i].astype(BF16), w_up[i].astype(BF16), w_down[i].astype(BF16))

    gain0 = row(norm_mix[0])
    w_in = w_in_ret[0].astype(BF16)
    w_out = w_out_ret[0].astype(BF16)
    cos_p, sin_p = _rope_tables(jnp.arange(len_p))
    pos_s = jnp.tile(PAST_LEN + jnp.arange(SAMPLE_PAD), nb_s)
    cos_s, sin_s = _rope_tables(pos_s)
    xs_pad = jnp.pad(x_sample, ((0, 0), (0, SAMPLE_PAD - len_s), (0, 0))).reshape(nb_s * SAMPLE_PAD, D_MODEL)

    proj_p = _ret_proj(xp, 0, n_p, ROW_TILE, gain0, w_in, cos_p, sin_p)
    proj_s = _ret_proj(xs_pad, 0, nb_s * SAMPLE_PAD, nb_s * SAMPLE_PAD, gain0, w_in, cos_s, sin_s)
    zero_state = jnp.zeros((nb_p, RET_HEADS, RET_DK, RET_DV), F32)
    o_p, ret_p = _ret_core(proj_p, zero_state, nb_p, len_p // RET_CHUNK, RET_CHUNK, RET_CHUNK)
    o_s, ret_s = _ret_core(proj_s, state_ret[0].astype(F32), nb_s, 1, SAMPLE_PAD, len_s)
    o_s = o_s.reshape(nb_s, SAMPLE_PAD, RET_VW)[:, :len_s].reshape(n_s, RET_VW)
    moe0 = moe_weights(0)
    x1, buckets = _mm_res(o_p, w_out, xp, 0, ROW_TILE, tail=_mm_res(o_s, w_out, xs, 0, n_s), route=moe0[:3])
    x2, x2_s = _moe(x1, buckets, n_p, n_s, *moe0)

    gain1 = row(norm_mix[1])
    w_in = w_in_att[0].astype(BF16)
    w_out = w_out_att[0].astype(BF16)
    q_gain = row(q_norm[0]) * (ATT_HD ** -0.5)
    k_gain = row(k_norm[0])
    qkv_s = _att_proj(x2_s, 0, n_s, n_s, gain1, w_in, q_gain, k_gain, F32)
    qkv_s = qkv_s.reshape(nb_s, len_s, 3, ATT_NG, ATT_G, ATT_HD)

    outs, win_p, win_s, sample_bias = [], [], [], []
    for g, (win, dil) in enumerate(ATT_GROUPS):
        tab = rel_bias[:, g * ATT_G:(g + 1) * ATT_G].astype(F32)
        w_g = jnp.concatenate([w_in[:, (i * ATT_NG + g) * ATT_W:(i * ATT_NG + g + 1) * ATT_W] for i in range(3)], axis=1)
        if dil == 1:
            halves = 1
            qkv_g = _att_proj(x2, 0, n_p, ROW_TILE, gain1, w_g, q_gain, k_gain, BF16).reshape(nb_p, 1, len_p, 3 * ATT_W)
        else:
            halves = len(_residue_picks(dil)[1][0])
            qkv_g = _att_proj_dil(x2, nb_p, len_p, dil, gain1, w_g, q_gain, k_gain)
        outs.append(_att_band(qkv_g, dil, _band_bias(tab, dil, halves)))
        win_p.append(_window_rows(qkv_g, dil, caches[g].dtype)[None])
        win_s.append(_window_shift(caches[g], qkv_s[:, :, 1:, g].astype(caches[g].dtype))[None])
        b = _bucket_rows(tab, (ATT_SPAN - jnp.arange(ATT_SPAN + 1)) * dil)
        sample_bias.append(jnp.broadcast_to(b[:, :, None], (ATT_SPAN + 1, ATT_G, LANES)))
    o_s = _att_sample(qkv_s, caches, sample_bias)
    x3_s = _mm_res(o_s.reshape(n_s, ATT_W).astype(BF16), w_out, x2_s, 0, n_s)
    moe1 = moe_weights(1)
    x3, buckets = _att_merge(outs, w_out, x2, n_p, ROW_TILE, x3_s, moe1[:3])
    y_p, y_s = _moe(x3, buckets, n_p, n_s, *moe1)

    y_p = y_p.reshape(nb_p, len_p, D_MODEL)
    y_s = y_s.reshape(nb_s, len_s, D_MODEL)
    dt = state_ret.dtype
    return (y_p, y_s, ret_p[None].astype(dt), ret_s[None].astype(dt),
            win_p[0], win_s[0], win_p[1], win_s[1], win_p[2], win_s[2])
```

```python
import functools

import jax
import jax.numpy as jnp
from jax import lax
from jax.experimental import pallas as pl
from jax.experimental.pallas import tpu as pltpu

F32 = jnp.float32
BF16 = jnp.bfloat16

D_MODEL = 1024
PAST_LEN = 16384
RET_HEADS = 4
RET_DK = 256
RET_DV = 512
RET_QK = RET_HEADS * RET_DK
RET_VW = RET_HEADS * RET_DV
RET_IN = 2 * RET_QK + 2 * RET_VW
ROPE_BASE = 10000.0
ATT_GROUPS = ((128, 1), (512, 4), (2048, 16))
ATT_NG = len(ATT_GROUPS)
ATT_G = 8
ATT_HD = 128
ATT_W = ATT_G * ATT_HD
ATT_IN = 3 * ATT_NG * ATT_W
ATT_SPAN = 128
N_BUCKETS = 32
MAX_DISTANCE = 2048
MOE_GROUPS = 4
MOE_PER_GROUP = 4
MOE_EXPERTS = MOE_GROUPS * MOE_PER_GROUP
MOE_HIDDEN = 512
EPS = 1e-6
NEG = -1e30

LANES = 128
V7X_VMEM_LIMIT_BYTES = 56 * 1024 * 1024
ROW_TILE = 512
RET_CHUNK = 256
SAMPLE_PAD = 16
MOE_TILE = 512
WINDOW_SHIFT_ROWS = 256
WINDOW_SHIFT_DEPTH = 3
ZERO_ROWS = 256
ATT_BAND_UNITS = 4
OUT_LSE = ATT_W + LANES


def _params(*sem):
    return pltpu.CompilerParams(dimension_semantics=sem, vmem_limit_bytes=V7X_VMEM_LIMIT_BYTES)


def _resident(shape):
    return pl.BlockSpec(shape, lambda *_: (0,) * len(shape), pipeline_mode=pl.Buffered(1))


def _rmsnorm_rows(x, gain):
    return x * lax.rsqrt(jnp.mean(x * x, axis=-1, keepdims=True) + EPS) * gain


def _dot(a, b):
    return jnp.dot(a, b, preferred_element_type=F32)


def _dot_nt(a, b):
    return lax.dot_general(a, b, (((1,), (1,)), ((), ())), preferred_element_type=F32)


def _dot_tn(a, b):
    return lax.dot_general(a, b, (((0,), (0,)), ((), ())), preferred_element_type=F32)


def _ret_proj_kernel(x_ref, g_ref, w_ref, cos_ref, sin_ref, o_ref):
    xn = _rmsnorm_rows(x_ref[...], g_ref[...]).astype(BF16)
    cos = cos_ref[...]
    sin = sin_ref[...]
    even = (lax.broadcasted_iota(jnp.int32, cos.shape, 1) & 1) == 0
    for c in range(2 * RET_HEADS):
        sl = slice(c * RET_DK, (c + 1) * RET_DK)
        acc = _dot(xn, w_ref[:, sl])
        partner = jnp.where(even, pltpu.roll(acc, RET_DK - 1, 1), pltpu.roll(acc, 1, 1))
        r = acc * cos + partner * sin
        if c >= RET_HEADS:
            r = r * (RET_DK ** -0.5)
        o_ref[:, sl] = r.astype(o_ref.dtype)
    for c in range(2 * RET_QK // RET_DV, RET_IN // RET_DV):
        sl = slice(c * RET_DV, (c + 1) * RET_DV)
        o_ref[:, sl] = _dot(xn, w_ref[:, sl]).astype(o_ref.dtype)


def _ret_proj(x, row_off, rows, tm, gain, w, cos, sin):
    tab_blocks = cos.shape[0] // tm
    return pl.pallas_call(
        _ret_proj_kernel,
        out_shape=jax.ShapeDtypeStruct((rows, RET_IN), BF16),
        grid=(rows // tm,),
        in_specs=[
            pl.BlockSpec((tm, D_MODEL), lambda i: (i + row_off, 0)),
            _resident((1, D_MODEL)),
            _resident((D_MODEL, RET_IN)),
            pl.BlockSpec((tm, RET_DK), lambda i: (i % tab_blocks, 0)),
            pl.BlockSpec((tm, RET_DK), lambda i: (i % tab_blocks, 0)),
        ],
        out_specs=pl.BlockSpec((tm, RET_IN), lambda i: (i, 0)),
        compiler_params=_params("parallel"),
        name="ret_proj",
    )(x, gain, w, cos, sin)


def _ret_core_kernel(cdec_ref, q_ref, k_ref, v_ref, gt_ref, s0_ref, intra_ref, qdec_ref, kdec_ref, o_ref, st_ref):
    @pl.when(pl.program_id(1) == 0)
    def _():
        st_ref[...] = s0_ref[...]

    for h in range(RET_HEADS):
        qk = slice(h * RET_DK, (h + 1) * RET_DK)
        vv = slice(h * RET_DV, (h + 1) * RET_DV)
        qh, kh, vh = q_ref[:, qk], k_ref[:, qk], v_ref[:, vv]
        s = st_ref[0, h]
        sc = _dot_nt(qh, kh) * intra_ref[h]
        o = _dot(sc.astype(BF16), vh)
        o = o + _dot(qh, s.astype(BF16)) * jnp.concatenate([qdec_ref[h]] * (RET_DV // LANES), axis=1)
        kd = (kh.astype(F32) * jnp.concatenate([kdec_ref[h]] * (RET_DK // LANES), axis=1)).astype(BF16)
        st_ref[0, h] = s * cdec_ref[h] + _dot_tn(kd, vh)
        o = o * lax.rsqrt(jnp.mean(o * o, axis=-1, keepdims=True) + EPS)
        g = gt_ref[:, vv].astype(F32)
        o_ref[:, vv] = (o * (g * jax.nn.sigmoid(g))).astype(o_ref.dtype)


def _ret_decay_tables(chunk, n_real):
    log_g = jnp.log(1.0 - 2.0 ** (-5.0 - jnp.arange(RET_HEADS, dtype=F32)))
    idx = jnp.arange(chunk, dtype=F32)
    diff = idx[:, None] - idx[None, :]
    intra = jnp.where(diff >= 0, jnp.exp(jnp.maximum(diff, 0.0)[None] * log_g[:, None, None]), 0.0)
    q_dec = jnp.exp((idx + 1.0)[None, :] * log_g[:, None])
    k_dec = jnp.where(idx[None, :] < n_real, jnp.exp((n_real - 1.0 - idx)[None, :] * log_g[:, None]), 0.0)
    c_dec = jnp.exp(n_real * log_g)
    rep = lambda t: jnp.broadcast_to(t[:, :, None], (RET_HEADS, chunk, LANES))
    return c_dec, intra, rep(q_dec), rep(k_dec)


def _ret_core(proj, s0, nseq, nchunk, chunk, n_real):
    c_dec, intra, q_dec, k_dec = _ret_decay_tables(chunk, n_real)
    rows = nseq * nchunk * chunk
    row = lambda b, c: b * nchunk + c
    st_spec = pl.BlockSpec((1, RET_HEADS, RET_DK, RET_DV), lambda b, c: (b, 0, 0, 0))
    return pl.pallas_call(
        _ret_core_kernel,
        out_shape=(jax.ShapeDtypeStruct((rows, RET_VW), BF16),
                   jax.ShapeDtypeStruct((nseq, RET_HEADS, RET_DK, RET_DV), F32)),
        grid=(nseq, nchunk),
        in_specs=[
            pl.BlockSpec(memory_space=pltpu.SMEM),
            pl.BlockSpec((chunk, RET_QK), lambda b, c: (row(b, c), 0)),
            pl.BlockSpec((chunk, RET_QK), lambda b, c: (row(b, c), 1)),
            pl.BlockSpec((chunk, RET_VW), lambda b, c: (row(b, c), 1)),
            pl.BlockSpec((chunk, RET_VW), lambda b, c: (row(b, c), 2)),
            st_spec,
            _resident((RET_HEADS, chunk, chunk)),
            _resident((RET_HEADS, chunk, LANES)),
            _resident((RET_HEADS, chunk, LANES)),
        ],
        out_specs=(pl.BlockSpec((chunk, RET_VW), lambda b, c: (row(b, c), 0)), st_spec),
        compiler_params=_params("parallel", "arbitrary"),
        name="ret_core",
    )(c_dec, proj, proj, proj, proj, s0, intra, q_dec, k_dec)


def _with_tail_and_buckets(rows_fn, tail_ref, route_refs, o_ref, b_ref):
    last = pl.num_programs(0) - 1

    def emit(rows):
        o_ref[...] = rows
        b_ref[...] = _bucket_ids(rows, *route_refs)

    @pl.when(pl.program_id(0) < last)
    def _():
        emit(rows_fn())

    @pl.when(pl.program_id(0) == last)
    def _():
        pad = jnp.zeros((o_ref.shape[0] - tail_ref.shape[0], o_ref.shape[1]), o_ref.dtype)
        emit(jnp.concatenate([tail_ref[...], pad], axis=0))


def _route_specs():
    return [_resident((1, D_MODEL)), _resident((2, D_MODEL, LANES)), _resident((1, LANES))]


def _mm_res_kernel(a_ref, w_ref, x_ref, *rest):
    rows_fn = lambda: x_ref[...] + _dot(a_ref[...], w_ref[...])
    if len(rest) == 1:
        rest[0][...] = rows_fn()
    else:
        tail_ref, *route_refs, o_ref, b_ref = rest
        _with_tail_and_buckets(rows_fn, tail_ref, route_refs, o_ref, b_ref)


def _mm_res(a, w, x, x_off, tm, tail=None, route=None):
    m, k = a.shape
    nblk = m // tm
    clamp = (lambda i: jnp.minimum(i, nblk - 1)) if tail is not None else (lambda i: i)
    in_specs = [
        pl.BlockSpec((tm, k), lambda i: (clamp(i), 0)),
        _resident((k, D_MODEL)),
        pl.BlockSpec((tm, D_MODEL), lambda i: (clamp(i) + x_off, 0)),
    ]
    args = [a, w, x]
    steps = nblk
    out_shape = jax.ShapeDtypeStruct((nblk * tm, D_MODEL), F32)
    out_specs = pl.BlockSpec((tm, D_MODEL), lambda i: (i, 0))
    if tail is not None:
        steps = nblk + 1
        in_specs += [_resident(tail.shape)] + _route_specs()
        args += [tail, *route]
        out_shape = (jax.ShapeDtypeStruct((steps * tm, D_MODEL), F32), jax.ShapeDtypeStruct((steps * tm, LANES), jnp.int32))
        out_specs = (out_specs, pl.BlockSpec((tm, LANES), lambda i: (i, 0)))
    return pl.pallas_call(
        _mm_res_kernel,
        out_shape=out_shape,
        grid=(steps,),
        in_specs=in_specs,
        out_specs=out_specs,
        compiler_params=_params("arbitrary"),
        name="mm_res",
    )(*args)


ATT_PROJ_CHUNK = 4 * ATT_HD


def _att_proj_chunk(xn, w_ref, qg_ref, kg_ref, c, width):
    acc = _dot(xn, w_ref[:, c * ATT_PROJ_CHUNK:(c + 1) * ATT_PROJ_CHUNK])
    n_qk = 2 * (width // 3) // ATT_PROJ_CHUNK
    if c >= n_qk:
        return acc
    gain = qg_ref[...] if c < n_qk // 2 else kg_ref[...]
    heads = []
    for hh in range(ATT_PROJ_CHUNK // ATT_HD):
        a = acc[:, hh * ATT_HD:(hh + 1) * ATT_HD]
        heads.append(a * lax.rsqrt(jnp.mean(a * a, axis=-1, keepdims=True) + EPS) * gain)
    return jnp.concatenate(heads, axis=1)


def _att_proj_kernel(x_ref, g_ref, w_ref, qg_ref, kg_ref, o_ref):
    xn = _rmsnorm_rows(x_ref[...], g_ref[...]).astype(BF16)
    width = o_ref.shape[1]
    for c in range(width // ATT_PROJ_CHUNK):
        sl = slice(c * ATT_PROJ_CHUNK, (c + 1) * ATT_PROJ_CHUNK)
        o_ref[:, sl] = _att_proj_chunk(xn, w_ref, qg_ref, kg_ref, c, width).astype(o_ref.dtype)


def _att_proj_dil_kernel(x_hbm, g_ref, w_ref, qg_ref, kg_ref, o_ref, xbuf, sem, *, picks, rows):
    i = pl.program_id(0)
    slot = i % 2

    def fetch(step, sl, action):
        k = 0
        for mids in picks:
            for s in mids:
                cp = pltpu.make_async_copy(x_hbm.at[pl.ds(step * rows, rows), s], xbuf.at[sl, pl.ds(k * rows, rows)],
                                           sem.at[sl])
                getattr(cp, action)()
                k += 1

    @pl.when(i == 0)
    def _():
        fetch(0, 0, "start")

    @pl.when(i + 1 < pl.num_programs(0))
    def _():
        fetch(i + 1, 1 - slot, "start")

    fetch(i, slot, "wait")
    per = rows * len(picks[0])
    width = o_ref.shape[3]
    xn = _rmsnorm_rows(xbuf[slot], g_ref[...]).astype(BF16)
    for c in range(width // ATT_PROJ_CHUNK):
        sl = slice(c * ATT_PROJ_CHUNK, (c + 1) * ATT_PROJ_CHUNK)
        val = _att_proj_chunk(xn, w_ref, qg_ref, kg_ref, c, width).astype(o_ref.dtype)
        for r in range(len(picks)):
            o_ref[0, r, :, sl] = val[r * per:(r + 1) * per]


def _residue_picks(dil):
    if dil % 8 == 0:
        return dil, [[r] for r in range(dil)]
    assert 8 % dil == 0 and dil > 1
    return 8, [[r + dil * e for e in range(8 // dil)] for r in range(dil)]


def _att_proj_dil(x, nb, length, dil, gain, w, q_gain, k_gain):
    sub, picks = _residue_picks(dil)
    width = w.shape[1]
    rows = ATT_SPAN // 2
    per = rows * len(picks[0])
    ls = length // dil
    steps_per_seq = ls // per
    return pl.pallas_call(
        functools.partial(_att_proj_dil_kernel, picks=picks, rows=rows),
        out_shape=jax.ShapeDtypeStruct((nb, dil, ls, width), BF16),
        grid=(nb * steps_per_seq,),
        in_specs=[
            pl.BlockSpec(memory_space=pl.ANY),
            _resident((1, D_MODEL)),
            _resident((D_MODEL, width)),
            _resident((1, ATT_HD)),
            _resident((1, ATT_HD)),
        ],
        out_specs=pl.BlockSpec((1, dil, per, width), lambda i: (i // steps_per_seq, 0, i % steps_per_seq, 0)),
        scratch_shapes=[pltpu.VMEM((2, dil * per, D_MODEL), F32), pltpu.SemaphoreType.DMA((2,))],
        compiler_params=_params("arbitrary"),
        name="att_proj_dil",
    )(x.reshape(x.shape[0] // sub, sub, D_MODEL), gain, w, q_gain, k_gain)


def _att_proj(x, row_off, rows, tm, gain, w, q_gain, k_gain, out_dtype):
    width = w.shape[1]
    return pl.pallas_call(
        _att_proj_kernel,
        out_shape=jax.ShapeDtypeStruct((rows, width), out_dtype),
        grid=(rows // tm,),
        in_specs=[
            pl.BlockSpec((tm, D_MODEL), lambda i: (i + row_off, 0)),
            _resident((1, D_MODEL)),
            _resident((D_MODEL, width)),
            _resident((1, ATT_HD)),
            _resident((1, ATT_HD)),
        ],
        out_specs=pl.BlockSpec((tm, width), lambda i: (i, 0)),
        compiler_params=_params("parallel"),
        name="att_proj",
    )(x, gain, w, q_gain, k_gain)


def _att_band_kernel(q_ref, kp_ref, kc_ref, vp_ref, vc_ref, bias_ref, o_ref, *scratch, picks, blocks):
    n = pl.program_id(1)
    res = q_ref.shape[1]
    col = lax.broadcasted_iota(jnp.int32, (ATT_SPAN, 2 * ATT_SPAN), 1)
    no_prev = jnp.logical_and(col < ATT_SPAN, n == 0)
    lane = lax.broadcasted_iota(jnp.int32, (ATT_SPAN, LANES), 1)

    def unit(rr, j):
        lse_tile = jnp.zeros((ATT_SPAN, LANES), F32)
        heads = []
        for h in range(ATT_G):
            sl = slice(h * ATT_HD, (h + 1) * ATT_HD)
            if j == 0:
                kcat = jnp.concatenate([kp_ref[0, rr, :, sl], kc_ref[0, rr, 0:ATT_SPAN, sl]], axis=0)
                vcat = jnp.concatenate([vp_ref[0, rr, :, sl], vc_ref[0, rr, 0:ATT_SPAN, sl]], axis=0)
            else:
                kcat = kc_ref[0, rr, (j - 1) * ATT_SPAN:(j + 1) * ATT_SPAN, sl]
                vcat = vc_ref[0, rr, (j - 1) * ATT_SPAN:(j + 1) * ATT_SPAN, sl]
            s = _dot_nt(q_ref[0, rr, j * ATT_SPAN:(j + 1) * ATT_SPAN, sl], kcat) + bias_ref[h]
            if j == 0:
                s = jnp.where(no_prev, NEG, s)
            m = jnp.max(s, axis=-1, keepdims=True)
            p = jnp.exp(s - m)
            den = jnp.sum(p, axis=-1, keepdims=True)
            heads.append(_dot(p.astype(BF16), vcat) / den)
            lse_tile = jnp.where(lane == h, m + jnp.log(den), lse_tile)
        return jnp.concatenate(heads + [lse_tile], axis=1)

    if picks is None:
        for j in range(blocks):
            o_ref[0, j * ATT_SPAN:(j + 1) * ATT_SPAN, :] = unit(0, j)
        return

    obuf, sem = scratch
    b, g = pl.program_id(0), pl.program_id(2)
    steps = pl.num_programs(0) * pl.num_programs(1) * pl.num_programs(2)
    flat = (b * pl.num_programs(1) + n) * pl.num_programs(2) + g
    slot = flat % 2
    per = ATT_SPAN // len(picks[0])

    def writeback(sl, action):
        for rr in range(res):
            for e, mid in enumerate(picks[0]):
                cp = pltpu.make_async_copy(obuf.at[sl, rr, pl.ds(e * per, per)],
                                           o_ref.at[b, pl.ds(n * per, per), g * res + rr + mid], sem.at[sl])
                getattr(cp, action)()

    @pl.when(flat >= 2)
    def _():
        writeback(slot, "wait")

    for rr in range(res):
        obuf[slot, rr] = unit(rr, 0)
    writeback(slot, "start")

    @pl.when(flat == steps - 1)
    def _():
        writeback(slot, "wait")

        @pl.when(steps >= 2)
        def _():
            writeback(1 - slot, "wait")


def _att_band(qkv, dil, bias):
    nb, _, ls, _ = qkv.shape
    units = ATT_BAND_UNITS
    if dil == 1:
        picks, res, blocks = None, 1, units
        out_shape = (nb, ls, OUT_LSE)
        out_spec = pl.BlockSpec((1, blocks * ATT_SPAN, OUT_LSE), lambda b, n, g: (b, n, 0))
        scratch = []
    else:
        sub, picks = _residue_picks(dil)
        res, blocks = units, 1
        out_shape = (nb, ls * dil // sub, sub, OUT_LSE)
        out_spec = pl.BlockSpec(memory_space=pl.ANY)
        scratch = [pltpu.VMEM((2, res, ATT_SPAN, OUT_LSE), F32), pltpu.SemaphoreType.DMA((2,))]
    cur = lambda cb: pl.BlockSpec((1, res, blocks * ATT_SPAN, ATT_W), lambda b, n, g: (b, g, n, cb))
    prev = lambda cb: pl.BlockSpec((1, res, ATT_SPAN, ATT_W), lambda b, n, g: (b, g, jnp.maximum(n * blocks - 1, 0), cb))
    out = pl.pallas_call(
        functools.partial(_att_band_kernel, picks=picks, blocks=blocks),
        out_shape=jax.ShapeDtypeStruct(out_shape, F32),
        grid=(nb, ls // (blocks * ATT_SPAN), dil // res),
        in_specs=[cur(0), prev(1), cur(1), prev(2), cur(2), _resident((ATT_G, ATT_SPAN, 2 * ATT_SPAN))],
        out_specs=out_spec,
        scratch_shapes=scratch,
        compiler_params=_params("arbitrary", "arbitrary", "arbitrary"),
        name="att_band",
    )(qkv, qkv, qkv, qkv, qkv, bias)
    return out.reshape(nb * ls * dil, OUT_LSE)


def _window_rows_kernel(k_ref, v_ref, o_ref):
    halves = o_ref.shape[2]
    per = o_ref.shape[1]
    for e in range(halves):
        for kv, ref in enumerate((k_ref, v_ref)):
            for h in range(ATT_G):
                o_ref[0, :, e, 0, kv, h, :] = ref[0, 0, e * per:(e + 1) * per, h * ATT_HD:(h + 1) * ATT_HD].astype(o_ref.dtype)


def _window_rows(qkv, dil, dtype):
    nb, _, ls, _ = qkv.shape
    halves = 1 if dil == 1 else len(_residue_picks(dil)[1][0])
    per = ATT_SPAN // halves
    last = ls // ATT_SPAN - 1
    out = pl.pallas_call(
        _window_rows_kernel,
        out_shape=jax.ShapeDtypeStruct((nb, per, halves, dil, 2, ATT_G, ATT_HD), dtype),
        grid=(nb, dil),
        in_specs=[pl.BlockSpec((1, 1, ATT_SPAN, ATT_W), lambda b, r: (b, r, last, 1)),
                  pl.BlockSpec((1, 1, ATT_SPAN, ATT_W), lambda b, r: (b, r, last, 2))],
        out_specs=pl.BlockSpec((1, per, halves, 1, 2, ATT_G, ATT_HD), lambda b, r: (b, 0, 0, r, 0, 0, 0)),
        compiler_params=_params("parallel", "parallel"),
        name="window_rows",
    )(qkv, qkv)
    return out.reshape(nb, ATT_SPAN * dil, 2, ATT_G, ATT_HD)


def _att_merge_kernel(o0_ref, o1_ref, o2_ref, w_ref, x_ref, tail_ref, *rest):
    *route_refs, out_ref, b_ref = rest

    def rows_fn():
        acc = x_ref[...]
        refs = (o0_ref, o1_ref, o2_ref)
        for h in range(ATT_G):
            sl = slice(h * ATT_HD, (h + 1) * ATT_HD)
            lses = [r[:, ATT_W + h:ATT_W + h + 1] for r in refs]
            m = jnp.maximum(jnp.maximum(lses[0], lses[1]), lses[2])
            es = [jnp.exp(l - m) for l in lses]
            merged = (es[0] * o0_ref[:, sl] + es[1] * o1_ref[:, sl] + es[2] * o2_ref[:, sl]) / (es[0] + es[1] + es[2])
            acc = acc + _dot(merged.astype(BF16), w_ref[sl, :])
        return acc

    _with_tail_and_buckets(rows_fn, tail_ref, route_refs, out_ref, b_ref)


def _att_merge(outs, w, x, rows, tm, tail, route):
    nblk = rows // tm
    row_spec = lambda width: pl.BlockSpec((tm, width), lambda i: (jnp.minimum(i, nblk - 1), 0))
    return pl.pallas_call(
        _att_merge_kernel,
        out_shape=(jax.ShapeDtypeStruct(((nblk + 1) * tm, D_MODEL), F32),
                   jax.ShapeDtypeStruct(((nblk + 1) * tm, LANES), jnp.int32)),
        grid=(nblk + 1,),
        in_specs=[row_spec(OUT_LSE), row_spec(OUT_LSE), row_spec(OUT_LSE), _resident((ATT_W, D_MODEL)),
                  row_spec(D_MODEL), _resident(tail.shape)] + _route_specs(),
        out_specs=(pl.BlockSpec((tm, D_MODEL), lambda i: (i, 0)), pl.BlockSpec((tm, LANES), lambda i: (i, 0))),
        compiler_params=_params("arbitrary"),
        name="att_merge",
    )(*outs, w, x, tail, *route)


def _t5_bucket(dist):
    max_exact = N_BUCKETS // 2
    d32 = jnp.maximum(dist, 1).astype(F32)
    large = max_exact + (jnp.log(d32 / max_exact) / jnp.log(MAX_DISTANCE / max_exact)
                         * (N_BUCKETS - max_exact)).astype(jnp.int32)
    return jnp.where(dist < max_exact, dist, jnp.minimum(large, N_BUCKETS - 1))


def _bucket_rows(tab, dist):
    onehot = jax.nn.one_hot(_t5_bucket(dist), N_BUCKETS, dtype=F32)
    return jnp.dot(onehot, tab, precision=lax.Precision.HIGHEST)


def _band_bias(tab, dil, halves):
    pos = jnp.arange(ATT_SPAN).reshape(ATT_SPAN // halves, halves).T.reshape(-1)
    a = pos[:, None]
    c = jnp.concatenate([pos, ATT_SPAN + pos])[None, :]
    rel = a - c + ATT_SPAN
    bias = jnp.moveaxis(_bucket_rows(tab, jnp.clip(rel, 0, ATT_SPAN) * dil), -1, 0)
    return jnp.where(((rel >= 0) & (rel <= ATT_SPAN))[None], bias, NEG)


def _att_sample_kernel(qkv_ref, c0_ref, c1_ref, c2_ref, b0_ref, b1_ref, b2_ref, o_ref, *, steps):
    for t in range(steps):
        outs, lses = [], []
        for g, (c_ref, b_ref) in enumerate(((c0_ref, b0_ref), (c1_ref, b1_ref), (c2_ref, b2_ref))):
            dil = ATT_GROUPS[g][1]
            q = qkv_ref[0, t, 0, g]
            if dil == 1:
                keys = jnp.concatenate([c_ref[0, t:, 0, 0], qkv_ref[0, :t + 1, 1, g]], axis=0)
                vals = jnp.concatenate([c_ref[0, t:, 0, 1], qkv_ref[0, :t + 1, 2, g]], axis=0)
            else:
                keys = jnp.concatenate([c_ref[0, :, t, 0], qkv_ref[0, t:t + 1, 1, g]], axis=0)
                vals = jnp.concatenate([c_ref[0, :, t, 1], qkv_ref[0, t:t + 1, 2, g]], axis=0)
            s = jnp.sum(keys * q[None], axis=-1, keepdims=True) + b_ref[...]
            m = jnp.max(s, axis=0)
            p = jnp.exp(s - m[None])
            den = jnp.sum(p, axis=0)
            outs.append(jnp.sum(p * vals, axis=0) / den)
            lses.append(m + jnp.log(den))
        m = jnp.maximum(jnp.maximum(lses[0], lses[1]), lses[2])
        es = [jnp.exp(l - m) for l in lses]
        o_ref[0, t] = (es[0] * outs[0] + es[1] * outs[1] + es[2] * outs[2]) / (es[0] + es[1] + es[2])


def _att_sample(qkv, caches, biases):
    nb, steps = qkv.shape[:2]
    assert steps <= min(d for _, d in ATT_GROUPS[1:]) and all(c.shape[1] == w for c, (w, _) in zip(caches, ATT_GROUPS))
    views, view_specs = [], []
    for c, (win, dil) in zip(caches, ATT_GROUPS):
        views.append(c.reshape(nb, ATT_SPAN, dil, 2, ATT_G, ATT_HD))
        view_specs.append(pl.BlockSpec((1, ATT_SPAN, min(dil, steps), 2, ATT_G, ATT_HD), lambda b: (b, 0, 0, 0, 0, 0)))
    return pl.pallas_call(
        functools.partial(_att_sample_kernel, steps=steps),
        out_shape=jax.ShapeDtypeStruct((nb, steps, ATT_G, ATT_HD), F32),
        grid=(nb,),
        in_specs=[pl.BlockSpec((1, steps, 3, ATT_NG, ATT_G, ATT_HD), lambda b: (b, 0, 0, 0, 0, 0)),
                  *view_specs,
                  *[_resident((ATT_SPAN + 1, ATT_G, LANES))] * ATT_NG],
        out_specs=pl.BlockSpec((1, steps, ATT_G, ATT_HD), lambda b: (b, 0, 0, 0)),
        compiler_params=_params("parallel"),
        name="att_sample",
    )(qkv, *views, *biases)


def _window_shift_kernel(new_ref, old_hbm, out_hbm, buf, rsem, wsem, *, steps, chunks):
    s = pl.program_id(0)
    total = pl.num_programs(0)
    depth, rows = buf.shape[:2]

    def read(step, action):
        b, c, sl = step // chunks, step % chunks, step % depth

        @pl.when(c < chunks - 1)
        def _():
            getattr(pltpu.make_async_copy(old_hbm.at[b, pl.ds(c * rows + steps, rows)], buf.at[sl], rsem.at[sl]), action)()

        @pl.when(c == chunks - 1)
        def _():
            getattr(pltpu.make_async_copy(old_hbm.at[b, pl.ds(c * rows + steps, rows - steps)],
                                          buf.at[sl, pl.ds(0, rows - steps)], rsem.at[sl]), action)()

    def write(step, action):
        b, c, sl = step // chunks, step % chunks, step % depth
        getattr(pltpu.make_async_copy(buf.at[sl], out_hbm.at[b, pl.ds(c * rows, rows)], wsem.at[sl]), action)()

    @pl.when(s == 0)
    def _():
        for ahead in range(depth - 1):
            @pl.when(ahead < total)
            def _():
                read(ahead, "start")

    @pl.when(s >= 1)
    def _():
        write(s - 1, "wait")

    @pl.when(s + depth - 1 < total)
    def _():
        read(s + depth - 1, "start")

    read(s, "wait")

    @pl.when(s % chunks == chunks - 1)
    def _():
        buf[s % depth, rows - steps:rows] = new_ref[0]

    write(s, "start")

    @pl.when(s == total - 1)
    def _():
        write(s, "wait")


def _window_shift(old, new):
    nb, win = old.shape[:2]
    steps = new.shape[1]
    rows = min(win, WINDOW_SHIFT_ROWS)
    chunks = win // rows
    tail = old.shape[2:]
    any_spec = pl.BlockSpec(memory_space=pl.ANY)
    return pl.pallas_call(
        functools.partial(_window_shift_kernel, steps=steps, chunks=chunks),
        out_shape=jax.ShapeDtypeStruct(old.shape, old.dtype),
        grid=(nb * chunks,),
        in_specs=[pl.BlockSpec((1, steps) + tail, lambda s: (s // chunks, 0, 0, 0, 0)), any_spec],
        out_specs=any_spec,
        scratch_shapes=[pltpu.VMEM((WINDOW_SHIFT_DEPTH, rows) + tail, old.dtype),
                        pltpu.SemaphoreType.DMA((WINDOW_SHIFT_DEPTH,)), pltpu.SemaphoreType.DMA((WINDOW_SHIFT_DEPTH,))],
        compiler_params=_params("arbitrary"),
        name="window_shift",
    )(new, old)


def _router_logits(xn, wr_ref, br_ref):
    xh = xn.astype(BF16)
    xl = (xn - xh.astype(F32)).astype(BF16)
    return _dot(xh, wr_ref[0]) + (_dot(xh, wr_ref[1]) + _dot(xl, wr_ref[0])) + br_ref[...]


def _route(logits, grp):
    lane = lax.broadcasted_iota(jnp.int32, logits.shape, 1)
    is_group = lane < MOE_GROUPS
    mx = jnp.max(jnp.where(is_group, logits, NEG), axis=-1, keepdims=True)
    if grp is None:
        grp = jnp.min(jnp.where(jnp.logical_and(is_group, logits == mx), lane, LANES), axis=-1, keepdims=True)
    den = jnp.sum(jnp.where(is_group, jnp.exp(logits - mx), 0.0), axis=-1, keepdims=True)
    sel = jnp.sum(jnp.where(lane == grp, logits, 0.0), axis=-1, keepdims=True)
    p_group = jnp.exp(sel - mx) / den
    lo = MOE_GROUPS + MOE_PER_GROUP * grp
    le = jnp.where(jnp.logical_and(lane >= lo, lane < lo + MOE_PER_GROUP), logits, NEG)
    v1 = jnp.max(le, axis=-1, keepdims=True)
    i1 = jnp.min(jnp.where(le == v1, lane, LANES), axis=-1, keepdims=True)
    le = jnp.where(lane == i1, NEG, le)
    v2 = jnp.max(le, axis=-1, keepdims=True)
    i2 = jnp.min(jnp.where(le == v2, lane, LANES), axis=-1, keepdims=True)
    e2 = jnp.exp(v2 - v1)
    return grp, i1, i2, p_group / (1.0 + e2), p_group * e2 / (1.0 + e2)


MOE_PAIRS = MOE_PER_GROUP * (MOE_PER_GROUP - 1) // 2
MOE_BUCKETS = MOE_GROUPS * MOE_PAIRS


def _bucket_ids(x, g_ref, wr_ref, br_ref):
    logits = _router_logits(_rmsnorm_rows(x, g_ref[...]), wr_ref, br_ref)
    grp, i1, i2, _, _ = _route(logits, None)
    lo = MOE_GROUPS + MOE_PER_GROUP * grp
    a = jnp.minimum(i1, i2) - lo
    b = jnp.maximum(i1, i2) - lo
    pair = ((a * (2 * MOE_PER_GROUP - 1 - a)) >> 1) + b - a - 1
    return jnp.broadcast_to(grp * MOE_PAIRS + pair, logits.shape)


def _moe_dispatch_kernel(slot_ref, zt_ref, x_ref, xs_hbm, zero_ref, sem, zsem, *, n_rows, tile):
    i = pl.program_id(0)
    tm = x_ref.shape[0]
    count = jnp.minimum(tm, n_rows - i * tm)
    n_buckets = zt_ref.shape[0] - 2

    @pl.when(i == 0)
    def _():
        zero_ref[...] = jnp.zeros(zero_ref.shape, zero_ref.dtype)
        parts = tile // zero_ref.shape[0]

        def zero_tile(t, action):
            for part in range(parts):
                row = pl.multiple_of(t * tile + part * zero_ref.shape[0], zero_ref.shape[0])
                getattr(pltpu.make_async_copy(zero_ref, xs_hbm.at[pl.ds(row, zero_ref.shape[0])], zsem), action)()

        for action in ("start", "wait"):
            for b in range(n_buckets):
                @pl.when(zt_ref[b] >= 0)
                def _():
                    zero_tile(zt_ref[b], action)

            def unused(t, carry):
                zero_tile(t, action)
                return carry
            lax.fori_loop(zt_ref[n_buckets], zt_ref[n_buckets + 1], unused, 0)

    def start(k, carry):
        pltpu.make_async_copy(x_ref.at[pl.ds(k, 1)], xs_hbm.at[pl.ds(slot_ref[i * tm + k], 1)], sem).start()
        return carry

    def wait(k, carry):
        pltpu.make_async_copy(x_ref.at[pl.ds(0, 1)], xs_hbm.at[pl.ds(0, 1)], sem).wait()
        return carry

    @pl.when(count == tm)
    def _():
        for k in range(tm):
            start(k, 0)
        pltpu.make_async_copy(x_ref, xs_hbm.at[pl.ds(0, tm)], sem).wait()

    @pl.when(count < tm)
    def _():
        lax.fori_loop(0, count, start, 0)
        lax.fori_loop(0, count, wait, 0)


def _moe_dispatch(x, slot_of_row, zero_tiles, n_rows, n_slots, tm, tile):
    return pl.pallas_call(
        functools.partial(_moe_dispatch_kernel, n_rows=n_rows, tile=tile),
        out_shape=jax.ShapeDtypeStruct((n_slots, D_MODEL), F32),
        grid_spec=pltpu.PrefetchScalarGridSpec(
            num_scalar_prefetch=2,
            grid=(pl.cdiv(n_rows, tm),),
            in_specs=[pl.BlockSpec((tm, D_MODEL), lambda i, s, p: (i, 0))],
            out_specs=pl.BlockSpec(memory_space=pl.ANY),
            scratch_shapes=[pltpu.VMEM((ZERO_ROWS, D_MODEL), F32), pltpu.SemaphoreType.DMA(()),
                            pltpu.SemaphoreType.DMA(())]),
        compiler_params=_params("arbitrary"),
        name="moe_dispatch",
    )(slot_of_row, zero_tiles, x)


def _moe_collect_kernel(slot_ref, ys_hbm, o_ref, sem, *, row_off):
    i = pl.program_id(0)
    tm = o_ref.shape[0]

    for k in range(tm):
        pltpu.make_async_copy(ys_hbm.at[pl.ds(slot_ref[row_off + i * tm + k], 1)], o_ref.at[pl.ds(k, 1)], sem).start()
    pltpu.make_async_copy(ys_hbm.at[pl.ds(0, tm)], o_ref, sem).wait()


def _moe_collect(ys, slot_of_row, row_off, rows, tm):
    return pl.pallas_call(
        functools.partial(_moe_collect_kernel, row_off=row_off),
        out_shape=jax.ShapeDtypeStruct((rows, D_MODEL), F32),
        grid_spec=pltpu.PrefetchScalarGridSpec(
            num_scalar_prefetch=1,
            grid=(rows // tm,),
            in_specs=[pl.BlockSpec(memory_space=pl.ANY)],
            out_specs=pl.BlockSpec((tm, D_MODEL), lambda i, s: (i, 0)),
            scratch_shapes=[pltpu.SemaphoreType.DMA(())]),
        compiler_params=_params("arbitrary"),
        name="moe_collect",
    )(slot_of_row, ys)


def _moe_expert_kernel(ea_ref, eb_ref, used_ref, x_ref, gain_ref, wr_ref, br_ref,
                       wga_ref, wua_ref, wda_ref, wgb_ref, wub_ref, wdb_ref, y_ref):
    t = pl.program_id(0)

    @pl.when(t >= used_ref[0])
    def _():
        y_ref[...] = jnp.zeros(y_ref.shape, y_ref.dtype)

    @pl.when(t < used_ref[0])
    def _():
        x = x_ref[...]
        xn = _rmsnorm_rows(x, gain_ref[...])
        _, i1, i2, w1, w2 = _route(_router_logits(xn, wr_ref, br_ref), ea_ref[t] // MOE_PER_GROUP)
        xb = xn.astype(BF16)
        acc = x
        for e_ref, wg_ref, wu_ref, wd_ref in ((ea_ref, wga_ref, wua_ref, wda_ref), (eb_ref, wgb_ref, wub_ref, wdb_ref)):
            e_lane = MOE_GROUPS + e_ref[t]
            ce = jnp.where(i1 == e_lane, w1, 0.0) + jnp.where(i2 == e_lane, w2, 0.0)
            hg = _dot(xb, wg_ref[0])
            hid = hg * jax.nn.sigmoid(hg) * _dot(xb, wu_ref[0])
            acc = acc + ce * _dot(hid.astype(BF16), wd_ref[0])
        y_ref[...] = acc


def _moe_plan(bucket, tile, n_tiles):
    onehot = (bucket[:, None] == jnp.arange(MOE_BUCKETS)[None, :]).astype(jnp.int32)
    csum = jnp.cumsum(onehot, axis=0)
    counts = csum[-1]
    tiles_per = (counts + tile - 1) // tile
    tile_end = jnp.cumsum(tiles_per)
    tile_start = tile_end - tiles_per
    slot_of_row = jnp.sum(onehot * (tile_start[None, :] * tile + csum - 1), axis=1)
    tile_bucket = jnp.minimum(jnp.sum(jnp.arange(n_tiles)[:, None] >= tile_end[None, :], axis=1), MOE_BUCKETS - 1)
    pair = tile_bucket % MOE_PAIRS
    first = jnp.sum(pair[:, None] >= jnp.cumsum(jnp.arange(MOE_PER_GROUP - 1, 0, -1))[None, :], axis=1)
    second = pair - ((first * (2 * MOE_PER_GROUP - 1 - first)) >> 1) + first + 1
    base = (tile_bucket // MOE_PAIRS) * MOE_PER_GROUP
    used = tile_end[-1:]
    zero_tiles = jnp.concatenate([jnp.where(tiles_per > 0, tile_end - 1, -1), used, jnp.full((1,), n_tiles)])
    i32 = lambda v: v.astype(jnp.int32)
    return i32(slot_of_row), i32(base + first), i32(base + second), i32(used), i32(zero_tiles)


def _moe(x, buckets, n_prompt, n_sample, gain, wr_pair, br, wg, wu, wd):
    n = n_prompt + n_sample
    tile = MOE_TILE
    n_tiles = n // tile + MOE_BUCKETS
    slot_of_row, expert_a, expert_b, n_used, zero_tiles = _moe_plan(buckets[:n, 0], tile, n_tiles)
    xs = _moe_dispatch(x, slot_of_row, zero_tiles, n, n_tiles * tile, ROW_TILE, tile)
    w_spec = lambda shape, which: pl.BlockSpec((1,) + shape, lambda t, ea, eb, u: ((ea, eb)[which][t], 0, 0))
    const = lambda shape: pl.BlockSpec(shape, lambda t, ea, eb, u: (0,) * len(shape), pipeline_mode=pl.Buffered(1))
    w_specs = [w_spec(shape, which) for which in range(2)
               for shape in ((D_MODEL, MOE_HIDDEN), (D_MODEL, MOE_HIDDEN), (MOE_HIDDEN, D_MODEL))]
    ys = pl.pallas_call(
        _moe_expert_kernel,
        out_shape=jax.ShapeDtypeStruct((n_tiles * tile, D_MODEL), F32),
        grid_spec=pltpu.PrefetchScalarGridSpec(
            num_scalar_prefetch=3,
            grid=(n_tiles,),
            in_specs=[pl.BlockSpec((tile, D_MODEL), lambda t, ea, eb, u: (jnp.minimum(t, u[0] - 1), 0)),
                      const((1, D_MODEL)), const((2, D_MODEL, LANES)), const((1, LANES)), *w_specs],
            out_specs=pl.BlockSpec((tile, D_MODEL), lambda t, ea, eb, u: (t, 0))),
        compiler_params=_params("arbitrary"),
        name="moe_experts",
    )(expert_a, expert_b, n_used, xs, gain, wr_pair, br, wg, wu, wd, wg, wu, wd)
    return (_moe_collect(ys, slot_of_row, 0, n_prompt, ROW_TILE),
            _moe_collect(ys, slot_of_row, n_prompt, n_sample, n_sample))


def _rope_tables(pos):
    half = RET_DK // 2
    inv = 1.0 / (ROPE_BASE ** jnp.linspace(0.0, 1.0, half, dtype=F32))
    ang = jnp.repeat(pos.astype(F32)[:, None] * inv[None, :], 2, axis=-1)
    sign = jnp.where(jnp.arange(RET_DK) % 2 == 0, -1.0, 1.0).astype(F32)
    return jnp.cos(ang), jnp.sin(ang) * sign


def kernel(x_prompt, x_sample, state_ret, cache_win0, cache_win1, cache_win2, rel_bias, norm_mix, norm_ffn, w_in_ret, w_out_ret, w_in_att, q_norm, k_norm, w_out_att, w_router_group, b_router_group, w_router_expert, b_router_expert, w_gate, w_up, w_down):
    nb_p, len_p, _ = x_prompt.shape
    nb_s, len_s, _ = x_sample.shape
    n_p, n_s = nb_p * len_p, nb_s * len_s
    n_all = n_p + n_s
    assert n_p % ROW_TILE == 0 and n_p % n_s == 0 and len_p % RET_CHUNK == 0 and len_s <= SAMPLE_PAD
    assert all(len_p % (ATT_SPAN * d) == 0 for _, d in ATT_GROUPS)
    caches = (cache_win0[0], cache_win1[0], cache_win2[0])

    xp = x_prompt.reshape(n_p, D_MODEL)
    xs = x_sample.reshape(n_s, D_MODEL)
    row = lambda v: v.reshape(1, -1).astype(F32)

    def moe_weights(i):
        wr = jnp.zeros((D_MODEL, LANES), F32)
        wr = wr.at[:, :MOE_GROUPS].set(w_router_group[i]).at[:, MOE_GROUPS:MOE_GROUPS + MOE_EXPERTS].set(w_router_expert[i])
        br = jnp.zeros((1, LANES), F32)
        br = br.at[0, :MOE_GROUPS].set(b_router_group[i]).at[0, MOE_GROUPS:MOE_GROUPS + MOE_EXPERTS].set(b_router_expert[i])
        wr_hi = wr.astype(BF16)
        wr_pair = jnp.stack([wr_hi, (wr - wr_hi.astype(F32)).astype(BF16)])
        return (row(norm_ffn[i]), wr_pair, br, w_gate[i].astype(BF16), w_up[i].astype(BF16), w_down[i].astype(BF16))

    gain0 = row(norm_mix[0])
    w_in = w_in_ret[0].astype(BF16)
    w_out = w_out_ret[0].astype(BF16)
    cos_p, sin_p = _rope_tables(jnp.arange(len_p))
    pos_s = jnp.tile(PAST_LEN + jnp.arange(SAMPLE_PAD), nb_s)
    cos_s, sin_s = _rope_tables(pos_s)
    xs_pad = jnp.pad(x_sample, ((0, 0), (0, SAMPLE_PAD - len_s), (0, 0))).reshape(nb_s * SAMPLE_PAD, D_MODEL)

    proj_p = _ret_proj(xp, 0, n_p, ROW_TILE, gain0, w_in, cos_p, sin_p)
    proj_s = _ret_proj(xs_pad, 0, nb_s * SAMPLE_PAD, nb_s * SAMPLE_PAD, gain0, w_in, cos_s, sin_s)
    zero_state = jnp.zeros((nb_p, RET_HEADS, RET_DK, RET_DV), F32)
    o_p, ret_p = _ret_core(proj_p, zero_state, nb_p, len_p // RET_CHUNK, RET_CHUNK, RET_CHUNK)
    o_s, ret_s = _ret_core(proj_s, state_ret[0].astype(F32), nb_s, 1, SAMPLE_PAD, len_s)
    o_s = o_s.reshape(nb_s, SAMPLE_PAD, RET_VW)[:, :len_s].reshape(n_s, RET_VW)
    moe0 = moe_weights(0)
    x1, buckets = _mm_res(o_p, w_out, xp, 0, ROW_TILE, tail=_mm_res(o_s, w_out, xs, 0, n_s), route=moe0[:3])
    x2, x2_s = _moe(x1, buckets, n_p, n_s, *moe0)

    gain1 = row(norm_mix[1])
    w_in = w_in_att[0].astype(BF16)
    w_out = w_out_att[0].astype(BF16)
    q_gain = row(q_norm[0]) * (ATT_HD ** -0.5)
    k_gain = row(k_norm[0])
    qkv_s = _att_proj(x2_s, 0, n_s, n_s, gain1, w_in, q_gain, k_gain, F32)
    qkv_s = qkv_s.reshape(nb_s, len_s, 3, ATT_NG, ATT_G, ATT_HD)

    outs, win_p, win_s, sample_bias = [], [], [], []
    for g, (win, dil) in enumerate(ATT_GROUPS):
        tab = rel_bias[:, g * ATT_G:(g + 1) * ATT_G].astype(F32)
        w_g = jnp.concatenate([w_in[:, (i * ATT_NG + g) * ATT_W:(i * ATT_NG + g + 1) * ATT_W] for i in range(3)], axis=1)
        if dil == 1:
            halves = 1
            qkv_g = _att_proj(x2, 0, n_p, ROW_TILE, gain1, w_g, q_gain, k_gain, BF16).reshape(nb_p, 1, len_p, 3 * ATT_W)
        else:
            halves = len(_residue_picks(dil)[1][0])
            qkv_g = _att_proj_dil(x2, nb_p, len_p, dil, gain1, w_g, q_gain, k_gain)
        outs.append(_att_band(qkv_g, dil, _band_bias(tab, dil, halves)))
        win_p.append(_window_rows(qkv_g, dil, caches[g].dtype)[None])
        win_s.append(_window_shift(caches[g], qkv_s[:, :, 1:, g].astype(caches[g].dtype))[None])
        b = _bucket_rows(tab, (ATT_SPAN - jnp.arange(ATT_SPAN + 1)) * dil)
        sample_bias.append(jnp.broadcast_to(b[:, :, None], (ATT_SPAN + 1, ATT_G, LANES)))
    o_s = _att_sample(qkv_s, caches, sample_bias)
    x3_s = _mm_res(o_s.reshape(n_s, ATT_W).astype(BF16), w_out, x2_s, 0, n_s)
    moe1 = moe_weights(1)
    x3, buckets = _att_merge(outs, w_out, x2, n_p, ROW_TILE, x3_s, moe1[:3])
    y_p, y_s = _moe(x3, buckets, n_p, n_s, *moe1)

    y_p = y_p.reshape(nb_p, len_p, D_MODEL)
    y_s = y_s.reshape(nb_s, len_s, D_MODEL)
    dt = state_ret.dtype
    return (y_p, y_s, ret_p[None].astype(dt), ret_s[None].astype(dt),
            win_p[0], win_s[0], win_p[1], win_s[1], win_p[2], win_s[2])
```

```python
import functools

import jax
import jax.numpy as jnp
from jax import lax
from jax.experimental import pallas as pl
from jax.experimental.pallas import tpu as pltpu

F32 = jnp.float32
BF16 = jnp.bfloat16

D_MODEL = 1024
PAST_LEN = 16384
RET_HEADS = 4
RET_DK = 256
RET_DV = 512
RET_QK = RET_HEADS * RET_DK
RET_VW = RET_HEADS * RET_DV
RET_IN = 2 * RET_QK + 2 * RET_VW
ROPE_BASE = 10000.0
ATT_GROUPS = ((128, 1), (512, 4), (2048, 16))
ATT_NG = len(ATT_GROUPS)
ATT_G = 8
ATT_HD = 128
ATT_W = ATT_G * ATT_HD
ATT_IN = 3 * ATT_NG * ATT_W
ATT_SPAN = 128
N_BUCKETS = 32
MAX_DISTANCE = 2048
MOE_GROUPS = 4
MOE_PER_GROUP = 4
MOE_EXPERTS = MOE_GROUPS * MOE_PER_GROUP
MOE_HIDDEN = 512
EPS = 1e-6
NEG = -1e30

LANES = 128
V7X_VMEM_LIMIT_BYTES = 56 * 1024 * 1024
ROW_TILE = 512
RET_CHUNK = 256
SAMPLE_PAD = 16
MOE_TILE = 512
WINDOW_SHIFT_ROWS = 256
WINDOW_SHIFT_DEPTH = 3
ZERO_ROWS = 256
ATT_BAND_UNITS = 4
OUT_LSE = ATT_W + LANES


def _params(*sem):
    return pltpu.CompilerParams(dimension_semantics=sem, vmem_limit_bytes=V7X_VMEM_LIMIT_BYTES)


def _resident(shape):
    return pl.BlockSpec(shape, lambda *_: (0,) * len(shape), pipeline_mode=pl.Buffered(1))


def _rmsnorm_rows(x, gain):
    return x * lax.rsqrt(jnp.mean(x * x, axis=-1, keepdims=True) + EPS) * gain


def _dot(a, b):
    return jnp.dot(a, b, preferred_element_type=F32)


def _dot_nt(a, b):
    return lax.dot_general(a, b, (((1,), (1,)), ((), ())), preferred_element_type=F32)


def _dot_tn(a, b):
    return lax.dot_general(a, b, (((0,), (0,)), ((), ())), preferred_element_type=F32)


def _ret_proj_kernel(x_ref, g_ref, w_ref, cos_ref, sin_ref, o_ref):
    xn = _rmsnorm_rows(x_ref[...], g_ref[...]).astype(BF16)
    cos = cos_ref[...]
    sin = sin_ref[...]
    even = (lax.broadcasted_iota(jnp.int32, cos.shape, 1) & 1) == 0
    for c in range(2 * RET_HEADS):
        sl = slice(c * RET_DK, (c + 1) * RET_DK)
        acc = _dot(xn, w_ref[:, sl])
        partner = jnp.where(even, pltpu.roll(acc, RET_DK - 1, 1), pltpu.roll(acc, 1, 1))
        r = acc * cos + partner * sin
        if c >= RET_HEADS:
            r = r * (RET_DK ** -0.5)
        o_ref[:, sl] = r.astype(o_ref.dtype)
    for c in range(2 * RET_QK // RET_DV, RET_IN // RET_DV):
        sl = slice(c * RET_DV, (c + 1) * RET_DV)
        o_ref[:, sl] = _dot(xn, w_ref[:, sl]).astype(o_ref.dtype)


def _ret_proj(x, row_off, rows, tm, gain, w, cos, sin):
    tab_blocks = cos.shape[0] // tm
    return pl.pallas_call(
        _ret_proj_kernel,
        out_shape=jax.ShapeDtypeStruct((rows, RET_IN), BF16),
        grid=(rows // tm,),
        in_specs=[
            pl.BlockSpec((tm, D_MODEL), lambda i: (i + row_off, 0)),
            _resident((1, D_MODEL)),
            _resident((D_MODEL, RET_IN)),
            pl.BlockSpec((tm, RET_DK), lambda i: (i % tab_blocks, 0)),
            pl.BlockSpec((tm, RET_DK), lambda i: (i % tab_blocks, 0)),
        ],
        out_specs=pl.BlockSpec((tm, RET_IN), lambda i: (i, 0)),
        compiler_params=_params("parallel"),
        name="ret_proj",
    )(x, gain, w, cos, sin)


def _window_move(q, slot, old_hbm, out_hbm, buf, sem, shift, chunks, kind, action):
    b, c = q // chunks, q % chunks
    rows = buf.shape[1]
    if kind == "write":
        getattr(pltpu.make_async_copy(buf.at[slot], out_hbm.at[b, pl.ds(c * rows, rows)], sem.at[slot]), action)()
        return
    for last, n in ((False, rows), (True, rows - shift)):
        @pl.when(c == chunks - 1 if last else c != chunks - 1)
        def _():
            getattr(pltpu.make_async_copy(old_hbm.at[b, pl.ds(c * rows + shift, n)], buf.at[slot, pl.ds(0, n)],
                                          sem.at[slot]), action)()
            if last and action == "wait":
                buf[slot, rows - shift:rows] = jnp.zeros((shift,) + buf.shape[2:], buf.dtype)


def _ret_core_kernel(cdec_ref, q_ref, k_ref, v_ref, gt_ref, s0_ref, intra_ref, qdec_ref, kdec_ref, *rest,
                     shift=None, chunks=None):
    if shift is None:
        o_ref, st_ref = rest
    else:
        old_hbm, o_ref, st_ref, moved_hbm, buf, rsem, wsem = rest
        per_step = buf.shape[0]
        step = pl.program_id(0) * pl.num_programs(1) + pl.program_id(1)
        last_step = pl.num_programs(0) * pl.num_programs(1) - 1
        move = lambda q, j, sem, kind, action: _window_move(q, j, old_hbm, moved_hbm, buf, sem, shift, chunks, kind, action)

        @pl.when(step > 0)
        def _():
            for j in range(per_step):
                move((step - 1) * per_step + j, j, wsem, "write", "wait")

        for j in range(per_step):
            move(step * per_step + j, j, rsem, "read", "start")

    @pl.when(pl.program_id(1) == 0)
    def _():
        st_ref[...] = s0_ref[...]

    for h in range(RET_HEADS):
        qk = slice(h * RET_DK, (h + 1) * RET_DK)
        vv = slice(h * RET_DV, (h + 1) * RET_DV)
        qh, kh, vh = q_ref[:, qk], k_ref[:, qk], v_ref[:, vv]
        s = st_ref[0, h]
        sc = _dot_nt(qh, kh) * intra_ref[h]
        o = _dot(sc.astype(BF16), vh)
        o = o + _dot(qh, s.astype(BF16)) * jnp.concatenate([qdec_ref[h]] * (RET_DV // LANES), axis=1)
        kd = (kh.astype(F32) * jnp.concatenate([kdec_ref[h]] * (RET_DK // LANES), axis=1)).astype(BF16)
        st_ref[0, h] = s * cdec_ref[h] + _dot_tn(kd, vh)
        o = o * lax.rsqrt(jnp.mean(o * o, axis=-1, keepdims=True) + EPS)
        g = gt_ref[:, vv].astype(F32)
        o_ref[:, vv] = (o * (g * jax.nn.sigmoid(g))).astype(o_ref.dtype)

    if shift is not None:
        for j in range(per_step):
            move(step * per_step + j, j, rsem, "read", "wait")
            move(step * per_step + j, j, wsem, "write", "start")

        @pl.when(step == last_step)
        def _():
            for j in range(per_step):
                move(step * per_step + j, j, wsem, "write", "wait")


def _ret_decay_tables(chunk, n_real):
    log_g = jnp.log(1.0 - 2.0 ** (-5.0 - jnp.arange(RET_HEADS, dtype=F32)))
    idx = jnp.arange(chunk, dtype=F32)
    diff = idx[:, None] - idx[None, :]
    intra = jnp.where(diff >= 0, jnp.exp(jnp.maximum(diff, 0.0)[None] * log_g[:, None, None]), 0.0)
    q_dec = jnp.exp((idx + 1.0)[None, :] * log_g[:, None])
    k_dec = jnp.where(idx[None, :] < n_real, jnp.exp((n_real - 1.0 - idx)[None, :] * log_g[:, None]), 0.0)
    c_dec = jnp.exp(n_real * log_g)
    rep = lambda t: jnp.broadcast_to(t[:, :, None], (RET_HEADS, chunk, LANES))
    return c_dec, intra, rep(q_dec), rep(k_dec)


def _ret_core(proj, s0, nseq, nchunk, chunk, n_real, window=None, shift=None):
    c_dec, intra, q_dec, k_dec = _ret_decay_tables(chunk, n_real)
    rows = nseq * nchunk * chunk
    row = lambda b, c: b * nchunk + c
    st_spec = pl.BlockSpec((1, RET_HEADS, RET_DK, RET_DV), lambda b, c: (b, 0, 0, 0))
    in_specs = [
        pl.BlockSpec(memory_space=pltpu.SMEM),
        pl.BlockSpec((chunk, RET_QK), lambda b, c: (row(b, c), 0)),
        pl.BlockSpec((chunk, RET_QK), lambda b, c: (row(b, c), 1)),
        pl.BlockSpec((chunk, RET_VW), lambda b, c: (row(b, c), 1)),
        pl.BlockSpec((chunk, RET_VW), lambda b, c: (row(b, c), 2)),
        st_spec,
        _resident((RET_HEADS, chunk, chunk)),
        _resident((RET_HEADS, chunk, LANES)),
        _resident((RET_HEADS, chunk, LANES)),
    ]
    args = [c_dec, proj, proj, proj, proj, s0, intra, q_dec, k_dec]
    out_shape = [jax.ShapeDtypeStruct((rows, RET_VW), BF16), jax.ShapeDtypeStruct((nseq, RET_HEADS, RET_DK, RET_DV), F32)]
    out_specs = [pl.BlockSpec((chunk, RET_VW), lambda b, c: (row(b, c), 0)), st_spec]
    scratch, kwargs = [], {}
    if window is not None:
        chunks = window.shape[1] // WINDOW_SHIFT_ROWS
        per_step = window.shape[0] * chunks // (nseq * nchunk)
        assert per_step * nseq * nchunk == window.shape[0] * chunks
        in_specs.append(pl.BlockSpec(memory_space=pl.ANY))
        args.append(window)
        out_shape.append(jax.ShapeDtypeStruct(window.shape, window.dtype))
        out_specs.append(pl.BlockSpec(memory_space=pl.ANY))
        scratch = [pltpu.VMEM((per_step, WINDOW_SHIFT_ROWS) + window.shape[2:], window.dtype),
                   pltpu.SemaphoreType.DMA((per_step,)), pltpu.SemaphoreType.DMA((per_step,))]
        kwargs = dict(shift=shift, chunks=chunks)
    return pl.pallas_call(
        functools.partial(_ret_core_kernel, **kwargs),
        out_shape=tuple(out_shape),
        grid=(nseq, nchunk),
        in_specs=in_specs,
        out_specs=tuple(out_specs),
        scratch_shapes=scratch,
        compiler_params=_params("arbitrary", "arbitrary"),
        name="ret_core",
    )(*args)


def _window_tail_kernel(new_ref, win_hbm, out_hbm, sem):
    b = pl.program_id(0)
    steps = new_ref.shape[1]
    cp = pltpu.make_async_copy(new_ref.at[0], out_hbm.at[b, pl.ds(out_hbm.shape[1] - steps, steps)], sem)
    cp.start()
    cp.wait()


def _window_tail(shifted, new):
    nb, steps = new.shape[:2]
    return pl.pallas_call(
        _window_tail_kernel,
        out_shape=jax.ShapeDtypeStruct(shifted.shape, shifted.dtype),
        grid=(nb,),
        in_specs=[pl.BlockSpec((1, steps) + new.shape[2:], lambda b: (b, 0, 0, 0, 0)), pl.BlockSpec(memory_space=pl.ANY)],
        out_specs=pl.BlockSpec(memory_space=pl.ANY),
        scratch_shapes=[pltpu.SemaphoreType.DMA(())],
        input_output_aliases={1: 0},
        compiler_params=_params("arbitrary"),
        name="window_tail",
    )(new, shifted)


def _with_tail_and_buckets(rows_fn, tail_ref, route_refs, o_ref, b_ref):
    last = pl.num_programs(0) - 1

    def emit(rows):
        o_ref[...] = rows
        b_ref[...] = _bucket_ids(rows, *route_refs)

    @pl.when(pl.program_id(0) < last)
    def _():
        emit(rows_fn())

    @pl.when(pl.program_id(0) == last)
    def _():
        pad = jnp.zeros((o_ref.shape[0] - tail_ref.shape[0], o_ref.shape[1]), o_ref.dtype)
        emit(jnp.concatenate([tail_ref[...], pad], axis=0))


def _route_specs():
    return [_resident((1, D_MODEL)), _resident((2, D_MODEL, LANES)), _resident((1, LANES))]


def _mm_res_kernel(a_ref, w_ref, x_ref, *rest):
    rows_fn = lambda: x_ref[...] + _dot(a_ref[...], w_ref[...])
    if len(rest) == 1:
        rest[0][...] = rows_fn()
    else:
        tail_ref, *route_refs, o_ref, b_ref = rest
        _with_tail_and_buckets(rows_fn, tail_ref, route_refs, o_ref, b_ref)


def _mm_res(a, w, x, x_off, tm, tail=None, route=None):
    m, k = a.shape
    nblk = m // tm
    clamp = (lambda i: jnp.minimum(i, nblk - 1)) if tail is not None else (lambda i: i)
    in_specs = [
        pl.BlockSpec((tm, k), lambda i: (clamp(i), 0)),
        _resident((k, D_MODEL)),
        pl.BlockSpec((tm, D_MODEL), lambda i: (clamp(i) + x_off, 0)),
    ]
    args = [a, w, x]
    steps = nblk
    out_shape = jax.ShapeDtypeStruct((nblk * tm, D_MODEL), F32)
    out_specs = pl.BlockSpec((tm, D_MODEL), lambda i: (i, 0))
    if tail is not None:
        steps = nblk + 1
        in_specs += [_resident(tail.shape)] + _route_specs()
        args += [tail, *route]
        out_shape = (jax.ShapeDtypeStruct((steps * tm, D_MODEL), F32), jax.ShapeDtypeStruct((steps * tm, LANES), jnp.int32))
        out_specs = (out_specs, pl.BlockSpec((tm, LANES), lambda i: (i, 0)))
    return pl.pallas_call(
        _mm_res_kernel,
        out_shape=out_shape,
        grid=(steps,),
        in_specs=in_specs,
        out_specs=out_specs,
        compiler_params=_params("arbitrary"),
        name="mm_res",
    )(*args)


ATT_PROJ_CHUNK = 4 * ATT_HD


def _att_proj_chunk(xn, w_ref, qg_ref, kg_ref, c, width):
    acc = _dot(xn, w_ref[:, c * ATT_PROJ_CHUNK:(c + 1) * ATT_PROJ_CHUNK])
    n_qk = 2 * (width // 3) // ATT_PROJ_CHUNK
    if c >= n_qk:
        return acc
    gain = qg_ref[...] if c < n_qk // 2 else kg_ref[...]
    heads = []
    for hh in range(ATT_PROJ_CHUNK // ATT_HD):
        a = acc[:, hh * ATT_HD:(hh + 1) * ATT_HD]
        heads.append(a * lax.rsqrt(jnp.mean(a * a, axis=-1, keepdims=True) + EPS) * gain)
    return jnp.concatenate(heads, axis=1)


def _att_proj_kernel(x_ref, g_ref, w_ref, qg_ref, kg_ref, o_ref):
    xn = _rmsnorm_rows(x_ref[...], g_ref[...]).astype(BF16)
    width = o_ref.shape[1]
    for c in range(width // ATT_PROJ_CHUNK):
        sl = slice(c * ATT_PROJ_CHUNK, (c + 1) * ATT_PROJ_CHUNK)
        o_ref[:, sl] = _att_proj_chunk(xn, w_ref, qg_ref, kg_ref, c, width).astype(o_ref.dtype)


def _att_proj_dil_kernel(x_hbm, g_ref, w_ref, qg_ref, kg_ref, o_ref, xbuf, sem, *, picks, rows):
    i = pl.program_id(0)
    slot = i % 2

    def fetch(step, sl, action):
        k = 0
        for mids in picks:
            for s in mids:
                cp = pltpu.make_async_copy(x_hbm.at[pl.ds(step * rows, rows), s], xbuf.at[sl, pl.ds(k * rows, rows)],
                                           sem.at[sl])
                getattr(cp, action)()
                k += 1

    @pl.when(i == 0)
    def _():
        fetch(0, 0, "start")

    @pl.when(i + 1 < pl.num_programs(0))
    def _():
        fetch(i + 1, 1 - slot, "start")

    fetch(i, slot, "wait")
    per = rows * len(picks[0])
    width = o_ref.shape[3]
    xn = _rmsnorm_rows(xbuf[slot], g_ref[...]).astype(BF16)
    for c in range(width // ATT_PROJ_CHUNK):
        sl = slice(c * ATT_PROJ_CHUNK, (c + 1) * ATT_PROJ_CHUNK)
        val = _att_proj_chunk(xn, w_ref, qg_ref, kg_ref, c, width).astype(o_ref.dtype)
        for r in range(len(picks)):
            o_ref[0, r, :, sl] = val[r * per:(r + 1) * per]


def _residue_picks(dil):
    if dil % 8 == 0:
        return dil, [[r] for r in range(dil)]
    assert 8 % dil == 0 and dil > 1
    return 8, [[r + dil * e for e in range(8 // dil)] for r in range(dil)]


def _att_proj_dil(x, nb, length, dil, gain, w, q_gain, k_gain):
    sub, picks = _residue_picks(dil)
    width = w.shape[1]
    rows = ATT_SPAN // 2
    per = rows * len(picks[0])
    ls = length // dil
    steps_per_seq = ls // per
    return pl.pallas_call(
        functools.partial(_att_proj_dil_kernel, picks=picks, rows=rows),
        out_shape=jax.ShapeDtypeStruct((nb, dil, ls, width), BF16),
        grid=(nb * steps_per_seq,),
        in_specs=[
            pl.BlockSpec(memory_space=pl.ANY),
            _resident((1, D_MODEL)),
            _resident((D_MODEL, width)),
            _resident((1, ATT_HD)),
            _resident((1, ATT_HD)),
        ],
        out_specs=pl.BlockSpec((1, dil, per, width), lambda i: (i // steps_per_seq, 0, i % steps_per_seq, 0)),
        scratch_shapes=[pltpu.VMEM((2, dil * per, D_MODEL), F32), pltpu.SemaphoreType.DMA((2,))],
        compiler_params=_params("arbitrary"),
        name="att_proj_dil",
    )(x.reshape(x.shape[0] // sub, sub, D_MODEL), gain, w, q_gain, k_gain)


def _att_proj(x, row_off, rows, tm, gain, w, q_gain, k_gain, out_dtype):
    width = w.shape[1]
    return pl.pallas_call(
        _att_proj_kernel,
        out_shape=jax.ShapeDtypeStruct((rows, width), out_dtype),
        grid=(rows // tm,),
        in_specs=[
            pl.BlockSpec((tm, D_MODEL), lambda i: (i + row_off, 0)),
            _resident((1, D_MODEL)),
            _resident((D_MODEL, width)),
            _resident((1, ATT_HD)),
            _resident((1, ATT_HD)),
        ],
        out_specs=pl.BlockSpec((tm, width), lambda i: (i, 0)),
        compiler_params=_params("parallel"),
        name="att_proj",
    )(x, gain, w, q_gain, k_gain)


def _att_band_kernel(q_ref, kp_ref, kc_ref, vp_ref, vc_ref, bias_ref, o_ref, *scratch, picks, blocks):
    n = pl.program_id(1)
    res = q_ref.shape[1]
    col = lax.broadcasted_iota(jnp.int32, (ATT_SPAN, 2 * ATT_SPAN), 1)
    no_prev = jnp.logical_and(col < ATT_SPAN, n == 0)
    lane = lax.broadcasted_iota(jnp.int32, (ATT_SPAN, LANES), 1)

    def unit(rr, j):
        lse_tile = jnp.zeros((ATT_SPAN, LANES), F32)
        heads = []
        for h in range(ATT_G):
            sl = slice(h * ATT_HD, (h + 1) * ATT_HD)
            if j == 0:
                kcat = jnp.concatenate([kp_ref[0, rr, :, sl], kc_ref[0, rr, 0:ATT_SPAN, sl]], axis=0)
                vcat = jnp.concatenate([vp_ref[0, rr, :, sl], vc_ref[0, rr, 0:ATT_SPAN, sl]], axis=0)
            else:
                kcat = kc_ref[0, rr, (j - 1) * ATT_SPAN:(j + 1) * ATT_SPAN, sl]
                vcat = vc_ref[0, rr, (j - 1) * ATT_SPAN:(j + 1) * ATT_SPAN, sl]
            s = _dot_nt(q_ref[0, rr, j * ATT_SPAN:(j + 1) * ATT_SPAN, sl], kcat) + bias_ref[h]
            if j == 0:
                s = jnp.where(no_prev, NEG, s)
            m = jnp.max(s, axis=-1, keepdims=True)
            p = jnp.exp(s - m)
            den = jnp.sum(p, axis=-1, keepdims=True)
            heads.append(_dot(p.astype(BF16), vcat) / den)
            lse_tile = jnp.where(lane == h, m + jnp.log(den), lse_tile)
        return jnp.concatenate(heads + [lse_tile], axis=1)

    if picks is None:
        for j in range(blocks):
            o_ref[0, j * ATT_SPAN:(j + 1) * ATT_SPAN, :] = unit(0, j)
        return

    obuf, sem = scratch
    b, g = pl.program_id(0), pl.program_id(2)
    steps = pl.num_programs(0) * pl.num_programs(1) * pl.num_programs(2)
    flat = (b * pl.num_programs(1) + n) * pl.num_programs(2) + g
    slot = flat % 2
    per = ATT_SPAN // len(picks[0])

    def writeback(sl, action):
        for rr in range(res):
            for e, mid in enumerate(picks[0]):
                cp = pltpu.make_async_copy(obuf.at[sl, rr, pl.ds(e * per, per)],
                                           o_ref.at[b, pl.ds(n * per, per), g * res + rr + mid], sem.at[sl])
                getattr(cp, action)()

    @pl.when(flat >= 2)
    def _():
        writeback(slot, "wait")

    for rr in range(res):
        obuf[slot, rr] = unit(rr, 0)
    writeback(slot, "start")

    @pl.when(flat == steps - 1)
    def _():
        writeback(slot, "wait")

        @pl.when(steps >= 2)
        def _():
            writeback(1 - slot, "wait")


def _att_band(qkv, dil, bias):
    nb, _, ls, _ = qkv.shape
    units = ATT_BAND_UNITS
    if dil == 1:
        picks, res, blocks = None, 1, units
        out_shape = (nb, ls, OUT_LSE)
        out_spec = pl.BlockSpec((1, blocks * ATT_SPAN, OUT_LSE), lambda b, n, g: (b, n, 0))
        scratch = []
    else:
        sub, picks = _residue_picks(dil)
        res, blocks = units, 1
        out_shape = (nb, ls * dil // sub, sub, OUT_LSE)
        out_spec = pl.BlockSpec(memory_space=pl.ANY)
        scratch = [pltpu.VMEM((2, res, ATT_SPAN, OUT_LSE), F32), pltpu.SemaphoreType.DMA((2,))]
    cur = lambda cb: pl.BlockSpec((1, res, blocks * ATT_SPAN, ATT_W), lambda b, n, g: (b, g, n, cb))
    prev = lambda cb: pl.BlockSpec((1, res, ATT_SPAN, ATT_W), lambda b, n, g: (b, g, jnp.maximum(n * blocks - 1, 0), cb))
    out = pl.pallas_call(
        functools.partial(_att_band_kernel, picks=picks, blocks=blocks),
        out_shape=jax.ShapeDtypeStruct(out_shape, F32),
        grid=(nb, ls // (blocks * ATT_SPAN), dil // res),
        in_specs=[cur(0), prev(1), cur(1), prev(2), cur(2), _resident((ATT_G, ATT_SPAN, 2 * ATT_SPAN))],
        out_specs=out_spec,
        scratch_shapes=scratch,
        compiler_params=_params("arbitrary", "arbitrary", "arbitrary"),
        name="att_band",
    )(qkv, qkv, qkv, qkv, qkv, bias)
    return out.reshape(nb * ls * dil, OUT_LSE)


def _window_rows_kernel(k_ref, v_ref, o_ref):
    halves = o_ref.shape[2]
    per = o_ref.shape[1]
    for e in range(halves):
        for kv, ref in enumerate((k_ref, v_ref)):
            for h in range(ATT_G):
                o_ref[0, :, e, 0, kv, h, :] = ref[0, 0, e * per:(e + 1) * per, h * ATT_HD:(h + 1) * ATT_HD].astype(o_ref.dtype)


def _window_rows(qkv, dil, dtype):
    nb, _, ls, _ = qkv.shape
    halves = 1 if dil == 1 else len(_residue_picks(dil)[1][0])
    per = ATT_SPAN // halves
    last = ls // ATT_SPAN - 1
    out = pl.pallas_call(
        _window_rows_kernel,
        out_shape=jax.ShapeDtypeStruct((nb, per, halves, dil, 2, ATT_G, ATT_HD), dtype),
        grid=(nb, dil),
        in_specs=[pl.BlockSpec((1, 1, ATT_SPAN, ATT_W), lambda b, r: (b, r, last, 1)),
                  pl.BlockSpec((1, 1, ATT_SPAN, ATT_W), lambda b, r: (b, r, last, 2))],
        out_specs=pl.BlockSpec((1, per, halves, 1, 2, ATT_G, ATT_HD), lambda b, r: (b, 0, 0, r, 0, 0, 0)),
        compiler_params=_params("parallel", "parallel"),
        name="window_rows",
    )(qkv, qkv)
    return out.reshape(nb, ATT_SPAN * dil, 2, ATT_G, ATT_HD)


def _att_merge_kernel(o0_ref, o1_ref, o2_ref, w_ref, x_ref, tail_ref, *rest):
    *route_refs, out_ref, b_ref = rest

    def rows_fn():
        acc = x_ref[...]
        refs = (o0_ref, o1_ref, o2_ref)
        for h in range(ATT_G):
            sl = slice(h * ATT_HD, (h + 1) * ATT_HD)
            lses = [r[:, ATT_W + h:ATT_W + h + 1] for r in refs]
            m = jnp.maximum(jnp.maximum(lses[0], lses[1]), lses[2])
            es = [jnp.exp(l - m) for l in lses]
            merged = (es[0] * o0_ref[:, sl] + es[1] * o1_ref[:, sl] + es[2] * o2_ref[:, sl]) / (es[0] + es[1] + es[2])
            acc = acc + _dot(merged.astype(BF16), w_ref[sl, :])
        return acc

    _with_tail_and_buckets(rows_fn, tail_ref, route_refs, out_ref, b_ref)


def _att_merge(outs, w, x, rows, tm, tail, route):
    nblk = rows // tm
    row_spec = lambda width: pl.BlockSpec((tm, width), lambda i: (jnp.minimum(i, nblk - 1), 0))
    return pl.pallas_call(
        _att_merge_kernel,
        out_shape=(jax.ShapeDtypeStruct(((nblk + 1) * tm, D_MODEL), F32),
                   jax.ShapeDtypeStruct(((nblk + 1) * tm, LANES), jnp.int32)),
        grid=(nblk + 1,),
        in_specs=[row_spec(OUT_LSE), row_spec(OUT_LSE), row_spec(OUT_LSE), _resident((ATT_W, D_MODEL)),
                  row_spec(D_MODEL), _resident(tail.shape)] + _route_specs(),
        out_specs=(pl.BlockSpec((tm, D_MODEL), lambda i: (i, 0)), pl.BlockSpec((tm, LANES), lambda i: (i, 0))),
        compiler_params=_params("arbitrary"),
        name="att_merge",
    )(*outs, w, x, tail, *route)


def _t5_bucket(dist):
    max_exact = N_BUCKETS // 2
    d32 = jnp.maximum(dist, 1).astype(F32)
    large = max_exact + (jnp.log(d32 / max_exact) / jnp.log(MAX_DISTANCE / max_exact)
                         * (N_BUCKETS - max_exact)).astype(jnp.int32)
    return jnp.where(dist < max_exact, dist, jnp.minimum(large, N_BUCKETS - 1))


def _bucket_rows(tab, dist):
    onehot = jax.nn.one_hot(_t5_bucket(dist), N_BUCKETS, dtype=F32)
    return jnp.dot(onehot, tab, precision=lax.Precision.HIGHEST)


def _band_bias(tab, dil, halves):
    pos = jnp.arange(ATT_SPAN).reshape(ATT_SPAN // halves, halves).T.reshape(-1)
    a = pos[:, None]
    c = jnp.concatenate([pos, ATT_SPAN + pos])[None, :]
    rel = a - c + ATT_SPAN
    bias = jnp.moveaxis(_bucket_rows(tab, jnp.clip(rel, 0, ATT_SPAN) * dil), -1, 0)
    return jnp.where(((rel >= 0) & (rel <= ATT_SPAN))[None], bias, NEG)


def _att_sample_kernel(qkv_ref, c0_ref, c1_ref, c2_ref, b0_ref, b1_ref, b2_ref, o_ref, *, steps):
    for t in range(steps):
        outs, lses = [], []
        for g, (c_ref, b_ref) in enumerate(((c0_ref, b0_ref), (c1_ref, b1_ref), (c2_ref, b2_ref))):
            dil = ATT_GROUPS[g][1]
            q = qkv_ref[0, t, 0, g]
            if dil == 1:
                keys = jnp.concatenate([c_ref[0, t:, 0, 0], qkv_ref[0, :t + 1, 1, g]], axis=0)
                vals = jnp.concatenate([c_ref[0, t:, 0, 1], qkv_ref[0, :t + 1, 2, g]], axis=0)
            else:
                keys = jnp.concatenate([c_ref[0, :, t, 0], qkv_ref[0, t:t + 1, 1, g]], axis=0)
                vals = jnp.concatenate([c_ref[0, :, t, 1], qkv_ref[0, t:t + 1, 2, g]], axis=0)
            s = jnp.sum(keys * q[None], axis=-1, keepdims=True) + b_ref[...]
            m = jnp.max(s, axis=0)
            p = jnp.exp(s - m[None])
            den = jnp.sum(p, axis=0)
            outs.append(jnp.sum(p * vals, axis=0) / den)
            lses.append(m + jnp.log(den))
        m = jnp.maximum(jnp.maximum(lses[0], lses[1]), lses[2])
        es = [jnp.exp(l - m) for l in lses]
        o_ref[0, t] = (es[0] * outs[0] + es[1] * outs[1] + es[2] * outs[2]) / (es[0] + es[1] + es[2])


def _att_sample(qkv, caches, biases):
    nb, steps = qkv.shape[:2]
    assert steps <= min(d for _, d in ATT_GROUPS[1:]) and all(c.shape[1] == w for c, (w, _) in zip(caches, ATT_GROUPS))
    views, view_specs = [], []
    for c, (win, dil) in zip(caches, ATT_GROUPS):
        views.append(c.reshape(nb, ATT_SPAN, dil, 2, ATT_G, ATT_HD))
        view_specs.append(pl.BlockSpec((1, ATT_SPAN, min(dil, steps), 2, ATT_G, ATT_HD), lambda b: (b, 0, 0, 0, 0, 0)))
    return pl.pallas_call(
        functools.partial(_att_sample_kernel, steps=steps),
        out_shape=jax.ShapeDtypeStruct((nb, steps, ATT_G, ATT_HD), F32),
        grid=(nb,),
        in_specs=[pl.BlockSpec((1, steps, 3, ATT_NG, ATT_G, ATT_HD), lambda b: (b, 0, 0, 0, 0, 0)),
                  *view_specs,
                  *[_resident((ATT_SPAN + 1, ATT_G, LANES))] * ATT_NG],
        out_specs=pl.BlockSpec((1, steps, ATT_G, ATT_HD), lambda b: (b, 0, 0, 0)),
        compiler_params=_params("parallel"),
        name="att_sample",
    )(qkv, *views, *biases)


def _window_shift_kernel(new_ref, old_hbm, out_hbm, buf, rsem, wsem, *, steps, chunks):
    s = pl.program_id(0)
    total = pl.num_programs(0)
    depth, rows = buf.shape[:2]

    def read(step, action):
        b, c, sl = step // chunks, step % chunks, step % depth

        @pl.when(c < chunks - 1)
        def _():
            getattr(pltpu.make_async_copy(old_hbm.at[b, pl.ds(c * rows + steps, rows)], buf.at[sl], rsem.at[sl]), action)()

        @pl.when(c == chunks - 1)
        def _():
            getattr(pltpu.make_async_copy(old_hbm.at[b, pl.ds(c * rows + steps, rows - steps)],
                                          buf.at[sl, pl.ds(0, rows - steps)], rsem.at[sl]), action)()

    def write(step, action):
        b, c, sl = step // chunks, step % chunks, step % depth
        getattr(pltpu.make_async_copy(buf.at[sl], out_hbm.at[b, pl.ds(c * rows, rows)], wsem.at[sl]), action)()

    @pl.when(s == 0)
    def _():
        for ahead in range(depth - 1):
            @pl.when(ahead < total)
            def _():
                read(ahead, "start")

    @pl.when(s >= 1)
    def _():
        write(s - 1, "wait")

    @pl.when(s + depth - 1 < total)
    def _():
        read(s + depth - 1, "start")

    read(s, "wait")

    @pl.when(s % chunks == chunks - 1)
    def _():
        buf[s % depth, rows - steps:rows] = new_ref[0]

    write(s, "start")

    @pl.when(s == total - 1)
    def _():
        write(s, "wait")


def _window_shift(old, new):
    nb, win = old.shape[:2]
    steps = new.shape[1]
    rows = min(win, WINDOW_SHIFT_ROWS)
    chunks = win // rows
    tail = old.shape[2:]
    any_spec = pl.BlockSpec(memory_space=pl.ANY)
    return pl.pallas_call(
        functools.partial(_window_shift_kernel, steps=steps, chunks=chunks),
        out_shape=jax.ShapeDtypeStruct(old.shape, old.dtype),
        grid=(nb * chunks,),
        in_specs=[pl.BlockSpec((1, steps) + tail, lambda s: (s // chunks, 0, 0, 0, 0)), any_spec],
        out_specs=any_spec,
        scratch_shapes=[pltpu.VMEM((WINDOW_SHIFT_DEPTH, rows) + tail, old.dtype),
                        pltpu.SemaphoreType.DMA((WINDOW_SHIFT_DEPTH,)), pltpu.SemaphoreType.DMA((WINDOW_SHIFT_DEPTH,))],
        compiler_params=_params("arbitrary"),
        name="window_shift",
    )(new, old)


def _router_logits(xn, wr_ref, br_ref):
    xh = xn.astype(BF16)
    xl = (xn - xh.astype(F32)).astype(BF16)
    return _dot(xh, wr_ref[0]) + (_dot(xh, wr_ref[1]) + _dot(xl, wr_ref[0])) + br_ref[...]


def _route(logits, grp):
    lane = lax.broadcasted_iota(jnp.int32, logits.shape, 1)
    is_group = lane < MOE_GROUPS
    mx = jnp.max(jnp.where(is_group, logits, NEG), axis=-1, keepdims=True)
    if grp is None:
        grp = jnp.min(jnp.where(jnp.logical_and(is_group, logits == mx), lane, LANES), axis=-1, keepdims=True)
    den = jnp.sum(jnp.where(is_group, jnp.exp(logits - mx), 0.0), axis=-1, keepdims=True)
    sel = jnp.sum(jnp.where(lane == grp, logits, 0.0), axis=-1, keepdims=True)
    p_group = jnp.exp(sel - mx) / den
    lo = MOE_GROUPS + MOE_PER_GROUP * grp
    le = jnp.where(jnp.logical_and(lane >= lo, lane < lo + MOE_PER_GROUP), logits, NEG)
    v1 = jnp.max(le, axis=-1, keepdims=True)
    i1 = jnp.min(jnp.where(le == v1, lane, LANES), axis=-1, keepdims=True)
    le = jnp.where(lane == i1, NEG, le)
    v2 = jnp.max(le, axis=-1, keepdims=True)
    i2 = jnp.min(jnp.where(le == v2, lane, LANES), axis=-1, keepdims=True)
    e2 = jnp.exp(v2 - v1)
    return grp, i1, i2, p_group / (1.0 + e2), p_group * e2 / (1.0 + e2)


MOE_PAIRS = MOE_PER_GROUP * (MOE_PER_GROUP - 1) // 2
MOE_BUCKETS = MOE_GROUPS * MOE_PAIRS


def _bucket_ids(x, g_ref, wr_ref, br_ref):
    logits = _router_logits(_rmsnorm_rows(x, g_ref[...]), wr_ref, br_ref)
    grp, i1, i2, _, _ = _route(logits, None)
    lo = MOE_GROUPS + MOE_PER_GROUP * grp
    a = jnp.minimum(i1, i2) - lo
    b = jnp.maximum(i1, i2) - lo
    pair = ((a * (2 * MOE_PER_GROUP - 1 - a)) >> 1) + b - a - 1
    return jnp.broadcast_to(grp * MOE_PAIRS + pair, logits.shape)


def _moe_dispatch_kernel(slot_ref, zt_ref, x_ref, xs_hbm, zero_ref, sem, zsem, *, n_rows, tile):
    i = pl.program_id(0)
    tm = x_ref.shape[0]
    count = jnp.minimum(tm, n_rows - i * tm)
    n_buckets = zt_ref.shape[0] - 2

    @pl.when(i == 0)
    def _():
        zero_ref[...] = jnp.zeros(zero_ref.shape, zero_ref.dtype)
        parts = tile // zero_ref.shape[0]

        def zero_tile(t, action):
            for part in range(parts):
                row = pl.multiple_of(t * tile + part * zero_ref.shape[0], zero_ref.shape[0])
                getattr(pltpu.make_async_copy(zero_ref, xs_hbm.at[pl.ds(row, zero_ref.shape[0])], zsem), action)()

        for action in ("start", "wait"):
            for b in range(n_buckets):
                @pl.when(zt_ref[b] >= 0)
                def _():
                    zero_tile(zt_ref[b], action)

            def unused(t, carry):
                zero_tile(t, action)
                return carry
            lax.fori_loop(zt_ref[n_buckets], zt_ref[n_buckets + 1], unused, 0)

    def start(k, carry):
        pltpu.make_async_copy(x_ref.at[pl.ds(k, 1)], xs_hbm.at[pl.ds(slot_ref[i * tm + k], 1)], sem).start()
        return carry

    def wait(k, carry):
        pltpu.make_async_copy(x_ref.at[pl.ds(0, 1)], xs_hbm.at[pl.ds(0, 1)], sem).wait()
        return carry

    @pl.when(count == tm)
    def _():
        for k in range(tm):
            start(k, 0)
        pltpu.make_async_copy(x_ref, xs_hbm.at[pl.ds(0, tm)], sem).wait()

    @pl.when(count < tm)
    def _():
        lax.fori_loop(0, count, start, 0)
        lax.fori_loop(0, count, wait, 0)


def _moe_dispatch(x, slot_of_row, zero_tiles, n_rows, n_slots, tm, tile):
    return pl.pallas_call(
        functools.partial(_moe_dispatch_kernel, n_rows=n_rows, tile=tile),
        out_shape=jax.ShapeDtypeStruct((n_slots, D_MODEL), F32),
        grid_spec=pltpu.PrefetchScalarGridSpec(
            num_scalar_prefetch=2,
            grid=(pl.cdiv(n_rows, tm),),
            in_specs=[pl.BlockSpec((tm, D_MODEL), lambda i, s, p: (i, 0))],
            out_specs=pl.BlockSpec(memory_space=pl.ANY),
            scratch_shapes=[pltpu.VMEM((ZERO_ROWS, D_MODEL), F32), pltpu.SemaphoreType.DMA(()),
                            pltpu.SemaphoreType.DMA(())]),
        compiler_params=_params("arbitrary"),
        name="moe_dispatch",
    )(slot_of_row, zero_tiles, x)


def _moe_collect_kernel(slot_ref, ys_hbm, o_ref, sem, *, row_off):
    i = pl.program_id(0)
    tm = o_ref.shape[0]

    for k in range(tm):
        pltpu.make_async_copy(ys_hbm.at[pl.ds(slot_ref[row_off + i * tm + k], 1)], o_ref.at[pl.ds(k, 1)], sem).start()
    pltpu.make_async_copy(ys_hbm.at[pl.ds(0, tm)], o_ref, sem).wait()


def _moe_collect(ys, slot_of_row, row_off, rows, tm):
    return pl.pallas_call(
        functools.partial(_moe_collect_kernel, row_off=row_off),
        out_shape=jax.ShapeDtypeStruct((rows, D_MODEL), F32),
        grid_spec=pltpu.PrefetchScalarGridSpec(
            num_scalar_prefetch=1,
            grid=(rows // tm,),
            in_specs=[pl.BlockSpec(memory_space=pl.ANY)],
            out_specs=pl.BlockSpec((tm, D_MODEL), lambda i, s: (i, 0)),
            scratch_shapes=[pltpu.SemaphoreType.DMA(())]),
        compiler_params=_params("arbitrary"),
        name="moe_collect",
    )(slot_of_row, ys)


def _moe_expert_kernel(ea_ref, eb_ref, used_ref, x_ref, gain_ref, wr_ref, br_ref,
                       wga_ref, wua_ref, wda_ref, wgb_ref, wub_ref, wdb_ref, y_ref):
    t = pl.program_id(0)

    @pl.when(t >= used_ref[0])
    def _():
        y_ref[...] = jnp.zeros(y_ref.shape, y_ref.dtype)

    @pl.when(t < used_ref[0])
    def _():
        x = x_ref[...]
        xn = _rmsnorm_rows(x, gain_ref[...])
        _, i1, i2, w1, w2 = _route(_router_logits(xn, wr_ref, br_ref), ea_ref[t] // MOE_PER_GROUP)
        xb = xn.astype(BF16)
        acc = x
        for e_ref, wg_ref, wu_ref, wd_ref in ((ea_ref, wga_ref, wua_ref, wda_ref), (eb_ref, wgb_ref, wub_ref, wdb_ref)):
            e_lane = MOE_GROUPS + e_ref[t]
            ce = jnp.where(i1 == e_lane, w1, 0.0) + jnp.where(i2 == e_lane, w2, 0.0)
            hg = _dot(xb, wg_ref[0, 0].astype(BF16))
            hid = hg * jax.nn.sigmoid(hg) * _dot(xb, wu_ref[0, 0].astype(BF16))
            acc = acc + ce * _dot(hid.astype(BF16), wd_ref[0, 0].astype(BF16))
        y_ref[...] = acc


def _moe_plan(bucket, tile, n_tiles):
    onehot = (bucket[:, None] == jnp.arange(MOE_BUCKETS)[None, :]).astype(jnp.int32)
    csum = jnp.cumsum(onehot, axis=0)
    counts = csum[-1]
    tiles_per = (counts + tile - 1) // tile
    tile_end = jnp.cumsum(tiles_per)
    tile_start = tile_end - tiles_per
    slot_of_row = jnp.sum(onehot * (tile_start[None, :] * tile + csum - 1), axis=1)
    tile_bucket = jnp.minimum(jnp.sum(jnp.arange(n_tiles)[:, None] >= tile_end[None, :], axis=1), MOE_BUCKETS - 1)
    pair = tile_bucket % MOE_PAIRS
    first = jnp.sum(pair[:, None] >= jnp.cumsum(jnp.arange(MOE_PER_GROUP - 1, 0, -1))[None, :], axis=1)
    second = pair - ((first * (2 * MOE_PER_GROUP - 1 - first)) >> 1) + first + 1
    base = (tile_bucket // MOE_PAIRS) * MOE_PER_GROUP
    used = tile_end[-1:]
    zero_tiles = jnp.concatenate([jnp.where(tiles_per > 0, tile_end - 1, -1), used, jnp.full((1,), n_tiles)])
    i32 = lambda v: v.astype(jnp.int32)
    return i32(slot_of_row), i32(base + first), i32(base + second), i32(used), i32(zero_tiles)


def _moe(x, buckets, n_prompt, n_sample, layer, gain, wr_pair, br, wg, wu, wd):
    n = n_prompt + n_sample
    tile = MOE_TILE
    n_tiles = n // tile + MOE_BUCKETS
    slot_of_row, expert_a, expert_b, n_used, zero_tiles = _moe_plan(buckets[:n, 0], tile, n_tiles)
    xs = _moe_dispatch(x, slot_of_row, zero_tiles, n, n_tiles * tile, ROW_TILE, tile)
    w_spec = lambda shape, which: pl.BlockSpec((1, 1) + shape, lambda t, ea, eb, u: (layer, (ea, eb)[which][t], 0, 0))
    const = lambda shape: pl.BlockSpec(shape, lambda t, ea, eb, u: (0,) * len(shape), pipeline_mode=pl.Buffered(1))
    w_specs = [w_spec(shape, which) for which in range(2)
               for shape in ((D_MODEL, MOE_HIDDEN), (D_MODEL, MOE_HIDDEN), (MOE_HIDDEN, D_MODEL))]
    ys = pl.pallas_call(
        _moe_expert_kernel,
        out_shape=jax.ShapeDtypeStruct((n_tiles * tile, D_MODEL), F32),
        grid_spec=pltpu.PrefetchScalarGridSpec(
            num_scalar_prefetch=3,
            grid=(n_tiles,),
            in_specs=[pl.BlockSpec((tile, D_MODEL), lambda t, ea, eb, u: (jnp.minimum(t, u[0] - 1), 0)),
                      const((1, D_MODEL)), const((2, D_MODEL, LANES)), const((1, LANES)), *w_specs],
            out_specs=pl.BlockSpec((tile, D_MODEL), lambda t, ea, eb, u: (t, 0))),
        compiler_params=_params("arbitrary"),
        name="moe_experts",
    )(expert_a, expert_b, n_used, xs, gain, wr_pair, br, wg, wu, wd, wg, wu, wd)
    return (_moe_collect(ys, slot_of_row, 0, n_prompt, ROW_TILE),
            _moe_collect(ys, slot_of_row, n_prompt, n_sample, n_sample))


def _rope_tables(pos):
    half = RET_DK // 2
    inv = 1.0 / (ROPE_BASE ** jnp.linspace(0.0, 1.0, half, dtype=F32))
    ang = jnp.repeat(pos.astype(F32)[:, None] * inv[None, :], 2, axis=-1)
    sign = jnp.where(jnp.arange(RET_DK) % 2 == 0, -1.0, 1.0).astype(F32)
    return jnp.cos(ang), jnp.sin(ang) * sign


def kernel(x_prompt, x_sample, state_ret, cache_win0, cache_win1, cache_win2, rel_bias, norm_mix, norm_ffn, w_in_ret, w_out_ret, w_in_att, q_norm, k_norm, w_out_att, w_router_group, b_router_group, w_router_expert, b_router_expert, w_gate, w_up, w_down):
    nb_p, len_p, _ = x_prompt.shape
    nb_s, len_s, _ = x_sample.shape
    n_p, n_s = nb_p * len_p, nb_s * len_s
    n_all = n_p + n_s
    assert n_p % ROW_TILE == 0 and n_p % n_s == 0 and len_p % RET_CHUNK == 0 and len_s <= SAMPLE_PAD
    assert all(len_p % (ATT_SPAN * d) == 0 for _, d in ATT_GROUPS)
    caches = (cache_win0[0], cache_win1[0], cache_win2[0])

    xp = x_prompt.reshape(n_p, D_MODEL)
    xs = x_sample.reshape(n_s, D_MODEL)
    row = lambda v: v.reshape(1, -1).astype(F32)

    def moe_weights(i):
        wr = jnp.zeros((D_MODEL, LANES), F32)
        wr = wr.at[:, :MOE_GROUPS].set(w_router_group[i]).at[:, MOE_GROUPS:MOE_GROUPS + MOE_EXPERTS].set(w_router_expert[i])
        br = jnp.zeros((1, LANES), F32)
        br = br.at[0, :MOE_GROUPS].set(b_router_group[i]).at[0, MOE_GROUPS:MOE_GROUPS + MOE_EXPERTS].set(b_router_expert[i])
        wr_hi = wr.astype(BF16)
        wr_pair = jnp.stack([wr_hi, (wr - wr_hi.astype(F32)).astype(BF16)])
        return (row(norm_ffn[i]), wr_pair, br, w_gate, w_up, w_down)

    gain0 = row(norm_mix[0])
    w_in = w_in_ret[0].astype(BF16)
    w_out = w_out_ret[0].astype(BF16)
    cos_p, sin_p = _rope_tables(jnp.arange(len_p))
    pos_s = jnp.tile(PAST_LEN + jnp.arange(SAMPLE_PAD), nb_s)
    cos_s, sin_s = _rope_tables(pos_s)
    xs_pad = jnp.pad(x_sample, ((0, 0), (0, SAMPLE_PAD - len_s), (0, 0))).reshape(nb_s * SAMPLE_PAD, D_MODEL)

    proj_p = _ret_proj(xp, 0, n_p, ROW_TILE, gain0, w_in, cos_p, sin_p)
    proj_s = _ret_proj(xs_pad, 0, nb_s * SAMPLE_PAD, nb_s * SAMPLE_PAD, gain0, w_in, cos_s, sin_s)
    zero_state = jnp.zeros((nb_p, RET_HEADS, RET_DK, RET_DV), F32)
    big = max(range(ATT_NG), key=lambda g: caches[g].shape[1])
    ret_steps = nb_p * (len_p // RET_CHUNK)
    big_chunks = nb_s * (caches[big].shape[1] // WINDOW_SHIFT_ROWS)
    moved = None
    if caches[big].shape[1] % WINDOW_SHIFT_ROWS == 0 and big_chunks % ret_steps == 0 and big_chunks // ret_steps <= 4:
        o_p, ret_p, moved = _ret_core(proj_p, zero_state, nb_p, len_p // RET_CHUNK, RET_CHUNK, RET_CHUNK,
                                      window=caches[big], shift=len_s)
    else:
        o_p, ret_p = _ret_core(proj_p, zero_state, nb_p, len_p // RET_CHUNK, RET_CHUNK, RET_CHUNK)
    o_s, ret_s = _ret_core(proj_s, state_ret[0].astype(F32), nb_s, 1, SAMPLE_PAD, len_s)
    o_s = o_s.reshape(nb_s, SAMPLE_PAD, RET_VW)[:, :len_s].reshape(n_s, RET_VW)
    moe0 = moe_weights(0)
    x1, buckets = _mm_res(o_p, w_out, xp, 0, ROW_TILE, tail=_mm_res(o_s, w_out, xs, 0, n_s), route=moe0[:3])
    x2, x2_s = _moe(x1, buckets, n_p, n_s, 0, *moe0)

    gain1 = row(norm_mix[1])
    w_in = w_in_att[0].astype(BF16)
    w_out = w_out_att[0].astype(BF16)
    q_gain = row(q_norm[0]) * (ATT_HD ** -0.5)
    k_gain = row(k_norm[0])
    qkv_s = _att_proj(x2_s, 0, n_s, n_s, gain1, w_in, q_gain, k_gain, F32)
    qkv_s = qkv_s.reshape(nb_s, len_s, 3, ATT_NG, ATT_G, ATT_HD)

    outs, win_p, win_s, sample_bias = [], [], [], []
    for g, (win, dil) in enumerate(ATT_GROUPS):
        tab = rel_bias[:, g * ATT_G:(g + 1) * ATT_G].astype(F32)
        w_g = jnp.concatenate([w_in[:, (i * ATT_NG + g) * ATT_W:(i * ATT_NG + g + 1) * ATT_W] for i in range(3)], axis=1)
        if dil == 1:
            halves = 1
            qkv_g = _att_proj(x2, 0, n_p, ROW_TILE, gain1, w_g, q_gain, k_gain, BF16).reshape(nb_p, 1, len_p, 3 * ATT_W)
        else:
            halves = len(_residue_picks(dil)[1][0])
            qkv_g = _att_proj_dil(x2, nb_p, len_p, dil, gain1, w_g, q_gain, k_gain)
        outs.append(_att_band(qkv_g, dil, _band_bias(tab, dil, halves)))
        win_p.append(_window_rows(qkv_g, dil, caches[g].dtype)[None])
        new_rows = qkv_s[:, :, 1:, g].astype(caches[g].dtype)
        if g == big and moved is not None:
            win_s.append(_window_tail(moved, new_rows)[None])
        else:
            win_s.append(_window_shift(caches[g], new_rows)[None])
        b = _bucket_rows(tab, (ATT_SPAN - jnp.arange(ATT_SPAN + 1)) * dil)
        sample_bias.append(jnp.broadcast_to(b[:, :, None], (ATT_SPAN + 1, ATT_G, LANES)))
    o_s = _att_sample(qkv_s, caches, sample_bias)
    x3_s = _mm_res(o_s.reshape(n_s, ATT_W).astype(BF16), w_out, x2_s, 0, n_s)
    moe1 = moe_weights(1)
    x3, buckets = _att_merge(outs, w_out, x2, n_p, ROW_TILE, x3_s, moe1[:3])
    y_p, y_s = _moe(x3, buckets, n_p, n_s, 1, *moe1)

    y_p = y_p.reshape(nb_p, len_p, D_MODEL)
    y_s = y_s.reshape(nb_s, len_s, D_MODEL)
    dt = state_ret.dtype
    return (y_p, y_s, ret_p[None].astype(dt), ret_s[None].astype(dt),
            win_p[0], win_s[0], win_p[1], win_s[1], win_p[2], win_s[2])
```

```python
import functools

import jax
import jax.numpy as jnp
from jax import lax
from jax.experimental import pallas as pl
from jax.experimental.pallas import tpu as pltpu

F32 = jnp.float32
BF16 = jnp.bfloat16

D_MODEL = 1024
PAST_LEN = 16384
RET_HEADS = 4
RET_DK = 256
RET_DV = 512
RET_QK = RET_HEADS * RET_DK
RET_VW = RET_HEADS * RET_DV
RET_IN = 2 * RET_QK + 2 * RET_VW
ROPE_BASE = 10000.0
ATT_GROUPS = ((128, 1), (512, 4), (2048, 16))
ATT_NG = len(ATT_GROUPS)
ATT_G = 8
ATT_HD = 128
ATT_W = ATT_G * ATT_HD
ATT_IN = 3 * ATT_NG * ATT_W
ATT_SPAN = 128
N_BUCKETS = 32
MAX_DISTANCE = 2048
MOE_GROUPS = 4
MOE_PER_GROUP = 4
MOE_EXPERTS = MOE_GROUPS * MOE_PER_GROUP
MOE_HIDDEN = 512
EPS = 1e-6
NEG = -1e30

LANES = 128
V7X_VMEM_LIMIT_BYTES = 56 * 1024 * 1024
ROW_TILE = 512
RET_CHUNK = 256
SAMPLE_PAD = 16
MOE_TILE = 512
WINDOW_SHIFT_ROWS = 256
WINDOW_SHIFT_DEPTH = 3
ZERO_ROWS = 256
ATT_BAND_UNITS = 4
OUT_LSE = ATT_W + LANES


def _params(*sem):
    return pltpu.CompilerParams(dimension_semantics=sem, vmem_limit_bytes=V7X_VMEM_LIMIT_BYTES)


def _resident(shape):
    return pl.BlockSpec(shape, lambda *_: (0,) * len(shape), pipeline_mode=pl.Buffered(1))


def _rmsnorm_rows(x, gain):
    return x * lax.rsqrt(jnp.mean(x * x, axis=-1, keepdims=True) + EPS) * gain


def _dot(a, b):
    return jnp.dot(a, b, preferred_element_type=F32)


def _dot_nt(a, b):
    return lax.dot_general(a, b, (((1,), (1,)), ((), ())), preferred_element_type=F32)


def _dot_tn(a, b):
    return lax.dot_general(a, b, (((0,), (0,)), ((), ())), preferred_element_type=F32)


def _ret_proj_kernel(x_ref, g_ref, w_ref, cos_ref, sin_ref, *rest, shift=None, chunks=None):
    if shift is None:
        (o_ref,) = rest
    else:
        old_hbm, o_ref, moved_hbm, buf, rsem, wsem = rest
        background = _BackgroundShift(pl.program_id(0), pl.num_programs(0) - 1, old_hbm, moved_hbm, buf, rsem, wsem,
                                      shift, chunks)
        background.begin()
    xn = _rmsnorm_rows(x_ref[...], g_ref[...]).astype(BF16)
    cos = cos_ref[...]
    sin = sin_ref[...]
    even = (lax.broadcasted_iota(jnp.int32, cos.shape, 1) & 1) == 0
    for c in range(2 * RET_HEADS):
        sl = slice(c * RET_DK, (c + 1) * RET_DK)
        acc = _dot(xn, w_ref[:, sl])
        partner = jnp.where(even, pltpu.roll(acc, RET_DK - 1, 1), pltpu.roll(acc, 1, 1))
        r = acc * cos + partner * sin
        if c >= RET_HEADS:
            r = r * (RET_DK ** -0.5)
        o_ref[:, sl] = r.astype(o_ref.dtype)
    for c in range(2 * RET_QK // RET_DV, RET_IN // RET_DV):
        sl = slice(c * RET_DV, (c + 1) * RET_DV)
        o_ref[:, sl] = _dot(xn, w_ref[:, sl]).astype(o_ref.dtype)
    if shift is not None:
        background.end()


def _ret_proj(x, row_off, rows, tm, gain, w, cos, sin, window=None, shift=None):
    tab_blocks = cos.shape[0] // tm
    in_specs = [
        pl.BlockSpec((tm, D_MODEL), lambda i: (i + row_off, 0)),
        _resident((1, D_MODEL)),
        _resident((D_MODEL, RET_IN)),
        pl.BlockSpec((tm, RET_DK), lambda i: (i % tab_blocks, 0)),
        pl.BlockSpec((tm, RET_DK), lambda i: (i % tab_blocks, 0)),
    ]
    args = [x, gain, w, cos, sin]
    out_shape = [jax.ShapeDtypeStruct((rows, RET_IN), BF16)]
    out_specs = [pl.BlockSpec((tm, RET_IN), lambda i: (i, 0))]
    scratch, kwargs = [], {}
    if window is not None:
        in_spec, shape, out_spec, scratch, chunks = _background_shift_specs(window, rows // tm)
        in_specs.append(in_spec)
        args.append(window)
        out_shape.append(shape)
        out_specs.append(out_spec)
        kwargs = dict(shift=shift, chunks=chunks)
    out = pl.pallas_call(
        functools.partial(_ret_proj_kernel, **kwargs),
        out_shape=tuple(out_shape),
        grid=(rows // tm,),
        in_specs=in_specs,
        out_specs=tuple(out_specs),
        scratch_shapes=scratch,
        compiler_params=_params("arbitrary"),
        name="ret_proj",
    )(*args)
    return out if window is not None else out[0]


def _window_move(q, slot, old_hbm, out_hbm, buf, sem, shift, chunks, kind, action):
    b, c = q // chunks, q % chunks
    rows = buf.shape[1]
    if kind == "write":
        getattr(pltpu.make_async_copy(buf.at[slot], out_hbm.at[b, pl.ds(c * rows, rows)], sem.at[slot]), action)()
        return
    for last, n in ((False, rows), (True, rows - shift)):
        @pl.when(c == chunks - 1 if last else c != chunks - 1)
        def _():
            getattr(pltpu.make_async_copy(old_hbm.at[b, pl.ds(c * rows + shift, n)], buf.at[slot, pl.ds(0, n)],
                                          sem.at[slot]), action)()
            if last and action == "wait":
                buf[slot, rows - shift:rows] = jnp.zeros((shift,) + buf.shape[2:], buf.dtype)


class _BackgroundShift:
    def __init__(self, step, last_step, old_hbm, out_hbm, buf, rsem, wsem, shift, chunks):
        self.step, self.last_step, self.per_step = step, last_step, buf.shape[0] // 2
        self.args = (old_hbm, out_hbm, buf)
        self.rsem, self.wsem, self.shift, self.chunks = rsem, wsem, shift, chunks

    def _each(self, step, sem, kind, action):
        for j in range(self.per_step):
            slot = (step % 2) * self.per_step + j
            _window_move(step * self.per_step + j, slot, *self.args, sem, self.shift, self.chunks, kind, action)

    def begin(self):
        @pl.when(self.step >= 2)
        def _():
            self._each(self.step - 2, self.wsem, "write", "wait")
        self._each(self.step, self.rsem, "read", "start")

    def end(self):
        self._each(self.step, self.rsem, "read", "wait")
        self._each(self.step, self.wsem, "write", "start")

        @pl.when(self.step == self.last_step)
        def _():
            self._each(self.step, self.wsem, "write", "wait")

            @pl.when(self.step >= 1)
            def _():
                self._each(self.step - 1, self.wsem, "write", "wait")


def _background_shift_specs(window, steps):
    chunks = window.shape[1] // WINDOW_SHIFT_ROWS
    per_step = window.shape[0] * chunks // steps
    assert per_step * steps == window.shape[0] * chunks
    any_spec = pl.BlockSpec(memory_space=pl.ANY)
    scratch = [pltpu.VMEM((2 * per_step, WINDOW_SHIFT_ROWS) + window.shape[2:], window.dtype),
               pltpu.SemaphoreType.DMA((2 * per_step,)), pltpu.SemaphoreType.DMA((2 * per_step,))]
    return any_spec, jax.ShapeDtypeStruct(window.shape, window.dtype), any_spec, scratch, chunks


def _can_shift_behind(window, steps):
    chunks = window.shape[0] * (window.shape[1] // WINDOW_SHIFT_ROWS)
    return window.shape[1] % WINDOW_SHIFT_ROWS == 0 and chunks % steps == 0 and 1 <= chunks // steps <= 2


def _ret_core_kernel(cdec_ref, q_ref, k_ref, v_ref, gt_ref, s0_ref, intra_ref, qdec_ref, kdec_ref, *rest,
                     shift=None, chunks=None):
    if shift is None:
        o_ref, st_ref = rest
    else:
        old_hbm, o_ref, st_ref, moved_hbm, buf, rsem, wsem = rest
        background = _BackgroundShift(pl.program_id(0) * pl.num_programs(1) + pl.program_id(1),
                                      pl.num_programs(0) * pl.num_programs(1) - 1,
                                      old_hbm, moved_hbm, buf, rsem, wsem, shift, chunks)
        background.begin()

    @pl.when(pl.program_id(1) == 0)
    def _():
        st_ref[...] = s0_ref[...]

    for h in range(RET_HEADS):
        qk = slice(h * RET_DK, (h + 1) * RET_DK)
        vv = slice(h * RET_DV, (h + 1) * RET_DV)
        qh, kh, vh = q_ref[:, qk], k_ref[:, qk], v_ref[:, vv]
        s = st_ref[0, h]
        sc = _dot_nt(qh, kh) * intra_ref[h]
        o = _dot(sc.astype(BF16), vh)
        o = o + _dot(qh, s.astype(BF16)) * jnp.concatenate([qdec_ref[h]] * (RET_DV // LANES), axis=1)
        kd = (kh.astype(F32) * jnp.concatenate([kdec_ref[h]] * (RET_DK // LANES), axis=1)).astype(BF16)
        st_ref[0, h] = s * cdec_ref[h] + _dot_tn(kd, vh)
        o = o * lax.rsqrt(jnp.mean(o * o, axis=-1, keepdims=True) + EPS)
        g = gt_ref[:, vv].astype(F32)
        o_ref[:, vv] = (o * (g * jax.nn.sigmoid(g))).astype(o_ref.dtype)

    if shift is not None:
        background.end()


def _ret_decay_tables(chunk, n_real):
    log_g = jnp.log(1.0 - 2.0 ** (-5.0 - jnp.arange(RET_HEADS, dtype=F32)))
    idx = jnp.arange(chunk, dtype=F32)
    diff = idx[:, None] - idx[None, :]
    intra = jnp.where(diff >= 0, jnp.exp(jnp.maximum(diff, 0.0)[None] * log_g[:, None, None]), 0.0)
    q_dec = jnp.exp((idx + 1.0)[None, :] * log_g[:, None])
    k_dec = jnp.where(idx[None, :] < n_real, jnp.exp((n_real - 1.0 - idx)[None, :] * log_g[:, None]), 0.0)
    c_dec = jnp.exp(n_real * log_g)
    rep = lambda t: jnp.broadcast_to(t[:, :, None], (RET_HEADS, chunk, LANES))
    return c_dec, intra, rep(q_dec), rep(k_dec)


def _ret_core(proj, s0, nseq, nchunk, chunk, n_real, window=None, shift=None):
    c_dec, intra, q_dec, k_dec = _ret_decay_tables(chunk, n_real)
    rows = nseq * nchunk * chunk
    row = lambda b, c: b * nchunk + c
    st_spec = pl.BlockSpec((1, RET_HEADS, RET_DK, RET_DV), lambda b, c: (b, 0, 0, 0))
    in_specs = [
        pl.BlockSpec(memory_space=pltpu.SMEM),
        pl.BlockSpec((chunk, RET_QK), lambda b, c: (row(b, c), 0)),
        pl.BlockSpec((chunk, RET_QK), lambda b, c: (row(b, c), 1)),
        pl.BlockSpec((chunk, RET_VW), lambda b, c: (row(b, c), 1)),
        pl.BlockSpec((chunk, RET_VW), lambda b, c: (row(b, c), 2)),
        st_spec,
        _resident((RET_HEADS, chunk, chunk)),
        _resident((RET_HEADS, chunk, LANES)),
        _resident((RET_HEADS, chunk, LANES)),
    ]
    args = [c_dec, proj, proj, proj, proj, s0, intra, q_dec, k_dec]
    out_shape = [jax.ShapeDtypeStruct((rows, RET_VW), BF16), jax.ShapeDtypeStruct((nseq, RET_HEADS, RET_DK, RET_DV), F32)]
    out_specs = [pl.BlockSpec((chunk, RET_VW), lambda b, c: (row(b, c), 0)), st_spec]
    scratch, kwargs = [], {}
    if window is not None:
        in_spec, shape, out_spec, scratch, chunks = _background_shift_specs(window, nseq * nchunk)
        in_specs.append(in_spec)
        args.append(window)
        out_shape.append(shape)
        out_specs.append(out_spec)
        kwargs = dict(shift=shift, chunks=chunks)
    return pl.pallas_call(
        functools.partial(_ret_core_kernel, **kwargs),
        out_shape=tuple(out_shape),
        grid=(nseq, nchunk),
        in_specs=in_specs,
        out_specs=tuple(out_specs),
        scratch_shapes=scratch,
        compiler_params=_params("arbitrary", "arbitrary"),
        name="ret_core",
    )(*args)


def _window_tail_kernel(new_ref, win_hbm, out_hbm, sem):
    b = pl.program_id(0)
    steps = new_ref.shape[1]
    cp = pltpu.make_async_copy(new_ref.at[0], out_hbm.at[b, pl.ds(out_hbm.shape[1] - steps, steps)], sem)
    cp.start()
    cp.wait()


def _window_tail(shifted, new):
    nb, steps = new.shape[:2]
    return pl.pallas_call(
        _window_tail_kernel,
        out_shape=jax.ShapeDtypeStruct(shifted.shape, shifted.dtype),
        grid=(nb,),
        in_specs=[pl.BlockSpec((1, steps) + new.shape[2:], lambda b: (b, 0, 0, 0, 0)), pl.BlockSpec(memory_space=pl.ANY)],
        out_specs=pl.BlockSpec(memory_space=pl.ANY),
        scratch_shapes=[pltpu.SemaphoreType.DMA(())],
        input_output_aliases={1: 0},
        compiler_params=_params("arbitrary"),
        name="window_tail",
    )(new, shifted)


def _with_tail_and_buckets(rows_fn, tail_ref, route_refs, o_ref, b_ref):
    last = pl.num_programs(0) - 1

    def emit(rows):
        o_ref[...] = rows
        b_ref[...] = _bucket_ids(rows, *route_refs)

    @pl.when(pl.program_id(0) < last)
    def _():
        emit(rows_fn())

    @pl.when(pl.program_id(0) == last)
    def _():
        pad = jnp.zeros((o_ref.shape[0] - tail_ref.shape[0], o_ref.shape[1]), o_ref.dtype)
        emit(jnp.concatenate([tail_ref[...], pad], axis=0))


def _route_specs():
    return [_resident((1, D_MODEL)), _resident((2, D_MODEL, LANES)), _resident((1, LANES))]


def _mm_res_kernel(a_ref, w_ref, x_ref, *rest):
    rows_fn = lambda: x_ref[...] + _dot(a_ref[...], w_ref[...])
    if len(rest) == 1:
        rest[0][...] = rows_fn()
    else:
        tail_ref, *route_refs, o_ref, b_ref = rest
        _with_tail_and_buckets(rows_fn, tail_ref, route_refs, o_ref, b_ref)


def _mm_res(a, w, x, x_off, tm, tail=None, route=None):
    m, k = a.shape
    nblk = m // tm
    clamp = (lambda i: jnp.minimum(i, nblk - 1)) if tail is not None else (lambda i: i)
    in_specs = [
        pl.BlockSpec((tm, k), lambda i: (clamp(i), 0)),
        _resident((k, D_MODEL)),
        pl.BlockSpec((tm, D_MODEL), lambda i: (clamp(i) + x_off, 0)),
    ]
    args = [a, w, x]
    steps = nblk
    out_shape = jax.ShapeDtypeStruct((nblk * tm, D_MODEL), F32)
    out_specs = pl.BlockSpec((tm, D_MODEL), lambda i: (i, 0))
    if tail is not None:
        steps = nblk + 1
        in_specs += [_resident(tail.shape)] + _route_specs()
        args += [tail, *route]
        out_shape = (jax.ShapeDtypeStruct((steps * tm, D_MODEL), F32), jax.ShapeDtypeStruct((steps * tm, LANES), jnp.int32))
        out_specs = (out_specs, pl.BlockSpec((tm, LANES), lambda i: (i, 0)))
    return pl.pallas_call(
        _mm_res_kernel,
        out_shape=out_shape,
        grid=(steps,),
        in_specs=in_specs,
        out_specs=out_specs,
        compiler_params=_params("arbitrary"),
        name="mm_res",
    )(*args)


ATT_PROJ_CHUNK = 4 * ATT_HD


def _att_proj_chunk(xn, w_ref, qg_ref, kg_ref, c, width):
    acc = _dot(xn, w_ref[:, c * ATT_PROJ_CHUNK:(c + 1) * ATT_PROJ_CHUNK])
    n_qk = 2 * (width // 3) // ATT_PROJ_CHUNK
    if c >= n_qk:
        return acc
    gain = qg_ref[...] if c < n_qk // 2 else kg_ref[...]
    heads = []
    for hh in range(ATT_PROJ_CHUNK // ATT_HD):
        a = acc[:, hh * ATT_HD:(hh + 1) * ATT_HD]
        heads.append(a * lax.rsqrt(jnp.mean(a * a, axis=-1, keepdims=True) + EPS) * gain)
    return jnp.concatenate(heads, axis=1)


def _att_proj_kernel(x_ref, g_ref, w_ref, qg_ref, kg_ref, o_ref):
    xn = _rmsnorm_rows(x_ref[...], g_ref[...]).astype(BF16)
    width = o_ref.shape[1]
    for c in range(width // ATT_PROJ_CHUNK):
        sl = slice(c * ATT_PROJ_CHUNK, (c + 1) * ATT_PROJ_CHUNK)
        o_ref[:, sl] = _att_proj_chunk(xn, w_ref, qg_ref, kg_ref, c, width).astype(o_ref.dtype)


def _att_proj_dil_kernel(x_hbm, g_ref, w_ref, qg_ref, kg_ref, o_ref, xbuf, sem, *, picks, rows):
    i = pl.program_id(0)
    slot = i % 2

    def fetch(step, sl, action):
        k = 0
        for mids in picks:
            for s in mids:
                cp = pltpu.make_async_copy(x_hbm.at[pl.ds(step * rows, rows), s], xbuf.at[sl, pl.ds(k * rows, rows)],
                                           sem.at[sl])
                getattr(cp, action)()
                k += 1

    @pl.when(i == 0)
    def _():
        fetch(0, 0, "start")

    @pl.when(i + 1 < pl.num_programs(0))
    def _():
        fetch(i + 1, 1 - slot, "start")

    fetch(i, slot, "wait")
    per = rows * len(picks[0])
    width = o_ref.shape[3]
    xn = _rmsnorm_rows(xbuf[slot], g_ref[...]).astype(BF16)
    for c in range(width // ATT_PROJ_CHUNK):
        sl = slice(c * ATT_PROJ_CHUNK, (c + 1) * ATT_PROJ_CHUNK)
        val = _att_proj_chunk(xn, w_ref, qg_ref, kg_ref, c, width).astype(o_ref.dtype)
        for r in range(len(picks)):
            o_ref[0, r, :, sl] = val[r * per:(r + 1) * per]


def _residue_picks(dil):
    if dil % 8 == 0:
        return dil, [[r] for r in range(dil)]
    assert 8 % dil == 0 and dil > 1
    return 8, [[r + dil * e for e in range(8 // dil)] for r in range(dil)]


def _att_proj_dil(x, nb, length, dil, gain, w, q_gain, k_gain):
    sub, picks = _residue_picks(dil)
    width = w.shape[1]
    rows = ATT_SPAN // 2
    per = rows * len(picks[0])
    ls = length // dil
    steps_per_seq = ls // per
    return pl.pallas_call(
        functools.partial(_att_proj_dil_kernel, picks=picks, rows=rows),
        out_shape=jax.ShapeDtypeStruct((nb, dil, ls, width), BF16),
        grid=(nb * steps_per_seq,),
        in_specs=[
            pl.BlockSpec(memory_space=pl.ANY),
            _resident((1, D_MODEL)),
            _resident((D_MODEL, width)),
            _resident((1, ATT_HD)),
            _resident((1, ATT_HD)),
        ],
        out_specs=pl.BlockSpec((1, dil, per, width), lambda i: (i // steps_per_seq, 0, i % steps_per_seq, 0)),
        scratch_shapes=[pltpu.VMEM((2, dil * per, D_MODEL), F32), pltpu.SemaphoreType.DMA((2,))],
        compiler_params=_params("arbitrary"),
        name="att_proj_dil",
    )(x.reshape(x.shape[0] // sub, sub, D_MODEL), gain, w, q_gain, k_gain)


def _att_proj(x, row_off, rows, tm, gain, w, q_gain, k_gain, out_dtype):
    width = w.shape[1]
    return pl.pallas_call(
        _att_proj_kernel,
        out_shape=jax.ShapeDtypeStruct((rows, width), out_dtype),
        grid=(rows // tm,),
        in_specs=[
            pl.BlockSpec((tm, D_MODEL), lambda i: (i + row_off, 0)),
            _resident((1, D_MODEL)),
            _resident((D_MODEL, width)),
            _resident((1, ATT_HD)),
            _resident((1, ATT_HD)),
        ],
        out_specs=pl.BlockSpec((tm, width), lambda i: (i, 0)),
        compiler_params=_params("parallel"),
        name="att_proj",
    )(x, gain, w, q_gain, k_gain)


def _att_band_kernel(q_ref, kp_ref, kc_ref, vp_ref, vc_ref, bias_ref, o_ref, *scratch, picks, blocks):
    n = pl.program_id(1)
    res = q_ref.shape[1]
    col = lax.broadcasted_iota(jnp.int32, (ATT_SPAN, 2 * ATT_SPAN), 1)
    no_prev = jnp.logical_and(col < ATT_SPAN, n == 0)
    lane = lax.broadcasted_iota(jnp.int32, (ATT_SPAN, LANES), 1)

    def unit(rr, j):
        lse_tile = jnp.zeros((ATT_SPAN, LANES), F32)
        heads = []
        for h in range(ATT_G):
            sl = slice(h * ATT_HD, (h + 1) * ATT_HD)
            if j == 0:
                kcat = jnp.concatenate([kp_ref[0, rr, :, sl], kc_ref[0, rr, 0:ATT_SPAN, sl]], axis=0)
                vcat = jnp.concatenate([vp_ref[0, rr, :, sl], vc_ref[0, rr, 0:ATT_SPAN, sl]], axis=0)
            else:
                kcat = kc_ref[0, rr, (j - 1) * ATT_SPAN:(j + 1) * ATT_SPAN, sl]
                vcat = vc_ref[0, rr, (j - 1) * ATT_SPAN:(j + 1) * ATT_SPAN, sl]
            s = _dot_nt(q_ref[0, rr, j * ATT_SPAN:(j + 1) * ATT_SPAN, sl], kcat) + bias_ref[h]
            if j == 0:
                s = jnp.where(no_prev, NEG, s)
            m = jnp.max(s, axis=-1, keepdims=True)
            p = jnp.exp(s - m)
            den = jnp.sum(p, axis=-1, keepdims=True)
            heads.append(_dot(p.astype(BF16), vcat) / den)
            lse_tile = jnp.where(lane == h, m + jnp.log(den), lse_tile)
        return jnp.concatenate(heads + [lse_tile], axis=1)

    if picks is None:
        for j in range(blocks):
            o_ref[0, j * ATT_SPAN:(j + 1) * ATT_SPAN, :] = unit(0, j)
        return

    obuf, sem = scratch
    b, g = pl.program_id(0), pl.program_id(2)
    steps = pl.num_programs(0) * pl.num_programs(1) * pl.num_programs(2)
    flat = (b * pl.num_programs(1) + n) * pl.num_programs(2) + g
    slot = flat % 2
    per = ATT_SPAN // len(picks[0])

    def writeback(sl, action):
        for rr in range(res):
            for e, mid in enumerate(picks[0]):
                cp = pltpu.make_async_copy(obuf.at[sl, rr, pl.ds(e * per, per)],
                                           o_ref.at[b, pl.ds(n * per, per), g * res + rr + mid], sem.at[sl])
                getattr(cp, action)()

    @pl.when(flat >= 2)
    def _():
        writeback(slot, "wait")

    for rr in range(res):
        obuf[slot, rr] = unit(rr, 0)
    writeback(slot, "start")

    @pl.when(flat == steps - 1)
    def _():
        writeback(slot, "wait")

        @pl.when(steps >= 2)
        def _():
            writeback(1 - slot, "wait")


def _att_band(qkv, dil, bias):
    nb, _, ls, _ = qkv.shape
    units = ATT_BAND_UNITS
    if dil == 1:
        picks, res, blocks = None, 1, units
        out_shape = (nb, ls, OUT_LSE)
        out_spec = pl.BlockSpec((1, blocks * ATT_SPAN, OUT_LSE), lambda b, n, g: (b, n, 0))
        scratch = []
    else:
        sub, picks = _residue_picks(dil)
        res, blocks = units, 1
        out_shape = (nb, ls * dil // sub, sub, OUT_LSE)
        out_spec = pl.BlockSpec(memory_space=pl.ANY)
        scratch = [pltpu.VMEM((2, res, ATT_SPAN, OUT_LSE), F32), pltpu.SemaphoreType.DMA((2,))]
    cur = lambda cb: pl.BlockSpec((1, res, blocks * ATT_SPAN, ATT_W), lambda b, n, g: (b, g, n, cb))
    prev = lambda cb: pl.BlockSpec((1, res, ATT_SPAN, ATT_W), lambda b, n, g: (b, g, jnp.maximum(n * blocks - 1, 0), cb))
    out = pl.pallas_call(
        functools.partial(_att_band_kernel, picks=picks, blocks=blocks),
        out_shape=jax.ShapeDtypeStruct(out_shape, F32),
        grid=(nb, ls // (blocks * ATT_SPAN), dil // res),
        in_specs=[cur(0), prev(1), cur(1), prev(2), cur(2), _resident((ATT_G, ATT_SPAN, 2 * ATT_SPAN))],
        out_specs=out_spec,
        scratch_shapes=scratch,
        compiler_params=_params("arbitrary", "arbitrary", "arbitrary"),
        name="att_band",
    )(qkv, qkv, qkv, qkv, qkv, bias)
    return out.reshape(nb * ls * dil, OUT_LSE)


def _window_rows_kernel(k_ref, v_ref, o_ref):
    halves = o_ref.shape[2]
    per = o_ref.shape[1]
    for e in range(halves):
        for kv, ref in enumerate((k_ref, v_ref)):
            for h in range(ATT_G):
                o_ref[0, :, e, 0, kv, h, :] = ref[0, 0, e * per:(e + 1) * per, h * ATT_HD:(h + 1) * ATT_HD].astype(o_ref.dtype)


def _window_rows(qkv, dil, dtype):
    nb, _, ls, _ = qkv.shape
    halves = 1 if dil == 1 else len(_residue_picks(dil)[1][0])
    per = ATT_SPAN // halves
    last = ls // ATT_SPAN - 1
    out = pl.pallas_call(
        _window_rows_kernel,
        out_shape=jax.ShapeDtypeStruct((nb, per, halves, dil, 2, ATT_G, ATT_HD), dtype),
        grid=(nb, dil),
        in_specs=[pl.BlockSpec((1, 1, ATT_SPAN, ATT_W), lambda b, r: (b, r, last, 1)),
                  pl.BlockSpec((1, 1, ATT_SPAN, ATT_W), lambda b, r: (b, r, last, 2))],
        out_specs=pl.BlockSpec((1, per, halves, 1, 2, ATT_G, ATT_HD), lambda b, r: (b, 0, 0, r, 0, 0, 0)),
        compiler_params=_params("parallel", "parallel"),
        name="window_rows",
    )(qkv, qkv)
    return out.reshape(nb, ATT_SPAN * dil, 2, ATT_G, ATT_HD)


def _att_merge_kernel(o0_ref, o1_ref, o2_ref, w_ref, x_ref, tail_ref, *rest):
    *route_refs, out_ref, b_ref = rest

    def rows_fn():
        acc = x_ref[...]
        refs = (o0_ref, o1_ref, o2_ref)
        for h in range(ATT_G):
            sl = slice(h * ATT_HD, (h + 1) * ATT_HD)
            lses = [r[:, ATT_W + h:ATT_W + h + 1] for r in refs]
            m = jnp.maximum(jnp.maximum(lses[0], lses[1]), lses[2])
            es = [jnp.exp(l - m) for l in lses]
            merged = (es[0] * o0_ref[:, sl] + es[1] * o1_ref[:, sl] + es[2] * o2_ref[:, sl]) / (es[0] + es[1] + es[2])
            acc = acc + _dot(merged.astype(BF16), w_ref[sl, :])
        return acc

    _with_tail_and_buckets(rows_fn, tail_ref, route_refs, out_ref, b_ref)


def _att_merge(outs, w, x, rows, tm, tail, route):
    nblk = rows // tm
    row_spec = lambda width: pl.BlockSpec((tm, width), lambda i: (jnp.minimum(i, nblk - 1), 0))
    return pl.pallas_call(
        _att_merge_kernel,
        out_shape=(jax.ShapeDtypeStruct(((nblk + 1) * tm, D_MODEL), F32),
                   jax.ShapeDtypeStruct(((nblk + 1) * tm, LANES), jnp.int32)),
        grid=(nblk + 1,),
        in_specs=[row_spec(OUT_LSE), row_spec(OUT_LSE), row_spec(OUT_LSE), _resident((ATT_W, D_MODEL)),
                  row_spec(D_MODEL), _resident(tail.shape)] + _route_specs(),
        out_specs=(pl.BlockSpec((tm, D_MODEL), lambda i: (i, 0)), pl.BlockSpec((tm, LANES), lambda i: (i, 0))),
        compiler_params=_params("arbitrary"),
        name="att_merge",
    )(*outs, w, x, tail, *route)


def _t5_bucket(dist):
    max_exact = N_BUCKETS // 2
    d32 = jnp.maximum(dist, 1).astype(F32)
    large = max_exact + (jnp.log(d32 / max_exact) / jnp.log(MAX_DISTANCE / max_exact)
                         * (N_BUCKETS - max_exact)).astype(jnp.int32)
    return jnp.where(dist < max_exact, dist, jnp.minimum(large, N_BUCKETS - 1))


def _bucket_rows(tab, dist):
    onehot = jax.nn.one_hot(_t5_bucket(dist), N_BUCKETS, dtype=F32)
    return jnp.dot(onehot, tab, precision=lax.Precision.HIGHEST)


def _band_bias(tab, dil, halves):
    pos = jnp.arange(ATT_SPAN).reshape(ATT_SPAN // halves, halves).T.reshape(-1)
    a = pos[:, None]
    c = jnp.concatenate([pos, ATT_SPAN + pos])[None, :]
    rel = a - c + ATT_SPAN
    bias = jnp.moveaxis(_bucket_rows(tab, jnp.clip(rel, 0, ATT_SPAN) * dil), -1, 0)
    return jnp.where(((rel >= 0) & (rel <= ATT_SPAN))[None], bias, NEG)


def _att_sample_kernel(qkv_ref, c0_ref, c1_ref, c2_ref, b0_ref, b1_ref, b2_ref, o_ref, *, steps):
    for t in range(steps):
        outs, lses = [], []
        for g, (c_ref, b_ref) in enumerate(((c0_ref, b0_ref), (c1_ref, b1_ref), (c2_ref, b2_ref))):
            dil = ATT_GROUPS[g][1]
            q = qkv_ref[0, t, 0, g]
            if dil == 1:
                keys = jnp.concatenate([c_ref[0, t:, 0, 0], qkv_ref[0, :t + 1, 1, g]], axis=0)
                vals = jnp.concatenate([c_ref[0, t:, 0, 1], qkv_ref[0, :t + 1, 2, g]], axis=0)
            else:
                keys = jnp.concatenate([c_ref[0, :, t, 0], qkv_ref[0, t:t + 1, 1, g]], axis=0)
                vals = jnp.concatenate([c_ref[0, :, t, 1], qkv_ref[0, t:t + 1, 2, g]], axis=0)
            s = jnp.sum(keys * q[None], axis=-1, keepdims=True) + b_ref[...]
            m = jnp.max(s, axis=0)
            p = jnp.exp(s - m[None])
            den = jnp.sum(p, axis=0)
            outs.append(jnp.sum(p * vals, axis=0) / den)
            lses.append(m + jnp.log(den))
        m = jnp.maximum(jnp.maximum(lses[0], lses[1]), lses[2])
        es = [jnp.exp(l - m) for l in lses]
        o_ref[0, t] = (es[0] * outs[0] + es[1] * outs[1] + es[2] * outs[2]) / (es[0] + es[1] + es[2])


def _att_sample(qkv, caches, biases):
    nb, steps = qkv.shape[:2]
    assert steps <= min(d for _, d in ATT_GROUPS[1:]) and all(c.shape[1] == w for c, (w, _) in zip(caches, ATT_GROUPS))
    views, view_specs = [], []
    for c, (win, dil) in zip(caches, ATT_GROUPS):
        views.append(c.reshape(nb, ATT_SPAN, dil, 2, ATT_G, ATT_HD))
        view_specs.append(pl.BlockSpec((1, ATT_SPAN, min(dil, steps), 2, ATT_G, ATT_HD), lambda b: (b, 0, 0, 0, 0, 0)))
    return pl.pallas_call(
        functools.partial(_att_sample_kernel, steps=steps),
        out_shape=jax.ShapeDtypeStruct((nb, steps, ATT_G, ATT_HD), F32),
        grid=(nb,),
        in_specs=[pl.BlockSpec((1, steps, 3, ATT_NG, ATT_G, ATT_HD), lambda b: (b, 0, 0, 0, 0, 0)),
                  *view_specs,
                  *[_resident((ATT_SPAN + 1, ATT_G, LANES))] * ATT_NG],
        out_specs=pl.BlockSpec((1, steps, ATT_G, ATT_HD), lambda b: (b, 0, 0, 0)),
        compiler_params=_params("parallel"),
        name="att_sample",
    )(qkv, *views, *biases)


def _window_shift_kernel(new_ref, old_hbm, out_hbm, buf, rsem, wsem, *, steps, chunks):
    s = pl.program_id(0)
    total = pl.num_programs(0)
    depth, rows = buf.shape[:2]

    def read(step, action):
        b, c, sl = step // chunks, step % chunks, step % depth

        @pl.when(c < chunks - 1)
        def _():
            getattr(pltpu.make_async_copy(old_hbm.at[b, pl.ds(c * rows + steps, rows)], buf.at[sl], rsem.at[sl]), action)()

        @pl.when(c == chunks - 1)
        def _():
            getattr(pltpu.make_async_copy(old_hbm.at[b, pl.ds(c * rows + steps, rows - steps)],
                                          buf.at[sl, pl.ds(0, rows - steps)], rsem.at[sl]), action)()

    def write(step, action):
        b, c, sl = step // chunks, step % chunks, step % depth
        getattr(pltpu.make_async_copy(buf.at[sl], out_hbm.at[b, pl.ds(c * rows, rows)], wsem.at[sl]), action)()

    @pl.when(s == 0)
    def _():
        for ahead in range(depth - 1):
            @pl.when(ahead < total)
            def _():
                read(ahead, "start")

    @pl.when(s >= 1)
    def _():
        write(s - 1, "wait")

    @pl.when(s + depth - 1 < total)
    def _():
        read(s + depth - 1, "start")

    read(s, "wait")

    @pl.when(s % chunks == chunks - 1)
    def _():
        buf[s % depth, rows - steps:rows] = new_ref[0]

    write(s, "start")

    @pl.when(s == total - 1)
    def _():
        write(s, "wait")


def _window_shift(old, new):
    nb, win = old.shape[:2]
    steps = new.shape[1]
    rows = min(win, WINDOW_SHIFT_ROWS)
    chunks = win // rows
    tail = old.shape[2:]
    any_spec = pl.BlockSpec(memory_space=pl.ANY)
    return pl.pallas_call(
        functools.partial(_window_shift_kernel, steps=steps, chunks=chunks),
        out_shape=jax.ShapeDtypeStruct(old.shape, old.dtype),
        grid=(nb * chunks,),
        in_specs=[pl.BlockSpec((1, steps) + tail, lambda s: (s // chunks, 0, 0, 0, 0)), any_spec],
        out_specs=any_spec,
        scratch_shapes=[pltpu.VMEM((WINDOW_SHIFT_DEPTH, rows) + tail, old.dtype),
                        pltpu.SemaphoreType.DMA((WINDOW_SHIFT_DEPTH,)), pltpu.SemaphoreType.DMA((WINDOW_SHIFT_DEPTH,))],
        compiler_params=_params("arbitrary"),
        name="window_shift",
    )(new, old)


def _router_logits(xn, wr_ref, br_ref):
    xh = xn.astype(BF16)
    xl = (xn - xh.astype(F32)).astype(BF16)
    return _dot(xh, wr_ref[0]) + (_dot(xh, wr_ref[1]) + _dot(xl, wr_ref[0])) + br_ref[...]


def _route(logits, grp):
    lane = lax.broadcasted_iota(jnp.int32, logits.shape, 1)
    is_group = lane < MOE_GROUPS
    mx = jnp.max(jnp.where(is_group, logits, NEG), axis=-1, keepdims=True)
    if grp is None:
        grp = jnp.min(jnp.where(jnp.logical_and(is_group, logits == mx), lane, LANES), axis=-1, keepdims=True)
    den = jnp.sum(jnp.where(is_group, jnp.exp(logits - mx), 0.0), axis=-1, keepdims=True)
    sel = jnp.sum(jnp.where(lane == grp, logits, 0.0), axis=-1, keepdims=True)
    p_group = jnp.exp(sel - mx) / den
    lo = MOE_GROUPS + MOE_PER_GROUP * grp
    le = jnp.where(jnp.logical_and(lane >= lo, lane < lo + MOE_PER_GROUP), logits, NEG)
    v1 = jnp.max(le, axis=-1, keepdims=True)
    i1 = jnp.min(jnp.where(le == v1, lane, LANES), axis=-1, keepdims=True)
    le = jnp.where(lane == i1, NEG, le)
    v2 = jnp.max(le, axis=-1, keepdims=True)
    i2 = jnp.min(jnp.where(le == v2, lane, LANES), axis=-1, keepdims=True)
    e2 = jnp.exp(v2 - v1)
    return grp, i1, i2, p_group / (1.0 + e2), p_group * e2 / (1.0 + e2)


MOE_PAIRS = MOE_PER_GROUP * (MOE_PER_GROUP - 1) // 2
MOE_BUCKETS = MOE_GROUPS * MOE_PAIRS


def _bucket_ids(x, g_ref, wr_ref, br_ref):
    logits = _router_logits(_rmsnorm_rows(x, g_ref[...]), wr_ref, br_ref)
    grp, i1, i2, _, _ = _route(logits, None)
    lo = MOE_GROUPS + MOE_PER_GROUP * grp
    a = jnp.minimum(i1, i2) - lo
    b = jnp.maximum(i1, i2) - lo
    pair = ((a * (2 * MOE_PER_GROUP - 1 - a)) >> 1) + b - a - 1
    return jnp.broadcast_to(grp * MOE_PAIRS + pair, logits.shape)


def _moe_dispatch_kernel(slot_ref, zt_ref, x_ref, xs_hbm, zero_ref, sem, zsem, *, n_rows, tile):
    i = pl.program_id(0)
    tm = x_ref.shape[0]
    count = jnp.minimum(tm, n_rows - i * tm)
    n_buckets = zt_ref.shape[0] - 2

    @pl.when(i == 0)
    def _():
        zero_ref[...] = jnp.zeros(zero_ref.shape, zero_ref.dtype)
        parts = tile // zero_ref.shape[0]

        def zero_tile(t, action):
            for part in range(parts):
                row = pl.multiple_of(t * tile + part * zero_ref.shape[0], zero_ref.shape[0])
                getattr(pltpu.make_async_copy(zero_ref, xs_hbm.at[pl.ds(row, zero_ref.shape[0])], zsem), action)()

        for action in ("start", "wait"):
            for b in range(n_buckets):
                @pl.when(zt_ref[b] >= 0)
                def _():
                    zero_tile(zt_ref[b], action)

            def unused(t, carry):
                zero_tile(t, action)
                return carry
            lax.fori_loop(zt_ref[n_buckets], zt_ref[n_buckets + 1], unused, 0)

    def start(k, carry):
        pltpu.make_async_copy(x_ref.at[pl.ds(k, 1)], xs_hbm.at[pl.ds(slot_ref[i * tm + k], 1)], sem).start()
        return carry

    def wait(k, carry):
        pltpu.make_async_copy(x_ref.at[pl.ds(0, 1)], xs_hbm.at[pl.ds(0, 1)], sem).wait()
        return carry

    @pl.when(count == tm)
    def _():
        for k in range(tm):
            start(k, 0)
        pltpu.make_async_copy(x_ref, xs_hbm.at[pl.ds(0, tm)], sem).wait()

    @pl.when(count < tm)
    def _():
        lax.fori_loop(0, count, start, 0)
        lax.fori_loop(0, count, wait, 0)


def _moe_dispatch(x, slot_of_row, zero_tiles, n_rows, n_slots, tm, tile):
    return pl.pallas_call(
        functools.partial(_moe_dispatch_kernel, n_rows=n_rows, tile=tile),
        out_shape=jax.ShapeDtypeStruct((n_slots, D_MODEL), F32),
        grid_spec=pltpu.PrefetchScalarGridSpec(
            num_scalar_prefetch=2,
            grid=(pl.cdiv(n_rows, tm),),
            in_specs=[pl.BlockSpec((tm, D_MODEL), lambda i, s, p: (i, 0))],
            out_specs=pl.BlockSpec(memory_space=pl.ANY),
            scratch_shapes=[pltpu.VMEM((ZERO_ROWS, D_MODEL), F32), pltpu.SemaphoreType.DMA(()),
                            pltpu.SemaphoreType.DMA(())]),
        compiler_params=_params("arbitrary"),
        name="moe_dispatch",
    )(slot_of_row, zero_tiles, x)


def _moe_collect_kernel(slot_ref, ys_hbm, o_ref, sem, *, row_off):
    i = pl.program_id(0)
    tm = o_ref.shape[0]

    for k in range(tm):
        pltpu.make_async_copy(ys_hbm.at[pl.ds(slot_ref[row_off + i * tm + k], 1)], o_ref.at[pl.ds(k, 1)], sem).start()
    pltpu.make_async_copy(ys_hbm.at[pl.ds(0, tm)], o_ref, sem).wait()


def _moe_collect(ys, slot_of_row, row_off, rows, tm):
    return pl.pallas_call(
        functools.partial(_moe_collect_kernel, row_off=row_off),
        out_shape=jax.ShapeDtypeStruct((rows, D_MODEL), F32),
        grid_spec=pltpu.PrefetchScalarGridSpec(
            num_scalar_prefetch=1,
            grid=(rows // tm,),
            in_specs=[pl.BlockSpec(memory_space=pl.ANY)],
            out_specs=pl.BlockSpec((tm, D_MODEL), lambda i, s: (i, 0)),
            scratch_shapes=[pltpu.SemaphoreType.DMA(())]),
        compiler_params=_params("arbitrary"),
        name="moe_collect",
    )(slot_of_row, ys)


def _moe_expert_kernel(ea_ref, eb_ref, used_ref, x_ref, gain_ref, wr_ref, br_ref,
                       wga_ref, wua_ref, wda_ref, wgb_ref, wub_ref, wdb_ref, y_ref):
    t = pl.program_id(0)

    @pl.when(t >= used_ref[0])
    def _():
        y_ref[...] = jnp.zeros(y_ref.shape, y_ref.dtype)

    @pl.when(t < used_ref[0])
    def _():
        x = x_ref[...]
        xn = _rmsnorm_rows(x, gain_ref[...])
        _, i1, i2, w1, w2 = _route(_router_logits(xn, wr_ref, br_ref), ea_ref[t] // MOE_PER_GROUP)
        xb = xn.astype(BF16)
        acc = x
        for e_ref, wg_ref, wu_ref, wd_ref in ((ea_ref, wga_ref, wua_ref, wda_ref), (eb_ref, wgb_ref, wub_ref, wdb_ref)):
            e_lane = MOE_GROUPS + e_ref[t]
            ce = jnp.where(i1 == e_lane, w1, 0.0) + jnp.where(i2 == e_lane, w2, 0.0)
            hg = _dot(xb, wg_ref[0, 0].astype(BF16))
            hid = hg * jax.nn.sigmoid(hg) * _dot(xb, wu_ref[0, 0].astype(BF16))
            acc = acc + ce * _dot(hid.astype(BF16), wd_ref[0, 0].astype(BF16))
        y_ref[...] = acc


def _moe_plan(bucket, tile, n_tiles):
    onehot = (bucket[:, None] == jnp.arange(MOE_BUCKETS)[None, :]).astype(jnp.int32)
    csum = jnp.cumsum(onehot, axis=0)
    counts = csum[-1]
    tiles_per = (counts + tile - 1) // tile
    tile_end = jnp.cumsum(tiles_per)
    tile_start = tile_end - tiles_per
    slot_of_row = jnp.sum(onehot * (tile_start[None, :] * tile + csum - 1), axis=1)
    tile_bucket = jnp.minimum(jnp.sum(jnp.arange(n_tiles)[:, None] >= tile_end[None, :], axis=1), MOE_BUCKETS - 1)
    pair = tile_bucket % MOE_PAIRS
    first = jnp.sum(pair[:, None] >= jnp.cumsum(jnp.arange(MOE_PER_GROUP - 1, 0, -1))[None, :], axis=1)
    second = pair - ((first * (2 * MOE_PER_GROUP - 1 - first)) >> 1) + first + 1
    base = (tile_bucket // MOE_PAIRS) * MOE_PER_GROUP
    used = tile_end[-1:]
    zero_tiles = jnp.concatenate([jnp.where(tiles_per > 0, tile_end - 1, -1), used, jnp.full((1,), n_tiles)])
    i32 = lambda v: v.astype(jnp.int32)
    return i32(slot_of_row), i32(base + first), i32(base + second), i32(used), i32(zero_tiles)


def _moe(x, buckets, n_prompt, n_sample, layer, gain, wr_pair, br, wg, wu, wd):
    n = n_prompt + n_sample
    tile = MOE_TILE
    n_tiles = n // tile + MOE_BUCKETS
    slot_of_row, expert_a, expert_b, n_used, zero_tiles = _moe_plan(buckets[:n, 0], tile, n_tiles)
    xs = _moe_dispatch(x, slot_of_row, zero_tiles, n, n_tiles * tile, ROW_TILE, tile)
    w_spec = lambda shape, which: pl.BlockSpec((1, 1) + shape, lambda t, ea, eb, u: (layer, (ea, eb)[which][t], 0, 0))
    const = lambda shape: pl.BlockSpec(shape, lambda t, ea, eb, u: (0,) * len(shape), pipeline_mode=pl.Buffered(1))
    w_specs = [w_spec(shape, which) for which in range(2)
               for shape in ((D_MODEL, MOE_HIDDEN), (D_MODEL, MOE_HIDDEN), (MOE_HIDDEN, D_MODEL))]
    ys = pl.pallas_call(
        _moe_expert_kernel,
        out_shape=jax.ShapeDtypeStruct((n_tiles * tile, D_MODEL), F32),
        grid_spec=pltpu.PrefetchScalarGridSpec(
            num_scalar_prefetch=3,
            grid=(n_tiles,),
            in_specs=[pl.BlockSpec((tile, D_MODEL), lambda t, ea, eb, u: (jnp.minimum(t, u[0] - 1), 0)),
                      const((1, D_MODEL)), const((2, D_MODEL, LANES)), const((1, LANES)), *w_specs],
            out_specs=pl.BlockSpec((tile, D_MODEL), lambda t, ea, eb, u: (t, 0))),
        compiler_params=_params("arbitrary"),
        name="moe_experts",
    )(expert_a, expert_b, n_used, xs, gain, wr_pair, br, wg, wu, wd, wg, wu, wd)
    return (_moe_collect(ys, slot_of_row, 0, n_prompt, ROW_TILE),
            _moe_collect(ys, slot_of_row, n_prompt, n_sample, n_sample))


def _rope_tables(pos):
    half = RET_DK // 2
    inv = 1.0 / (ROPE_BASE ** jnp.linspace(0.0, 1.0, half, dtype=F32))
    ang = jnp.repeat(pos.astype(F32)[:, None] * inv[None, :], 2, axis=-1)
    sign = jnp.where(jnp.arange(RET_DK) % 2 == 0, -1.0, 1.0).astype(F32)
    return jnp.cos(ang), jnp.sin(ang) * sign


def kernel(x_prompt, x_sample, state_ret, cache_win0, cache_win1, cache_win2, rel_bias, norm_mix, norm_ffn, w_in_ret, w_out_ret, w_in_att, q_norm, k_norm, w_out_att, w_router_group, b_router_group, w_router_expert, b_router_expert, w_gate, w_up, w_down):
    nb_p, len_p, _ = x_prompt.shape
    nb_s, len_s, _ = x_sample.shape
    n_p, n_s = nb_p * len_p, nb_s * len_s
    n_all = n_p + n_s
    assert n_p % ROW_TILE == 0 and n_p % n_s == 0 and len_p % RET_CHUNK == 0 and len_s <= SAMPLE_PAD
    assert all(len_p % (ATT_SPAN * d) == 0 for _, d in ATT_GROUPS)
    caches = (cache_win0[0], cache_win1[0], cache_win2[0])

    xp = x_prompt.reshape(n_p, D_MODEL)
    xs = x_sample.reshape(n_s, D_MODEL)
    row = lambda v: v.reshape(1, -1).astype(F32)

    def moe_weights(i):
        wr = jnp.zeros((D_MODEL, LANES), F32)
        wr = wr.at[:, :MOE_GROUPS].set(w_router_group[i]).at[:, MOE_GROUPS:MOE_GROUPS + MOE_EXPERTS].set(w_router_expert[i])
        br = jnp.zeros((1, LANES), F32)
        br = br.at[0, :MOE_GROUPS].set(b_router_group[i]).at[0, MOE_GROUPS:MOE_GROUPS + MOE_EXPERTS].set(b_router_expert[i])
        wr_hi = wr.astype(BF16)
        wr_pair = jnp.stack([wr_hi, (wr - wr_hi.astype(F32)).astype(BF16)])
        return (row(norm_ffn[i]), wr_pair, br, w_gate, w_up, w_down)

    gain0 = row(norm_mix[0])
    w_in = w_in_ret[0].astype(BF16)
    w_out = w_out_ret[0].astype(BF16)
    cos_p, sin_p = _rope_tables(jnp.arange(len_p))
    pos_s = jnp.tile(PAST_LEN + jnp.arange(SAMPLE_PAD), nb_s)
    cos_s, sin_s = _rope_tables(pos_s)
    xs_pad = jnp.pad(x_sample, ((0, 0), (0, SAMPLE_PAD - len_s), (0, 0))).reshape(nb_s * SAMPLE_PAD, D_MODEL)

    by_size = sorted(range(ATT_NG), key=lambda g: -caches[g].shape[1])
    moved = {}
    zero_state = jnp.zeros((nb_p, RET_HEADS, RET_DK, RET_DV), F32)
    g_proj, g_core = by_size[1], by_size[0]
    if _can_shift_behind(caches[g_proj], n_p // ROW_TILE):
        proj_p, moved[g_proj] = _ret_proj(xp, 0, n_p, ROW_TILE, gain0, w_in, cos_p, sin_p, caches[g_proj], len_s)
    else:
        proj_p = _ret_proj(xp, 0, n_p, ROW_TILE, gain0, w_in, cos_p, sin_p)
    proj_s = _ret_proj(xs_pad, 0, nb_s * SAMPLE_PAD, nb_s * SAMPLE_PAD, gain0, w_in, cos_s, sin_s)
    if _can_shift_behind(caches[g_core], nb_p * (len_p // RET_CHUNK)):
        o_p, ret_p, moved[g_core] = _ret_core(proj_p, zero_state, nb_p, len_p // RET_CHUNK, RET_CHUNK, RET_CHUNK,
                                              window=caches[g_core], shift=len_s)
    else:
        o_p, ret_p = _ret_core(proj_p, zero_state, nb_p, len_p // RET_CHUNK, RET_CHUNK, RET_CHUNK)
    o_s, ret_s = _ret_core(proj_s, state_ret[0].astype(F32), nb_s, 1, SAMPLE_PAD, len_s)
    o_s = o_s.reshape(nb_s, SAMPLE_PAD, RET_VW)[:, :len_s].reshape(n_s, RET_VW)
    moe0 = moe_weights(0)
    x1, buckets = _mm_res(o_p, w_out, xp, 0, ROW_TILE, tail=_mm_res(o_s, w_out, xs, 0, n_s), route=moe0[:3])
    x2, x2_s = _moe(x1, buckets, n_p, n_s, 0, *moe0)

    gain1 = row(norm_mix[1])
    w_in = w_in_att[0].astype(BF16)
    w_out = w_out_att[0].astype(BF16)
    q_gain = row(q_norm[0]) * (ATT_HD ** -0.5)
    k_gain = row(k_norm[0])
    qkv_s = _att_proj(x2_s, 0, n_s, n_s, gain1, w_in, q_gain, k_gain, F32)
    qkv_s = qkv_s.reshape(nb_s, len_s, 3, ATT_NG, ATT_G, ATT_HD)

    outs, win_p, win_s, sample_bias = [], [], [], []
    for g, (win, dil) in enumerate(ATT_GROUPS):
        tab = rel_bias[:, g * ATT_G:(g + 1) * ATT_G].astype(F32)
        w_g = jnp.concatenate([w_in[:, (i * ATT_NG + g) * ATT_W:(i * ATT_NG + g + 1) * ATT_W] for i in range(3)], axis=1)
        if dil == 1:
            halves = 1
            qkv_g = _att_proj(x2, 0, n_p, ROW_TILE, gain1, w_g, q_gain, k_gain, BF16).reshape(nb_p, 1, len_p, 3 * ATT_W)
        else:
            halves = len(_residue_picks(dil)[1][0])
            qkv_g = _att_proj_dil(x2, nb_p, len_p, dil, gain1, w_g, q_gain, k_gain)
        outs.append(_att_band(qkv_g, dil, _band_bias(tab, dil, halves)))
        win_p.append(_window_rows(qkv_g, dil, caches[g].dtype)[None])
        new_rows = qkv_s[:, :, 1:, g].astype(caches[g].dtype)
        if g in moved:
            win_s.append(_window_tail(moved[g], new_rows)[None])
        else:
            win_s.append(_window_shift(caches[g], new_rows)[None])
        b = _bucket_rows(tab, (ATT_SPAN - jnp.arange(ATT_SPAN + 1)) * dil)
        sample_bias.append(jnp.broadcast_to(b[:, :, None], (ATT_SPAN + 1, ATT_G, LANES)))
    o_s = _att_sample(qkv_s, caches, sample_bias)
    x3_s = _mm_res(o_s.reshape(n_s, ATT_W).astype(BF16), w_out, x2_s, 0, n_s)
    moe1 = moe_weights(1)
    x3, buckets = _att_merge(outs, w_out, x2, n_p, ROW_TILE, x3_s, moe1[:3])
    y_p, y_s = _moe(x3, buckets, n_p, n_s, 1, *moe1)

    y_p = y_p.reshape(nb_p, len_p, D_MODEL)
    y_s = y_s.reshape(nb_s, len_s, D_MODEL)
    dt = state_ret.dtype
    return (y_p, y_s, ret_p[None].astype(dt), ret_s[None].astype(dt),
            win_p[0], win_s[0], win_p[1], win_s[1], win_p[2], win_s[2])
```

```python
import functools

import jax
import jax.numpy as jnp
from jax import lax
from jax.experimental import pallas as pl
from jax.experimental.pallas import tpu as pltpu

F32 = jnp.float32
BF16 = jnp.bfloat16

D_MODEL = 1024
PAST_LEN = 16384
RET_HEADS = 4
RET_DK = 256
RET_DV = 512
RET_QK = RET_HEADS * RET_DK
RET_VW = RET_HEADS * RET_DV
RET_IN = 2 * RET_QK + 2 * RET_VW
ROPE_BASE = 10000.0
ATT_GROUPS = ((128, 1), (512, 4), (2048, 16))
ATT_NG = len(ATT_GROUPS)
ATT_G = 8
ATT_HD = 128
ATT_W = ATT_G * ATT_HD
ATT_IN = 3 * ATT_NG * ATT_W
ATT_SPAN = 128
N_BUCKETS = 32
MAX_DISTANCE = 2048
MOE_GROUPS = 4
MOE_PER_GROUP = 4
MOE_EXPERTS = MOE_GROUPS * MOE_PER_GROUP
MOE_HIDDEN = 512
EPS = 1e-6
NEG = -1e30

LANES = 128
V7X_VMEM_LIMIT_BYTES = 56 * 1024 * 1024
ROW_TILE = 512
RET_CHUNK = 256
SAMPLE_PAD = 16
MOE_TILE = 512
WINDOW_SHIFT_ROWS = 256
WINDOW_SHIFT_DEPTH = 3
BACKGROUND_CHUNKS_PER_STEP = 4
ZERO_ROWS = 256
ATT_BAND_UNITS = 4
OUT_LSE = ATT_W + LANES


def _params(*sem):
    return pltpu.CompilerParams(dimension_semantics=sem, vmem_limit_bytes=V7X_VMEM_LIMIT_BYTES)


def _resident(shape):
    return pl.BlockSpec(shape, lambda *_: (0,) * len(shape), pipeline_mode=pl.Buffered(1))


def _rmsnorm_rows(x, gain):
    return x * lax.rsqrt(jnp.mean(x * x, axis=-1, keepdims=True) + EPS) * gain


def _dot(a, b):
    return jnp.dot(a, b, preferred_element_type=F32)


def _dot_nt(a, b):
    return lax.dot_general(a, b, (((1,), (1,)), ((), ())), preferred_element_type=F32)


def _dot_tn(a, b):
    return lax.dot_general(a, b, (((0,), (0,)), ((), ())), preferred_element_type=F32)


def _ret_proj_kernel(x_ref, g_ref, w_ref, cos_ref, sin_ref, *rest, shift=None, chunks=None):
    if shift is None:
        (o_ref,) = rest
    else:
        old_hbm, o_ref, moved_hbm, buf, rsem, wsem = rest
        background = _BackgroundShift(pl.program_id(0), pl.num_programs(0) - 1, old_hbm, moved_hbm, buf, rsem, wsem,
                                      shift, chunks)
        background.begin()
    xn = _rmsnorm_rows(x_ref[...], g_ref[...]).astype(BF16)
    cos = cos_ref[...]
    sin = sin_ref[...]
    even = (lax.broadcasted_iota(jnp.int32, cos.shape, 1) & 1) == 0
    for c in range(2 * RET_HEADS):
        sl = slice(c * RET_DK, (c + 1) * RET_DK)
        acc = _dot(xn, w_ref[:, sl])
        partner = jnp.where(even, pltpu.roll(acc, RET_DK - 1, 1), pltpu.roll(acc, 1, 1))
        r = acc * cos + partner * sin
        if c >= RET_HEADS:
            r = r * (RET_DK ** -0.5)
        o_ref[:, sl] = r.astype(o_ref.dtype)
    for c in range(2 * RET_QK // RET_DV, RET_IN // RET_DV):
        sl = slice(c * RET_DV, (c + 1) * RET_DV)
        o_ref[:, sl] = _dot(xn, w_ref[:, sl]).astype(o_ref.dtype)
    if shift is not None:
        background.end()


def _ret_proj(x, row_off, rows, tm, gain, w, cos, sin, window=None, shift=None):
    tab_blocks = cos.shape[0] // tm
    in_specs = [
        pl.BlockSpec((tm, D_MODEL), lambda i: (i + row_off, 0)),
        _resident((1, D_MODEL)),
        _resident((D_MODEL, RET_IN)),
        pl.BlockSpec((tm, RET_DK), lambda i: (i % tab_blocks, 0)),
        pl.BlockSpec((tm, RET_DK), lambda i: (i % tab_blocks, 0)),
    ]
    args = [x, gain, w, cos, sin]
    out_shape = [jax.ShapeDtypeStruct((rows, RET_IN), BF16)]
    out_specs = [pl.BlockSpec((tm, RET_IN), lambda i: (i, 0))]
    scratch, kwargs = [], {}
    if window is not None:
        in_spec, shape, out_spec, scratch, chunks = _background_shift_specs(window, rows // tm)
        in_specs.append(in_spec)
        args.append(window)
        out_shape.append(shape)
        out_specs.append(out_spec)
        kwargs = dict(shift=shift, chunks=chunks)
    out = pl.pallas_call(
        functools.partial(_ret_proj_kernel, **kwargs),
        out_shape=tuple(out_shape),
        grid=(rows // tm,),
        in_specs=in_specs,
        out_specs=tuple(out_specs),
        scratch_shapes=scratch,
        compiler_params=_params("arbitrary"),
        name="ret_proj",
    )(*args)
    return out if window is not None else out[0]


def _window_move(q, slot, old_hbm, out_hbm, buf, sem, shift, chunks, kind, action):
    b, c = q // chunks, q % chunks
    rows = buf.shape[1]
    if kind == "write":
        getattr(pltpu.make_async_copy(buf.at[slot], out_hbm.at[b, pl.ds(c * rows, rows)], sem.at[slot]), action)()
        return
    for last, n in ((False, rows), (True, rows - shift)):
        @pl.when(c == chunks - 1 if last else c != chunks - 1)
        def _():
            getattr(pltpu.make_async_copy(old_hbm.at[b, pl.ds(c * rows + shift, n)], buf.at[slot, pl.ds(0, n)],
                                          sem.at[slot]), action)()
            if last and action == "wait":
                buf[slot, rows - shift:rows] = jnp.zeros((shift,) + buf.shape[2:], buf.dtype)


class _BackgroundShift:
    def __init__(self, step, last_step, old_hbm, out_hbm, buf, rsem, wsem, shift, chunks):
        self.step, self.last_step, self.per_step = step, last_step, buf.shape[0] // 2
        self.args = (old_hbm, out_hbm, buf)
        self.rsem, self.wsem, self.shift, self.chunks = rsem, wsem, shift, chunks

    def _each(self, step, sem, kind, action):
        for j in range(self.per_step):
            slot = (step % 2) * self.per_step + j
            _window_move(step * self.per_step + j, slot, *self.args, sem, self.shift, self.chunks, kind, action)

    def begin(self):
        @pl.when(self.step >= 2)
        def _():
            self._each(self.step - 2, self.wsem, "write", "wait")
        self._each(self.step, self.rsem, "read", "start")

    def end(self):
        self._each(self.step, self.rsem, "read", "wait")
        self._each(self.step, self.wsem, "write", "start")

        @pl.when(self.step == self.last_step)
        def _():
            self._each(self.step, self.wsem, "write", "wait")

            @pl.when(self.step >= 1)
            def _():
                self._each(self.step - 1, self.wsem, "write", "wait")


def _background_chunk_rows(window, steps):
    for rows in (WINDOW_SHIFT_ROWS, WINDOW_SHIFT_ROWS // 2):
        chunks = window.shape[0] * (window.shape[1] // rows)
        if window.shape[1] % rows == 0 and chunks % steps == 0 and 1 <= chunks // steps <= BACKGROUND_CHUNKS_PER_STEP:
            return rows
    return None


def _background_shift_specs(window, steps):
    rows = _background_chunk_rows(window, steps)
    chunks = window.shape[1] // rows
    per_step = window.shape[0] * chunks // steps
    any_spec = pl.BlockSpec(memory_space=pl.ANY)
    scratch = [pltpu.VMEM((2 * per_step, rows) + window.shape[2:], window.dtype),
               pltpu.SemaphoreType.DMA((2 * per_step,)), pltpu.SemaphoreType.DMA((2 * per_step,))]
    return any_spec, jax.ShapeDtypeStruct(window.shape, window.dtype), any_spec, scratch, chunks


def _can_shift_behind(window, steps):
    return _background_chunk_rows(window, steps) is not None


def _ret_core_kernel(cdec_ref, q_ref, k_ref, v_ref, gt_ref, s0_ref, intra_ref, qdec_ref, kdec_ref, *rest,
                     shift=None, chunks=None):
    if shift is None:
        o_ref, st_ref = rest
    else:
        old_hbm, o_ref, st_ref, moved_hbm, buf, rsem, wsem = rest
        background = _BackgroundShift(pl.program_id(0) * pl.num_programs(1) + pl.program_id(1),
                                      pl.num_programs(0) * pl.num_programs(1) - 1,
                                      old_hbm, moved_hbm, buf, rsem, wsem, shift, chunks)
        background.begin()

    @pl.when(pl.program_id(1) == 0)
    def _():
        st_ref[...] = s0_ref[...]

    for h in range(RET_HEADS):
        qk = slice(h * RET_DK, (h + 1) * RET_DK)
        vv = slice(h * RET_DV, (h + 1) * RET_DV)
        qh, kh, vh = q_ref[:, qk], k_ref[:, qk], v_ref[:, vv]
        s = st_ref[0, h]
        sc = _dot_nt(qh, kh) * intra_ref[h]
        o = _dot(sc.astype(BF16), vh)
        o = o + _dot(qh, s.astype(BF16)) * jnp.concatenate([qdec_ref[h]] * (RET_DV // LANES), axis=1)
        kd = (kh.astype(F32) * jnp.concatenate([kdec_ref[h]] * (RET_DK // LANES), axis=1)).astype(BF16)
        st_ref[0, h] = s * cdec_ref[h] + _dot_tn(kd, vh)
        o = o * lax.rsqrt(jnp.mean(o * o, axis=-1, keepdims=True) + EPS)
        g = gt_ref[:, vv].astype(F32)
        o_ref[:, vv] = (o * (g * jax.nn.sigmoid(g))).astype(o_ref.dtype)

    if shift is not None:
        background.end()


def _ret_decay_tables(chunk, n_real):
    log_g = jnp.log(1.0 - 2.0 ** (-5.0 - jnp.arange(RET_HEADS, dtype=F32)))
    idx = jnp.arange(chunk, dtype=F32)
    diff = idx[:, None] - idx[None, :]
    intra = jnp.where(diff >= 0, jnp.exp(jnp.maximum(diff, 0.0)[None] * log_g[:, None, None]), 0.0)
    q_dec = jnp.exp((idx + 1.0)[None, :] * log_g[:, None])
    k_dec = jnp.where(idx[None, :] < n_real, jnp.exp((n_real - 1.0 - idx)[None, :] * log_g[:, None]), 0.0)
    c_dec = jnp.exp(n_real * log_g)
    rep = lambda t: jnp.broadcast_to(t[:, :, None], (RET_HEADS, chunk, LANES))
    return c_dec, intra, rep(q_dec), rep(k_dec)


def _ret_core(proj, s0, nseq, nchunk, chunk, n_real, window=None, shift=None):
    c_dec, intra, q_dec, k_dec = _ret_decay_tables(chunk, n_real)
    rows = nseq * nchunk * chunk
    row = lambda b, c: b * nchunk + c
    st_spec = pl.BlockSpec((1, RET_HEADS, RET_DK, RET_DV), lambda b, c: (b, 0, 0, 0))
    in_specs = [
        pl.BlockSpec(memory_space=pltpu.SMEM),
        pl.BlockSpec((chunk, RET_QK), lambda b, c: (row(b, c), 0)),
        pl.BlockSpec((chunk, RET_QK), lambda b, c: (row(b, c), 1)),
        pl.BlockSpec((chunk, RET_VW), lambda b, c: (row(b, c), 1)),
        pl.BlockSpec((chunk, RET_VW), lambda b, c: (row(b, c), 2)),
        st_spec,
        _resident((RET_HEADS, chunk, chunk)),
        _resident((RET_HEADS, chunk, LANES)),
        _resident((RET_HEADS, chunk, LANES)),
    ]
    args = [c_dec, proj, proj, proj, proj, s0, intra, q_dec, k_dec]
    out_shape = [jax.ShapeDtypeStruct((rows, RET_VW), BF16), jax.ShapeDtypeStruct((nseq, RET_HEADS, RET_DK, RET_DV), F32)]
    out_specs = [pl.BlockSpec((chunk, RET_VW), lambda b, c: (row(b, c), 0)), st_spec]
    scratch, kwargs = [], {}
    if window is not None:
        in_spec, shape, out_spec, scratch, chunks = _background_shift_specs(window, nseq * nchunk)
        in_specs.append(in_spec)
        args.append(window)
        out_shape.append(shape)
        out_specs.append(out_spec)
        kwargs = dict(shift=shift, chunks=chunks)
    return pl.pallas_call(
        functools.partial(_ret_core_kernel, **kwargs),
        out_shape=tuple(out_shape),
        grid=(nseq, nchunk),
        in_specs=in_specs,
        out_specs=tuple(out_specs),
        scratch_shapes=scratch,
        compiler_params=_params("arbitrary", "arbitrary"),
        name="ret_core",
    )(*args)


def _window_tail_kernel(new_ref, win_hbm, out_hbm, sem):
    b = pl.program_id(0)
    steps = new_ref.shape[1]
    cp = pltpu.make_async_copy(new_ref.at[0], out_hbm.at[b, pl.ds(out_hbm.shape[1] - steps, steps)], sem)
    cp.start()
    cp.wait()


def _window_tail(shifted, new):
    nb, steps = new.shape[:2]
    return pl.pallas_call(
        _window_tail_kernel,
        out_shape=jax.ShapeDtypeStruct(shifted.shape, shifted.dtype),
        grid=(nb,),
        in_specs=[pl.BlockSpec((1, steps) + new.shape[2:], lambda b: (b, 0, 0, 0, 0)), pl.BlockSpec(memory_space=pl.ANY)],
        out_specs=pl.BlockSpec(memory_space=pl.ANY),
        scratch_shapes=[pltpu.SemaphoreType.DMA(())],
        input_output_aliases={1: 0},
        compiler_params=_params("arbitrary"),
        name="window_tail",
    )(new, shifted)


def _with_tail_and_buckets(rows_fn, tail_ref, route_refs, o_ref, b_ref):
    last = pl.num_programs(0) - 1

    def emit(rows):
        o_ref[...] = rows
        b_ref[...] = _bucket_ids(rows, *route_refs)

    @pl.when(pl.program_id(0) < last)
    def _():
        emit(rows_fn())

    @pl.when(pl.program_id(0) == last)
    def _():
        pad = jnp.zeros((o_ref.shape[0] - tail_ref.shape[0], o_ref.shape[1]), o_ref.dtype)
        emit(jnp.concatenate([tail_ref[...], pad], axis=0))


def _route_specs():
    return [_resident((1, D_MODEL)), _resident((2, D_MODEL, LANES)), _resident((1, LANES))]


def _mm_res_kernel(a_ref, w_ref, x_ref, *rest):
    rows_fn = lambda: x_ref[...] + _dot(a_ref[...], w_ref[...])
    if len(rest) == 1:
        rest[0][...] = rows_fn()
    else:
        tail_ref, *route_refs, o_ref, b_ref = rest
        _with_tail_and_buckets(rows_fn, tail_ref, route_refs, o_ref, b_ref)


def _mm_res(a, w, x, x_off, tm, tail=None, route=None):
    m, k = a.shape
    nblk = m // tm
    clamp = (lambda i: jnp.minimum(i, nblk - 1)) if tail is not None else (lambda i: i)
    in_specs = [
        pl.BlockSpec((tm, k), lambda i: (clamp(i), 0)),
        _resident((k, D_MODEL)),
        pl.BlockSpec((tm, D_MODEL), lambda i: (clamp(i) + x_off, 0)),
    ]
    args = [a, w, x]
    steps = nblk
    out_shape = jax.ShapeDtypeStruct((nblk * tm, D_MODEL), F32)
    out_specs = pl.BlockSpec((tm, D_MODEL), lambda i: (i, 0))
    if tail is not None:
        steps = nblk + 1
        in_specs += [_resident(tail.shape)] + _route_specs()
        args += [tail, *route]
        out_shape = (jax.ShapeDtypeStruct((steps * tm, D_MODEL), F32), jax.ShapeDtypeStruct((steps * tm, LANES), jnp.int32))
        out_specs = (out_specs, pl.BlockSpec((tm, LANES), lambda i: (i, 0)))
    return pl.pallas_call(
        _mm_res_kernel,
        out_shape=out_shape,
        grid=(steps,),
        in_specs=in_specs,
        out_specs=out_specs,
        compiler_params=_params("arbitrary"),
        name="mm_res",
    )(*args)


ATT_PROJ_CHUNK = 4 * ATT_HD


def _att_proj_chunk(xn, w_ref, qg_ref, kg_ref, c, width):
    acc = _dot(xn, w_ref[:, c * ATT_PROJ_CHUNK:(c + 1) * ATT_PROJ_CHUNK])
    n_qk = 2 * (width // 3) // ATT_PROJ_CHUNK
    if c >= n_qk:
        return acc
    gain = qg_ref[...] if c < n_qk // 2 else kg_ref[...]
    heads = []
    for hh in range(ATT_PROJ_CHUNK // ATT_HD):
        a = acc[:, hh * ATT_HD:(hh + 1) * ATT_HD]
        heads.append(a * lax.rsqrt(jnp.mean(a * a, axis=-1, keepdims=True) + EPS) * gain)
    return jnp.concatenate(heads, axis=1)


def _att_proj_kernel(x_ref, g_ref, w_ref, qg_ref, kg_ref, o_ref):
    xn = _rmsnorm_rows(x_ref[...], g_ref[...]).astype(BF16)
    width = o_ref.shape[1]
    for c in range(width // ATT_PROJ_CHUNK):
        sl = slice(c * ATT_PROJ_CHUNK, (c + 1) * ATT_PROJ_CHUNK)
        o_ref[:, sl] = _att_proj_chunk(xn, w_ref, qg_ref, kg_ref, c, width).astype(o_ref.dtype)


def _att_proj_dil_kernel(x_hbm, g_ref, w_ref, qg_ref, kg_ref, o_ref, xbuf, sem, *, picks, rows):
    i = pl.program_id(0)
    slot = i % 2

    def fetch(step, sl, action):
        k = 0
        for mids in picks:
            for s in mids:
                cp = pltpu.make_async_copy(x_hbm.at[pl.ds(step * rows, rows), s], xbuf.at[sl, pl.ds(k * rows, rows)],
                                           sem.at[sl])
                getattr(cp, action)()
                k += 1

    @pl.when(i == 0)
    def _():
        fetch(0, 0, "start")

    @pl.when(i + 1 < pl.num_programs(0))
    def _():
        fetch(i + 1, 1 - slot, "start")

    fetch(i, slot, "wait")
    per = rows * len(picks[0])
    width = o_ref.shape[3]
    xn = _rmsnorm_rows(xbuf[slot], g_ref[...]).astype(BF16)
    for c in range(width // ATT_PROJ_CHUNK):
        sl = slice(c * ATT_PROJ_CHUNK, (c + 1) * ATT_PROJ_CHUNK)
        val = _att_proj_chunk(xn, w_ref, qg_ref, kg_ref, c, width).astype(o_ref.dtype)
        for r in range(len(picks)):
            o_ref[0, r, :, sl] = val[r * per:(r + 1) * per]


def _residue_picks(dil):
    if dil % 8 == 0:
        return dil, [[r] for r in range(dil)]
    assert 8 % dil == 0 and dil > 1
    return 8, [[r + dil * e for e in range(8 // dil)] for r in range(dil)]


def _att_proj_dil(x, nb, length, dil, gain, w, q_gain, k_gain):
    sub, picks = _residue_picks(dil)
    width = w.shape[1]
    rows = ATT_SPAN // 2
    per = rows * len(picks[0])
    ls = length // dil
    steps_per_seq = ls // per
    return pl.pallas_call(
        functools.partial(_att_proj_dil_kernel, picks=picks, rows=rows),
        out_shape=jax.ShapeDtypeStruct((nb, dil, ls, width), BF16),
        grid=(nb * steps_per_seq,),
        in_specs=[
            pl.BlockSpec(memory_space=pl.ANY),
            _resident((1, D_MODEL)),
            _resident((D_MODEL, width)),
            _resident((1, ATT_HD)),
            _resident((1, ATT_HD)),
        ],
        out_specs=pl.BlockSpec((1, dil, per, width), lambda i: (i // steps_per_seq, 0, i % steps_per_seq, 0)),
        scratch_shapes=[pltpu.VMEM((2, dil * per, D_MODEL), F32), pltpu.SemaphoreType.DMA((2,))],
        compiler_params=_params("arbitrary"),
        name="att_proj_dil",
    )(x.reshape(x.shape[0] // sub, sub, D_MODEL), gain, w, q_gain, k_gain)


def _att_proj(x, row_off, rows, tm, gain, w, q_gain, k_gain, out_dtype):
    width = w.shape[1]
    return pl.pallas_call(
        _att_proj_kernel,
        out_shape=jax.ShapeDtypeStruct((rows, width), out_dtype),
        grid=(rows // tm,),
        in_specs=[
            pl.BlockSpec((tm, D_MODEL), lambda i: (i + row_off, 0)),
            _resident((1, D_MODEL)),
            _resident((D_MODEL, width)),
            _resident((1, ATT_HD)),
            _resident((1, ATT_HD)),
        ],
        out_specs=pl.BlockSpec((tm, width), lambda i: (i, 0)),
        compiler_params=_params("parallel"),
        name="att_proj",
    )(x, gain, w, q_gain, k_gain)


def _att_band_kernel(q_ref, kp_ref, kc_ref, vp_ref, vc_ref, bias_ref, o_ref, *scratch, picks, blocks):
    n = pl.program_id(1)
    res = q_ref.shape[1]
    col = lax.broadcasted_iota(jnp.int32, (ATT_SPAN, 2 * ATT_SPAN), 1)
    no_prev = jnp.logical_and(col < ATT_SPAN, n == 0)
    lane = lax.broadcasted_iota(jnp.int32, (ATT_SPAN, LANES), 1)

    def unit(rr, j):
        lse_tile = jnp.zeros((ATT_SPAN, LANES), F32)
        heads = []
        for h in range(ATT_G):
            sl = slice(h * ATT_HD, (h + 1) * ATT_HD)
            if j == 0:
                kcat = jnp.concatenate([kp_ref[0, rr, :, sl], kc_ref[0, rr, 0:ATT_SPAN, sl]], axis=0)
                vcat = jnp.concatenate([vp_ref[0, rr, :, sl], vc_ref[0, rr, 0:ATT_SPAN, sl]], axis=0)
            else:
                kcat = kc_ref[0, rr, (j - 1) * ATT_SPAN:(j + 1) * ATT_SPAN, sl]
                vcat = vc_ref[0, rr, (j - 1) * ATT_SPAN:(j + 1) * ATT_SPAN, sl]
            s = _dot_nt(q_ref[0, rr, j * ATT_SPAN:(j + 1) * ATT_SPAN, sl], kcat) + bias_ref[h]
            if j == 0:
                s = jnp.where(no_prev, NEG, s)
            m = jnp.max(s, axis=-1, keepdims=True)
            p = jnp.exp(s - m)
            den = jnp.sum(p, axis=-1, keepdims=True)
            heads.append(_dot(p.astype(BF16), vcat) / den)
            lse_tile = jnp.where(lane == h, m + jnp.log(den), lse_tile)
        return jnp.concatenate(heads + [lse_tile], axis=1)

    if picks is None:
        for j in range(blocks):
            o_ref[0, j * ATT_SPAN:(j + 1) * ATT_SPAN, :] = unit(0, j)
        return

    obuf, sem = scratch
    b, g = pl.program_id(0), pl.program_id(2)
    steps = pl.num_programs(0) * pl.num_programs(1) * pl.num_programs(2)
    flat = (b * pl.num_programs(1) + n) * pl.num_programs(2) + g
    slot = flat % 2
    per = ATT_SPAN // len(picks[0])

    def writeback(sl, action):
        for rr in range(res):
            for e, mid in enumerate(picks[0]):
                cp = pltpu.make_async_copy(obuf.at[sl, rr, pl.ds(e * per, per)],
                                           o_ref.at[b, pl.ds(n * per, per), g * res + rr + mid], sem.at[sl])
                getattr(cp, action)()

    @pl.when(flat >= 2)
    def _():
        writeback(slot, "wait")

    for rr in range(res):
        obuf[slot, rr] = unit(rr, 0)
    writeback(slot, "start")

    @pl.when(flat == steps - 1)
    def _():
        writeback(slot, "wait")

        @pl.when(steps >= 2)
        def _():
            writeback(1 - slot, "wait")


def _att_band(qkv, dil, bias):
    nb, _, ls, _ = qkv.shape
    units = ATT_BAND_UNITS
    if dil == 1:
        picks, res, blocks = None, 1, units
        out_shape = (nb, ls, OUT_LSE)
        out_spec = pl.BlockSpec((1, blocks * ATT_SPAN, OUT_LSE), lambda b, n, g: (b, n, 0))
        scratch = []
    else:
        sub, picks = _residue_picks(dil)
        res, blocks = units, 1
        out_shape = (nb, ls * dil // sub, sub, OUT_LSE)
        out_spec = pl.BlockSpec(memory_space=pl.ANY)
        scratch = [pltpu.VMEM((2, res, ATT_SPAN, OUT_LSE), F32), pltpu.SemaphoreType.DMA((2,))]
    cur = lambda cb: pl.BlockSpec((1, res, blocks * ATT_SPAN, ATT_W), lambda b, n, g: (b, g, n, cb))
    prev = lambda cb: pl.BlockSpec((1, res, ATT_SPAN, ATT_W), lambda b, n, g: (b, g, jnp.maximum(n * blocks - 1, 0), cb))
    out = pl.pallas_call(
        functools.partial(_att_band_kernel, picks=picks, blocks=blocks),
        out_shape=jax.ShapeDtypeStruct(out_shape, F32),
        grid=(nb, ls // (blocks * ATT_SPAN), dil // res),
        in_specs=[cur(0), prev(1), cur(1), prev(2), cur(2), _resident((ATT_G, ATT_SPAN, 2 * ATT_SPAN))],
        out_specs=out_spec,
        scratch_shapes=scratch,
        compiler_params=_params("arbitrary", "arbitrary", "arbitrary"),
        name="att_band",
    )(qkv, qkv, qkv, qkv, qkv, bias)
    return out.reshape(nb * ls * dil, OUT_LSE)


def _window_rows_kernel(k_ref, v_ref, o_ref):
    halves = o_ref.shape[2]
    per = o_ref.shape[1]
    for e in range(halves):
        for kv, ref in enumerate((k_ref, v_ref)):
            for h in range(ATT_G):
                o_ref[0, :, e, 0, kv, h, :] = ref[0, 0, e * per:(e + 1) * per, h * ATT_HD:(h + 1) * ATT_HD].astype(o_ref.dtype)


def _window_rows(qkv, dil, dtype):
    nb, _, ls, _ = qkv.shape
    halves = 1 if dil == 1 else len(_residue_picks(dil)[1][0])
    per = ATT_SPAN // halves
    last = ls // ATT_SPAN - 1
    out = pl.pallas_call(
        _window_rows_kernel,
        out_shape=jax.ShapeDtypeStruct((nb, per, halves, dil, 2, ATT_G, ATT_HD), dtype),
        grid=(nb, dil),
        in_specs=[pl.BlockSpec((1, 1, ATT_SPAN, ATT_W), lambda b, r: (b, r, last, 1)),
                  pl.BlockSpec((1, 1, ATT_SPAN, ATT_W), lambda b, r: (b, r, last, 2))],
        out_specs=pl.BlockSpec((1, per, halves, 1, 2, ATT_G, ATT_HD), lambda b, r: (b, 0, 0, r, 0, 0, 0)),
        compiler_params=_params("parallel", "parallel"),
        name="window_rows",
    )(qkv, qkv)
    return out.reshape(nb, ATT_SPAN * dil, 2, ATT_G, ATT_HD)


def _att_merge_kernel(o0_ref, o1_ref, o2_ref, w_ref, x_ref, tail_ref, *rest):
    *route_refs, out_ref, b_ref = rest

    def rows_fn():
        acc = x_ref[...]
        refs = (o0_ref, o1_ref, o2_ref)
        for h in range(ATT_G):
            sl = slice(h * ATT_HD, (h + 1) * ATT_HD)
            lses = [r[:, ATT_W + h:ATT_W + h + 1] for r in refs]
            m = jnp.maximum(jnp.maximum(lses[0], lses[1]), lses[2])
            es = [jnp.exp(l - m) for l in lses]
            merged = (es[0] * o0_ref[:, sl] + es[1] * o1_ref[:, sl] + es[2] * o2_ref[:, sl]) / (es[0] + es[1] + es[2])
            acc = acc + _dot(merged.astype(BF16), w_ref[sl, :])
        return acc

    _with_tail_and_buckets(rows_fn, tail_ref, route_refs, out_ref, b_ref)


def _att_merge(outs, w, x, rows, tm, tail, route):
    nblk = rows // tm
    row_spec = lambda width: pl.BlockSpec((tm, width), lambda i: (jnp.minimum(i, nblk - 1), 0))
    return pl.pallas_call(
        _att_merge_kernel,
        out_shape=(jax.ShapeDtypeStruct(((nblk + 1) * tm, D_MODEL), F32),
                   jax.ShapeDtypeStruct(((nblk + 1) * tm, LANES), jnp.int32)),
        grid=(nblk + 1,),
        in_specs=[row_spec(OUT_LSE), row_spec(OUT_LSE), row_spec(OUT_LSE), _resident((ATT_W, D_MODEL)),
                  row_spec(D_MODEL), _resident(tail.shape)] + _route_specs(),
        out_specs=(pl.BlockSpec((tm, D_MODEL), lambda i: (i, 0)), pl.BlockSpec((tm, LANES), lambda i: (i, 0))),
        compiler_params=_params("arbitrary"),
        name="att_merge",
    )(*outs, w, x, tail, *route)


def _t5_bucket(dist):
    max_exact = N_BUCKETS // 2
    d32 = jnp.maximum(dist, 1).astype(F32)
    large = max_exact + (jnp.log(d32 / max_exact) / jnp.log(MAX_DISTANCE / max_exact)
                         * (N_BUCKETS - max_exact)).astype(jnp.int32)
    return jnp.where(dist < max_exact, dist, jnp.minimum(large, N_BUCKETS - 1))


def _bucket_rows(tab, dist):
    onehot = jax.nn.one_hot(_t5_bucket(dist), N_BUCKETS, dtype=F32)
    return jnp.dot(onehot, tab, precision=lax.Precision.HIGHEST)


def _band_bias(tab, dil, halves):
    pos = jnp.arange(ATT_SPAN).reshape(ATT_SPAN // halves, halves).T.reshape(-1)
    a = pos[:, None]
    c = jnp.concatenate([pos, ATT_SPAN + pos])[None, :]
    rel = a - c + ATT_SPAN
    bias = jnp.moveaxis(_bucket_rows(tab, jnp.clip(rel, 0, ATT_SPAN) * dil), -1, 0)
    return jnp.where(((rel >= 0) & (rel <= ATT_SPAN))[None], bias, NEG)


def _att_sample_kernel(qkv_ref, c0_ref, c1_ref, c2_ref, b0_ref, b1_ref, b2_ref, o_ref, *, steps):
    for t in range(steps):
        outs, lses = [], []
        for g, (c_ref, b_ref) in enumerate(((c0_ref, b0_ref), (c1_ref, b1_ref), (c2_ref, b2_ref))):
            dil = ATT_GROUPS[g][1]
            q = qkv_ref[0, t, 0, g]
            if dil == 1:
                keys = jnp.concatenate([c_ref[0, t:, 0, 0], qkv_ref[0, :t + 1, 1, g]], axis=0)
                vals = jnp.concatenate([c_ref[0, t:, 0, 1], qkv_ref[0, :t + 1, 2, g]], axis=0)
            else:
                keys = jnp.concatenate([c_ref[0, :, t, 0], qkv_ref[0, t:t + 1, 1, g]], axis=0)
                vals = jnp.concatenate([c_ref[0, :, t, 1], qkv_ref[0, t:t + 1, 2, g]], axis=0)
            s = jnp.sum(keys * q[None], axis=-1, keepdims=True) + b_ref[...]
            m = jnp.max(s, axis=0)
            p = jnp.exp(s - m[None])
            den = jnp.sum(p, axis=0)
            outs.append(jnp.sum(p * vals, axis=0) / den)
            lses.append(m + jnp.log(den))
        m = jnp.maximum(jnp.maximum(lses[0], lses[1]), lses[2])
        es = [jnp.exp(l - m) for l in lses]
        o_ref[0, t] = (es[0] * outs[0] + es[1] * outs[1] + es[2] * outs[2]) / (es[0] + es[1] + es[2])


def _att_sample(qkv, caches, biases):
    nb, steps = qkv.shape[:2]
    assert steps <= min(d for _, d in ATT_GROUPS[1:]) and all(c.shape[1] == w for c, (w, _) in zip(caches, ATT_GROUPS))
    views, view_specs = [], []
    for c, (win, dil) in zip(caches, ATT_GROUPS):
        views.append(c.reshape(nb, ATT_SPAN, dil, 2, ATT_G, ATT_HD))
        view_specs.append(pl.BlockSpec((1, ATT_SPAN, min(dil, steps), 2, ATT_G, ATT_HD), lambda b: (b, 0, 0, 0, 0, 0)))
    return pl.pallas_call(
        functools.partial(_att_sample_kernel, steps=steps),
        out_shape=jax.ShapeDtypeStruct((nb, steps, ATT_G, ATT_HD), F32),
        grid=(nb,),
        in_specs=[pl.BlockSpec((1, steps, 3, ATT_NG, ATT_G, ATT_HD), lambda b: (b, 0, 0, 0, 0, 0)),
                  *view_specs,
                  *[_resident((ATT_SPAN + 1, ATT_G, LANES))] * ATT_NG],
        out_specs=pl.BlockSpec((1, steps, ATT_G, ATT_HD), lambda b: (b, 0, 0, 0)),
        compiler_params=_params("parallel"),
        name="att_sample",
    )(qkv, *views, *biases)


def _window_shift_kernel(new_ref, old_hbm, out_hbm, buf, rsem, wsem, *, steps, chunks):
    s = pl.program_id(0)
    total = pl.num_programs(0)
    depth, rows = buf.shape[:2]

    def read(step, action):
        b, c, sl = step // chunks, step % chunks, step % depth

        @pl.when(c < chunks - 1)
        def _():
            getattr(pltpu.make_async_copy(old_hbm.at[b, pl.ds(c * rows + steps, rows)], buf.at[sl], rsem.at[sl]), action)()

        @pl.when(c == chunks - 1)
        def _():
            getattr(pltpu.make_async_copy(old_hbm.at[b, pl.ds(c * rows + steps, rows - steps)],
                                          buf.at[sl, pl.ds(0, rows - steps)], rsem.at[sl]), action)()

    def write(step, action):
        b, c, sl = step // chunks, step % chunks, step % depth
        getattr(pltpu.make_async_copy(buf.at[sl], out_hbm.at[b, pl.ds(c * rows, rows)], wsem.at[sl]), action)()

    @pl.when(s == 0)
    def _():
        for ahead in range(depth - 1):
            @pl.when(ahead < total)
            def _():
                read(ahead, "start")

    @pl.when(s >= 1)
    def _():
        write(s - 1, "wait")

    @pl.when(s + depth - 1 < total)
    def _():
        read(s + depth - 1, "start")

    read(s, "wait")

    @pl.when(s % chunks == chunks - 1)
    def _():
        buf[s % depth, rows - steps:rows] = new_ref[0]

    write(s, "start")

    @pl.when(s == total - 1)
    def _():
        write(s, "wait")


def _window_shift(old, new):
    nb, win = old.shape[:2]
    steps = new.shape[1]
    rows = min(win, WINDOW_SHIFT_ROWS)
    chunks = win // rows
    tail = old.shape[2:]
    any_spec = pl.BlockSpec(memory_space=pl.ANY)
    return pl.pallas_call(
        functools.partial(_window_shift_kernel, steps=steps, chunks=chunks),
        out_shape=jax.ShapeDtypeStruct(old.shape, old.dtype),
        grid=(nb * chunks,),
        in_specs=[pl.BlockSpec((1, steps) + tail, lambda s: (s // chunks, 0, 0, 0, 0)), any_spec],
        out_specs=any_spec,
        scratch_shapes=[pltpu.VMEM((WINDOW_SHIFT_DEPTH, rows) + tail, old.dtype),
                        pltpu.SemaphoreType.DMA((WINDOW_SHIFT_DEPTH,)), pltpu.SemaphoreType.DMA((WINDOW_SHIFT_DEPTH,))],
        compiler_params=_params("arbitrary"),
        name="window_shift",
    )(new, old)


def _router_logits(xn, wr_ref, br_ref):
    xh = xn.astype(BF16)
    xl = (xn - xh.astype(F32)).astype(BF16)
    return _dot(xh, wr_ref[0]) + (_dot(xh, wr_ref[1]) + _dot(xl, wr_ref[0])) + br_ref[...]


def _route(logits, grp):
    lane = lax.broadcasted_iota(jnp.int32, logits.shape, 1)
    is_group = lane < MOE_GROUPS
    mx = jnp.max(jnp.where(is_group, logits, NEG), axis=-1, keepdims=True)
    if grp is None:
        grp = jnp.min(jnp.where(jnp.logical_and(is_group, logits == mx), lane, LANES), axis=-1, keepdims=True)
    den = jnp.sum(jnp.where(is_group, jnp.exp(logits - mx), 0.0), axis=-1, keepdims=True)
    sel = jnp.sum(jnp.where(lane == grp, logits, 0.0), axis=-1, keepdims=True)
    p_group = jnp.exp(sel - mx) / den
    lo = MOE_GROUPS + MOE_PER_GROUP * grp
    le = jnp.where(jnp.logical_and(lane >= lo, lane < lo + MOE_PER_GROUP), logits, NEG)
    v1 = jnp.max(le, axis=-1, keepdims=True)
    i1 = jnp.min(jnp.where(le == v1, lane, LANES), axis=-1, keepdims=True)
    le = jnp.where(lane == i1, NEG, le)
    v2 = jnp.max(le, axis=-1, keepdims=True)
    i2 = jnp.min(jnp.where(le == v2, lane, LANES), axis=-1, keepdims=True)
    e2 = jnp.exp(v2 - v1)
    return grp, i1, i2, p_group / (1.0 + e2), p_group * e2 / (1.0 + e2)


MOE_PAIRS = MOE_PER_GROUP * (MOE_PER_GROUP - 1) // 2
MOE_BUCKETS = MOE_GROUPS * MOE_PAIRS


def _bucket_ids(x, g_ref, wr_ref, br_ref):
    logits = _router_logits(_rmsnorm_rows(x, g_ref[...]), wr_ref, br_ref)
    grp, i1, i2, _, _ = _route(logits, None)
    lo = MOE_GROUPS + MOE_PER_GROUP * grp
    a = jnp.minimum(i1, i2) - lo
    b = jnp.maximum(i1, i2) - lo
    pair = ((a * (2 * MOE_PER_GROUP - 1 - a)) >> 1) + b - a - 1
    return jnp.broadcast_to(grp * MOE_PAIRS + pair, logits.shape)


def _moe_dispatch_kernel(slot_ref, zt_ref, x_ref, xs_hbm, zero_ref, sem, zsem, *, n_rows, tile):
    i = pl.program_id(0)
    tm = x_ref.shape[0]
    count = jnp.minimum(tm, n_rows - i * tm)
    n_buckets = zt_ref.shape[0] - 2

    @pl.when(i == 0)
    def _():
        zero_ref[...] = jnp.zeros(zero_ref.shape, zero_ref.dtype)
        parts = tile // zero_ref.shape[0]

        def zero_tile(t, action):
            for part in range(parts):
                row = pl.multiple_of(t * tile + part * zero_ref.shape[0], zero_ref.shape[0])
                getattr(pltpu.make_async_copy(zero_ref, xs_hbm.at[pl.ds(row, zero_ref.shape[0])], zsem), action)()

        for action in ("start", "wait"):
            for b in range(n_buckets):
                @pl.when(zt_ref[b] >= 0)
                def _():
                    zero_tile(zt_ref[b], action)

            def unused(t, carry):
                zero_tile(t, action)
                return carry
            lax.fori_loop(zt_ref[n_buckets], zt_ref[n_buckets + 1], unused, 0)

    def start(k, carry):
        pltpu.make_async_copy(x_ref.at[pl.ds(k, 1)], xs_hbm.at[pl.ds(slot_ref[i * tm + k], 1)], sem).start()
        return carry

    def wait(k, carry):
        pltpu.make_async_copy(x_ref.at[pl.ds(0, 1)], xs_hbm.at[pl.ds(0, 1)], sem).wait()
        return carry

    @pl.when(count == tm)
    def _():
        for k in range(tm):
            start(k, 0)
        pltpu.make_async_copy(x_ref, xs_hbm.at[pl.ds(0, tm)], sem).wait()

    @pl.when(count < tm)
    def _():
        lax.fori_loop(0, count, start, 0)
        lax.fori_loop(0, count, wait, 0)


def _moe_dispatch(x, slot_of_row, zero_tiles, n_rows, n_slots, tm, tile):
    return pl.pallas_call(
        functools.partial(_moe_dispatch_kernel, n_rows=n_rows, tile=tile),
        out_shape=jax.ShapeDtypeStruct((n_slots, D_MODEL), F32),
        grid_spec=pltpu.PrefetchScalarGridSpec(
            num_scalar_prefetch=2,
            grid=(pl.cdiv(n_rows, tm),),
            in_specs=[pl.BlockSpec((tm, D_MODEL), lambda i, s, p: (i, 0))],
            out_specs=pl.BlockSpec(memory_space=pl.ANY),
            scratch_shapes=[pltpu.VMEM((ZERO_ROWS, D_MODEL), F32), pltpu.SemaphoreType.DMA(()),
                            pltpu.SemaphoreType.DMA(())]),
        compiler_params=_params("arbitrary"),
        name="moe_dispatch",
    )(slot_of_row, zero_tiles, x)


def _moe_collect_kernel(slot_ref, ys_hbm, o_ref, sem, *, row_off):
    i = pl.program_id(0)
    tm = o_ref.shape[0]

    for k in range(tm):
        pltpu.make_async_copy(ys_hbm.at[pl.ds(slot_ref[row_off + i * tm + k], 1)], o_ref.at[pl.ds(k, 1)], sem).start()
    pltpu.make_async_copy(ys_hbm.at[pl.ds(0, tm)], o_ref, sem).wait()


def _moe_collect(ys, slot_of_row, row_off, rows, tm):
    return pl.pallas_call(
        functools.partial(_moe_collect_kernel, row_off=row_off),
        out_shape=jax.ShapeDtypeStruct((rows, D_MODEL), F32),
        grid_spec=pltpu.PrefetchScalarGridSpec(
            num_scalar_prefetch=1,
            grid=(rows // tm,),
            in_specs=[pl.BlockSpec(memory_space=pl.ANY)],
            out_specs=pl.BlockSpec((tm, D_MODEL), lambda i, s: (i, 0)),
            scratch_shapes=[pltpu.SemaphoreType.DMA(())]),
        compiler_params=_params("arbitrary"),
        name="moe_collect",
    )(slot_of_row, ys)


def _moe_expert_kernel(ea_ref, eb_ref, used_ref, x_ref, gain_ref, wr_ref, br_ref,
                       wga_ref, wua_ref, wda_ref, wgb_ref, wub_ref, wdb_ref, y_ref):
    t = pl.program_id(0)

    @pl.when(t >= used_ref[0])
    def _():
        y_ref[...] = jnp.zeros(y_ref.shape, y_ref.dtype)

    @pl.when(t < used_ref[0])
    def _():
        x = x_ref[...]
        xn = _rmsnorm_rows(x, gain_ref[...])
        _, i1, i2, w1, w2 = _route(_router_logits(xn, wr_ref, br_ref), ea_ref[t] // MOE_PER_GROUP)
        xb = xn.astype(BF16)
        acc = x
        for e_ref, wg_ref, wu_ref, wd_ref in ((ea_ref, wga_ref, wua_ref, wda_ref), (eb_ref, wgb_ref, wub_ref, wdb_ref)):
            e_lane = MOE_GROUPS + e_ref[t]
            ce = jnp.where(i1 == e_lane, w1, 0.0) + jnp.where(i2 == e_lane, w2, 0.0)
            hg = _dot(xb, wg_ref[0, 0].astype(BF16))
            hid = hg * jax.nn.sigmoid(hg) * _dot(xb, wu_ref[0, 0].astype(BF16))
            acc = acc + ce * _dot(hid.astype(BF16), wd_ref[0, 0].astype(BF16))
        y_ref[...] = acc


def _moe_plan(bucket, tile, n_tiles):
    onehot = (bucket[:, None] == jnp.arange(MOE_BUCKETS)[None, :]).astype(jnp.int32)
    csum = jnp.cumsum(onehot, axis=0)
    counts = csum[-1]
    tiles_per = (counts + tile - 1) // tile
    tile_end = jnp.cumsum(tiles_per)
    tile_start = tile_end - tiles_per
    slot_of_row = jnp.sum(onehot * (tile_start[None, :] * tile + csum - 1), axis=1)
    tile_bucket = jnp.minimum(jnp.sum(jnp.arange(n_tiles)[:, None] >= tile_end[None, :], axis=1), MOE_BUCKETS - 1)
    pair = tile_bucket % MOE_PAIRS
    first = jnp.sum(pair[:, None] >= jnp.cumsum(jnp.arange(MOE_PER_GROUP - 1, 0, -1))[None, :], axis=1)
    second = pair - ((first * (2 * MOE_PER_GROUP - 1 - first)) >> 1) + first + 1
    base = (tile_bucket // MOE_PAIRS) * MOE_PER_GROUP
    used = tile_end[-1:]
    zero_tiles = jnp.concatenate([jnp.where(tiles_per > 0, tile_end - 1, -1), used, jnp.full((1,), n_tiles)])
    i32 = lambda v: v.astype(jnp.int32)
    return i32(slot_of_row), i32(base + first), i32(base + second), i32(used), i32(zero_tiles)


def _moe(x, buckets, n_prompt, n_sample, layer, gain, wr_pair, br, wg, wu, wd):
    n = n_prompt + n_sample
    tile = MOE_TILE
    n_tiles = n // tile + MOE_BUCKETS
    slot_of_row, expert_a, expert_b, n_used, zero_tiles = _moe_plan(buckets[:n, 0], tile, n_tiles)
    xs = _moe_dispatch(x, slot_of_row, zero_tiles, n, n_tiles * tile, ROW_TILE, tile)
    w_spec = lambda shape, which: pl.BlockSpec((1, 1) + shape, lambda t, ea, eb, u: (layer, (ea, eb)[which][t], 0, 0))
    const = lambda shape: pl.BlockSpec(shape, lambda t, ea, eb, u: (0,) * len(shape), pipeline_mode=pl.Buffered(1))
    w_specs = [w_spec(shape, which) for which in range(2)
               for shape in ((D_MODEL, MOE_HIDDEN), (D_MODEL, MOE_HIDDEN), (MOE_HIDDEN, D_MODEL))]
    ys = pl.pallas_call(
        _moe_expert_kernel,
        out_shape=jax.ShapeDtypeStruct((n_tiles * tile, D_MODEL), F32),
        grid_spec=pltpu.PrefetchScalarGridSpec(
            num_scalar_prefetch=3,
            grid=(n_tiles,),
            in_specs=[pl.BlockSpec((tile, D_MODEL), lambda t, ea, eb, u: (jnp.minimum(t, u[0] - 1), 0)),
                      const((1, D_MODEL)), const((2, D_MODEL, LANES)), const((1, LANES)), *w_specs],
            out_specs=pl.BlockSpec((tile, D_MODEL), lambda t, ea, eb, u: (t, 0))),
        compiler_params=_params("arbitrary"),
        name="moe_experts",
    )(expert_a, expert_b, n_used, xs, gain, wr_pair, br, wg, wu, wd, wg, wu, wd)
    return (_moe_collect(ys, slot_of_row, 0, n_prompt, ROW_TILE),
            _moe_collect(ys, slot_of_row, n_prompt, n_sample, n_sample))


def _rope_tables(pos):
    half = RET_DK // 2
    inv = 1.0 / (ROPE_BASE ** jnp.linspace(0.0, 1.0, half, dtype=F32))
    ang = jnp.repeat(pos.astype(F32)[:, None] * inv[None, :], 2, axis=-1)
    sign = jnp.where(jnp.arange(RET_DK) % 2 == 0, -1.0, 1.0).astype(F32)
    return jnp.cos(ang), jnp.sin(ang) * sign


def kernel(x_prompt, x_sample, state_ret, cache_win0, cache_win1, cache_win2, rel_bias, norm_mix, norm_ffn, w_in_ret, w_out_ret, w_in_att, q_norm, k_norm, w_out_att, w_router_group, b_router_group, w_router_expert, b_router_expert, w_gate, w_up, w_down):
    nb_p, len_p, _ = x_prompt.shape
    nb_s, len_s, _ = x_sample.shape
    n_p, n_s = nb_p * len_p, nb_s * len_s
    n_all = n_p + n_s
    assert n_p % ROW_TILE == 0 and n_p % n_s == 0 and len_p % RET_CHUNK == 0 and len_s <= SAMPLE_PAD
    assert all(len_p % (ATT_SPAN * d) == 0 for _, d in ATT_GROUPS)
    caches = (cache_win0[0], cache_win1[0], cache_win2[0])

    xp = x_prompt.reshape(n_p, D_MODEL)
    xs = x_sample.reshape(n_s, D_MODEL)
    row = lambda v: v.reshape(1, -1).astype(F32)

    def moe_weights(i):
        wr = jnp.zeros((D_MODEL, LANES), F32)
        wr = wr.at[:, :MOE_GROUPS].set(w_router_group[i]).at[:, MOE_GROUPS:MOE_GROUPS + MOE_EXPERTS].set(w_router_expert[i])
        br = jnp.zeros((1, LANES), F32)
        br = br.at[0, :MOE_GROUPS].set(b_router_group[i]).at[0, MOE_GROUPS:MOE_GROUPS + MOE_EXPERTS].set(b_router_expert[i])
        wr_hi = wr.astype(BF16)
        wr_pair = jnp.stack([wr_hi, (wr - wr_hi.astype(F32)).astype(BF16)])
        return (row(norm_ffn[i]), wr_pair, br, w_gate, w_up, w_down)

    gain0 = row(norm_mix[0])
    w_in = w_in_ret[0].astype(BF16)
    w_out = w_out_ret[0].astype(BF16)
    cos_p, sin_p = _rope_tables(jnp.arange(len_p))
    pos_s = jnp.tile(PAST_LEN + jnp.arange(SAMPLE_PAD), nb_s)
    cos_s, sin_s = _rope_tables(pos_s)
    xs_pad = jnp.pad(x_sample, ((0, 0), (0, SAMPLE_PAD - len_s), (0, 0))).reshape(nb_s * SAMPLE_PAD, D_MODEL)

    by_size = sorted(range(ATT_NG), key=lambda g: -caches[g].shape[1])
    moved = {}
    zero_state = jnp.zeros((nb_p, RET_HEADS, RET_DK, RET_DV), F32)
    g_proj, g_core = by_size[0], by_size[1]
    if _can_shift_behind(caches[g_proj], n_p // ROW_TILE):
        proj_p, moved[g_proj] = _ret_proj(xp, 0, n_p, ROW_TILE, gain0, w_in, cos_p, sin_p, caches[g_proj], len_s)
    else:
        proj_p = _ret_proj(xp, 0, n_p, ROW_TILE, gain0, w_in, cos_p, sin_p)
    proj_s = _ret_proj(xs_pad, 0, nb_s * SAMPLE_PAD, nb_s * SAMPLE_PAD, gain0, w_in, cos_s, sin_s)
    if _can_shift_behind(caches[g_core], nb_p * (len_p // RET_CHUNK)):
        o_p, ret_p, moved[g_core] = _ret_core(proj_p, zero_state, nb_p, len_p // RET_CHUNK, RET_CHUNK, RET_CHUNK,
                                              window=caches[g_core], shift=len_s)
    else:
        o_p, ret_p = _ret_core(proj_p, zero_state, nb_p, len_p // RET_CHUNK, RET_CHUNK, RET_CHUNK)
    o_s, ret_s = _ret_core(proj_s, state_ret[0].astype(F32), nb_s, 1, SAMPLE_PAD, len_s)
    o_s = o_s.reshape(nb_s, SAMPLE_PAD, RET_VW)[:, :len_s].reshape(n_s, RET_VW)
    moe0 = moe_weights(0)
    x1, buckets = _mm_res(o_p, w_out, xp, 0, ROW_TILE, tail=_mm_res(o_s, w_out, xs, 0, n_s), route=moe0[:3])
    x2, x2_s = _moe(x1, buckets, n_p, n_s, 0, *moe0)

    gain1 = row(norm_mix[1])
    w_in = w_in_att[0].astype(BF16)
    w_out = w_out_att[0].astype(BF16)
    q_gain = row(q_norm[0]) * (ATT_HD ** -0.5)
    k_gain = row(k_norm[0])
    qkv_s = _att_proj(x2_s, 0, n_s, n_s, gain1, w_in, q_gain, k_gain, F32)
    qkv_s = qkv_s.reshape(nb_s, len_s, 3, ATT_NG, ATT_G, ATT_HD)

    outs, win_p, win_s, sample_bias = [], [], [], []
    for g, (win, dil) in enumerate(ATT_GROUPS):
        tab = rel_bias[:, g * ATT_G:(g + 1) * ATT_G].astype(F32)
        w_g = jnp.concatenate([w_in[:, (i * ATT_NG + g) * ATT_W:(i * ATT_NG + g + 1) * ATT_W] for i in range(3)], axis=1)
        if dil == 1:
            halves = 1
            qkv_g = _att_proj(x2, 0, n_p, ROW_TILE, gain1, w_g, q_gain, k_gain, BF16).reshape(nb_p, 1, len_p, 3 * ATT_W)
        else:
            halves = len(_residue_picks(dil)[1][0])
            qkv_g = _att_proj_dil(x2, nb_p, len_p, dil, gain1, w_g, q_gain, k_gain)
        outs.append(_att_band(qkv_g, dil, _band_bias(tab, dil, halves)))
        win_p.append(_window_rows(qkv_g, dil, caches[g].dtype)[None])
        new_rows = qkv_s[:, :, 1:, g].astype(caches[g].dtype)
        if g in moved:
            win_s.append(_window_tail(moved[g], new_rows)[None])
        else:
            win_s.append(_window_shift(caches[g], new_rows)[None])
        b = _bucket_rows(tab, (ATT_SPAN - jnp.arange(ATT_SPAN + 1)) * dil)
        sample_bias.append(jnp.broadcast_to(b[:, :, None], (ATT_SPAN + 1, ATT_G, LANES)))
    o_s = _att_sample(qkv_s, caches, sample_bias)
    x3_s = _mm_res(o_s.reshape(n_s, ATT_W).astype(BF16), w_out, x2_s, 0, n_s)
    moe1 = moe_weights(1)
    x3, buckets = _att_merge(outs, w_out, x2, n_p, ROW_TILE, x3_s, moe1[:3])
    y_p, y_s = _moe(x3, buckets, n_p, n_s, 1, *moe1)

    y_p = y_p.reshape(nb_p, len_p, D_MODEL)
    y_s = y_s.reshape(nb_s, len_s, D_MODEL)
    dt = state_ret.dtype
    return (y_p, y_s, ret_p[None].astype(dt), ret_s[None].astype(dt),
            win_p[0], win_s[0], win_p[1], win_s[1], win_p[2], win_s[2])
```

```python
import functools

import jax
import jax.numpy as jnp
from jax import lax
from jax.experimental import pallas as pl
from jax.experimental.pallas import tpu as pltpu

F32 = jnp.float32
BF16 = jnp.bfloat16

D_MODEL = 1024
PAST_LEN = 16384
RET_HEADS = 4
RET_DK = 256
RET_DV = 512
RET_QK = RET_HEADS * RET_DK
RET_VW = RET_HEADS * RET_DV
RET_IN = 2 * RET_QK + 2 * RET_VW
ROPE_BASE = 10000.0
ATT_GROUPS = ((128, 1), (512, 4), (2048, 16))
ATT_NG = len(ATT_GROUPS)
ATT_G = 8
ATT_HD = 128
ATT_W = ATT_G * ATT_HD
ATT_IN = 3 * ATT_NG * ATT_W
ATT_SPAN = 128
N_BUCKETS = 32
MAX_DISTANCE = 2048
MOE_GROUPS = 4
MOE_PER_GROUP = 4
MOE_EXPERTS = MOE_GROUPS * MOE_PER_GROUP
MOE_HIDDEN = 512
EPS = 1e-6
NEG = -1e30

LANES = 128
V7X_VMEM_LIMIT_BYTES = 56 * 1024 * 1024
ROW_TILE = 512
RET_CHUNK = 256
SAMPLE_PAD = 16
MOE_TILE = 512
WINDOW_SHIFT_ROWS = 256
WINDOW_SHIFT_DEPTH = 3
BACKGROUND_CHUNKS_PER_STEP = 4
ZERO_ROWS = 256
ATT_BAND_UNITS = 4
OUT_LSE = ATT_W + LANES


def _params(*sem):
    return pltpu.CompilerParams(dimension_semantics=sem, vmem_limit_bytes=V7X_VMEM_LIMIT_BYTES)


def _resident(shape):
    return pl.BlockSpec(shape, lambda *_: (0,) * len(shape), pipeline_mode=pl.Buffered(1))


def _rmsnorm_rows(x, gain):
    return x * lax.rsqrt(jnp.mean(x * x, axis=-1, keepdims=True) + EPS) * gain


def _dot(a, b):
    return jnp.dot(a, b, preferred_element_type=F32)


def _dot_nt(a, b):
    return lax.dot_general(a, b, (((1,), (1,)), ((), ())), preferred_element_type=F32)


def _dot_tn(a, b):
    return lax.dot_general(a, b, (((0,), (0,)), ((), ())), preferred_element_type=F32)


def _ret_proj_kernel(x_ref, g_ref, w_ref, cos_ref, sin_ref, *rest, shift=None, chunks=None):
    if shift is None:
        (o_ref,) = rest
    else:
        old_hbm, o_ref, moved_hbm, buf, rsem, wsem = rest
        background = _BackgroundShift(pl.program_id(0), pl.num_programs(0) - 1, old_hbm, moved_hbm, buf, rsem, wsem,
                                      shift, chunks)
        background.begin()
    xn = _rmsnorm_rows(x_ref[...], g_ref[...]).astype(BF16)
    cos = cos_ref[...]
    sin = sin_ref[...]
    even = (lax.broadcasted_iota(jnp.int32, cos.shape, 1) & 1) == 0
    for c in range(2 * RET_HEADS):
        sl = slice(c * RET_DK, (c + 1) * RET_DK)
        acc = _dot(xn, w_ref[:, sl])
        partner = jnp.where(even, pltpu.roll(acc, RET_DK - 1, 1), pltpu.roll(acc, 1, 1))
        r = acc * cos + partner * sin
        if c >= RET_HEADS:
            r = r * (RET_DK ** -0.5)
        o_ref[:, sl] = r.astype(o_ref.dtype)
    for c in range(2 * RET_QK // RET_DV, RET_IN // RET_DV):
        sl = slice(c * RET_DV, (c + 1) * RET_DV)
        o_ref[:, sl] = _dot(xn, w_ref[:, sl]).astype(o_ref.dtype)
    if shift is not None:
        background.end()


def _ret_proj(x, row_off, rows, tm, gain, w, cos, sin, window=None, shift=None):
    tab_blocks = cos.shape[0] // tm
    in_specs = [
        pl.BlockSpec((tm, D_MODEL), lambda i: (i + row_off, 0)),
        _resident((1, D_MODEL)),
        _resident((D_MODEL, RET_IN)),
        pl.BlockSpec((tm, RET_DK), lambda i: (i % tab_blocks, 0)),
        pl.BlockSpec((tm, RET_DK), lambda i: (i % tab_blocks, 0)),
    ]
    args = [x, gain, w, cos, sin]
    out_shape = [jax.ShapeDtypeStruct((rows, RET_IN), BF16)]
    out_specs = [pl.BlockSpec((tm, RET_IN), lambda i: (i, 0))]
    scratch, kwargs = [], {}
    if window is not None:
        in_spec, shape, out_spec, scratch, chunks = _background_shift_specs(window, rows // tm)
        in_specs.append(in_spec)
        args.append(window)
        out_shape.append(shape)
        out_specs.append(out_spec)
        kwargs = dict(shift=shift, chunks=chunks)
    out = pl.pallas_call(
        functools.partial(_ret_proj_kernel, **kwargs),
        out_shape=tuple(out_shape),
        grid=(rows // tm,),
        in_specs=in_specs,
        out_specs=tuple(out_specs),
        scratch_shapes=scratch,
        compiler_params=_params("arbitrary"),
        name="ret_proj",
    )(*args)
    return out if window is not None else out[0]


def _window_move(q, slot, old_hbm, out_hbm, buf, sem, shift, chunks, kind, action):
    b, c = q // chunks, q % chunks
    rows = buf.shape[1]
    if kind == "write":
        getattr(pltpu.make_async_copy(buf.at[slot], out_hbm.at[b, pl.ds(c * rows, rows)], sem.at[slot]), action)()
        return
    for last, n in ((False, rows), (True, rows - shift)):
        @pl.when(c == chunks - 1 if last else c != chunks - 1)
        def _():
            getattr(pltpu.make_async_copy(old_hbm.at[b, pl.ds(c * rows + shift, n)], buf.at[slot, pl.ds(0, n)],
                                          sem.at[slot]), action)()
            if last and action == "wait":
                buf[slot, rows - shift:rows] = jnp.zeros((shift,) + buf.shape[2:], buf.dtype)


class _BackgroundShift:
    def __init__(self, step, last_step, old_hbm, out_hbm, buf, rsem, wsem, shift, chunks):
        self.step, self.last_step, self.per_step = step, last_step, buf.shape[0] // 2
        self.args = (old_hbm, out_hbm, buf)
        self.rsem, self.wsem, self.shift, self.chunks = rsem, wsem, shift, chunks

    def _each(self, step, sem, kind, action):
        for j in range(self.per_step):
            slot = (step % 2) * self.per_step + j
            _window_move(step * self.per_step + j, slot, *self.args, sem, self.shift, self.chunks, kind, action)

    def begin(self):
        @pl.when(self.step >= 2)
        def _():
            self._each(self.step - 2, self.wsem, "write", "wait")
        self._each(self.step, self.rsem, "read", "start")

    def end(self):
        self._each(self.step, self.rsem, "read", "wait")
        self._each(self.step, self.wsem, "write", "start")

        @pl.when(self.step == self.last_step)
        def _():
            self._each(self.step, self.wsem, "write", "wait")

            @pl.when(self.step >= 1)
            def _():
                self._each(self.step - 1, self.wsem, "write", "wait")


def _background_chunk_rows(window, steps):
    for rows in (WINDOW_SHIFT_ROWS, WINDOW_SHIFT_ROWS // 2):
        chunks = window.shape[0] * (window.shape[1] // rows)
        if window.shape[1] % rows == 0 and chunks % steps == 0 and 1 <= chunks // steps <= BACKGROUND_CHUNKS_PER_STEP:
            return rows
    return None


def _background_shift_specs(window, steps):
    rows = _background_chunk_rows(window, steps)
    chunks = window.shape[1] // rows
    per_step = window.shape[0] * chunks // steps
    any_spec = pl.BlockSpec(memory_space=pl.ANY)
    scratch = [pltpu.VMEM((2 * per_step, rows) + window.shape[2:], window.dtype),
               pltpu.SemaphoreType.DMA((2 * per_step,)), pltpu.SemaphoreType.DMA((2 * per_step,))]
    return any_spec, jax.ShapeDtypeStruct(window.shape, window.dtype), any_spec, scratch, chunks


def _can_shift_behind(window, steps):
    return _background_chunk_rows(window, steps) is not None


def _ret_core_kernel(cdec_ref, q_ref, k_ref, v_ref, gt_ref, s0_ref, intra_ref, qdec_ref, kdec_ref, *rest,
                     shift=None, chunks=None):
    if shift is None:
        o_ref, st_ref = rest
    else:
        old_hbm, o_ref, st_ref, moved_hbm, buf, rsem, wsem = rest
        background = _BackgroundShift(pl.program_id(0) * pl.num_programs(1) + pl.program_id(1),
                                      pl.num_programs(0) * pl.num_programs(1) - 1,
                                      old_hbm, moved_hbm, buf, rsem, wsem, shift, chunks)
        background.begin()

    @pl.when(pl.program_id(1) == 0)
    def _():
        st_ref[...] = s0_ref[...]

    for h in range(RET_HEADS):
        qk = slice(h * RET_DK, (h + 1) * RET_DK)
        vv = slice(h * RET_DV, (h + 1) * RET_DV)
        qh, kh, vh = q_ref[:, qk], k_ref[:, qk], v_ref[:, vv]
        s = st_ref[0, h]
        sc = _dot_nt(qh, kh) * intra_ref[h]
        o = _dot(sc.astype(BF16), vh)
        o = o + _dot(qh, s.astype(BF16)) * jnp.concatenate([qdec_ref[h]] * (RET_DV // LANES), axis=1)
        kd = (kh.astype(F32) * jnp.concatenate([kdec_ref[h]] * (RET_DK // LANES), axis=1)).astype(BF16)
        st_ref[0, h] = s * cdec_ref[h] + _dot_tn(kd, vh)
        o = o * lax.rsqrt(jnp.mean(o * o, axis=-1, keepdims=True) + EPS)
        g = gt_ref[:, vv].astype(F32)
        o_ref[:, vv] = (o * (g * jax.nn.sigmoid(g))).astype(o_ref.dtype)

    if shift is not None:
        background.end()


def _ret_decay_tables(chunk, n_real):
    log_g = jnp.log(1.0 - 2.0 ** (-5.0 - jnp.arange(RET_HEADS, dtype=F32)))
    idx = jnp.arange(chunk, dtype=F32)
    diff = idx[:, None] - idx[None, :]
    intra = jnp.where(diff >= 0, jnp.exp(jnp.maximum(diff, 0.0)[None] * log_g[:, None, None]), 0.0)
    q_dec = jnp.exp((idx + 1.0)[None, :] * log_g[:, None])
    k_dec = jnp.where(idx[None, :] < n_real, jnp.exp((n_real - 1.0 - idx)[None, :] * log_g[:, None]), 0.0)
    c_dec = jnp.exp(n_real * log_g)
    rep = lambda t: jnp.broadcast_to(t[:, :, None], (RET_HEADS, chunk, LANES))
    return c_dec, intra, rep(q_dec), rep(k_dec)


def _ret_core(proj, s0, nseq, nchunk, chunk, n_real, window=None, shift=None):
    c_dec, intra, q_dec, k_dec = _ret_decay_tables(chunk, n_real)
    rows = nseq * nchunk * chunk
    row = lambda b, c: b * nchunk + c
    st_spec = pl.BlockSpec((1, RET_HEADS, RET_DK, RET_DV), lambda b, c: (b, 0, 0, 0))
    in_specs = [
        pl.BlockSpec(memory_space=pltpu.SMEM),
        pl.BlockSpec((chunk, RET_QK), lambda b, c: (row(b, c), 0)),
        pl.BlockSpec((chunk, RET_QK), lambda b, c: (row(b, c), 1)),
        pl.BlockSpec((chunk, RET_VW), lambda b, c: (row(b, c), 1)),
        pl.BlockSpec((chunk, RET_VW), lambda b, c: (row(b, c), 2)),
        st_spec,
        _resident((RET_HEADS, chunk, chunk)),
        _resident((RET_HEADS, chunk, LANES)),
        _resident((RET_HEADS, chunk, LANES)),
    ]
    args = [c_dec, proj, proj, proj, proj, s0, intra, q_dec, k_dec]
    out_shape = [jax.ShapeDtypeStruct((rows, RET_VW), BF16), jax.ShapeDtypeStruct((nseq, RET_HEADS, RET_DK, RET_DV), F32)]
    out_specs = [pl.BlockSpec((chunk, RET_VW), lambda b, c: (row(b, c), 0)), st_spec]
    scratch, kwargs = [], {}
    if window is not None:
        in_spec, shape, out_spec, scratch, chunks = _background_shift_specs(window, nseq * nchunk)
        in_specs.append(in_spec)
        args.append(window)
        out_shape.append(shape)
        out_specs.append(out_spec)
        kwargs = dict(shift=shift, chunks=chunks)
    return pl.pallas_call(
        functools.partial(_ret_core_kernel, **kwargs),
        out_shape=tuple(out_shape),
        grid=(nseq, nchunk),
        in_specs=in_specs,
        out_specs=tuple(out_specs),
        scratch_shapes=scratch,
        compiler_params=_params("arbitrary", "arbitrary"),
        name="ret_core",
    )(*args)


def _window_tail_kernel(new_ref, win_hbm, out_hbm, sem):
    b = pl.program_id(0)
    steps = new_ref.shape[1]
    cp = pltpu.make_async_copy(new_ref.at[0], out_hbm.at[b, pl.ds(out_hbm.shape[1] - steps, steps)], sem)
    cp.start()
    cp.wait()


def _window_tail(shifted, new):
    nb, steps = new.shape[:2]
    return pl.pallas_call(
        _window_tail_kernel,
        out_shape=jax.ShapeDtypeStruct(shifted.shape, shifted.dtype),
        grid=(nb,),
        in_specs=[pl.BlockSpec((1, steps) + new.shape[2:], lambda b: (b, 0, 0, 0, 0)), pl.BlockSpec(memory_space=pl.ANY)],
        out_specs=pl.BlockSpec(memory_space=pl.ANY),
        scratch_shapes=[pltpu.SemaphoreType.DMA(())],
        input_output_aliases={1: 0},
        compiler_params=_params("arbitrary"),
        name="window_tail",
    )(new, shifted)


def _with_tail_and_buckets(rows_fn, tail_ref, route_refs, o_ref, b_ref):
    last = pl.num_programs(0) - 1

    def emit(rows):
        o_ref[...] = rows
        b_ref[...] = _bucket_ids(rows, *route_refs)

    @pl.when(pl.program_id(0) < last)
    def _():
        emit(rows_fn())

    @pl.when(pl.program_id(0) == last)
    def _():
        pad = jnp.zeros((o_ref.shape[0] - tail_ref.shape[0], o_ref.shape[1]), o_ref.dtype)
        emit(jnp.concatenate([tail_ref[...], pad], axis=0))


def _route_specs():
    return [_resident((1, D_MODEL)), _resident((2, D_MODEL, LANES)), _resident((1, LANES))]


def _mm_res_kernel(a_ref, w_ref, x_ref, *rest):
    rows_fn = lambda: x_ref[...] + _dot(a_ref[...], w_ref[...])
    if len(rest) == 1:
        rest[0][...] = rows_fn()
    else:
        tail_ref, *route_refs, o_ref, b_ref = rest
        _with_tail_and_buckets(rows_fn, tail_ref, route_refs, o_ref, b_ref)


def _mm_res(a, w, x, x_off, tm, tail=None, route=None):
    m, k = a.shape
    nblk = m // tm
    clamp = (lambda i: jnp.minimum(i, nblk - 1)) if tail is not None else (lambda i: i)
    in_specs = [
        pl.BlockSpec((tm, k), lambda i: (clamp(i), 0)),
        _resident((k, D_MODEL)),
        pl.BlockSpec((tm, D_MODEL), lambda i: (clamp(i) + x_off, 0)),
    ]
    args = [a, w, x]
    steps = nblk
    out_shape = jax.ShapeDtypeStruct((nblk * tm, D_MODEL), F32)
    out_specs = pl.BlockSpec((tm, D_MODEL), lambda i: (i, 0))
    if tail is not None:
        steps = nblk + 1
        in_specs += [_resident(tail.shape)] + _route_specs()
        args += [tail, *route]
        out_shape = (jax.ShapeDtypeStruct((steps * tm, D_MODEL), F32), jax.ShapeDtypeStruct((steps * tm, LANES), jnp.int32))
        out_specs = (out_specs, pl.BlockSpec((tm, LANES), lambda i: (i, 0)))
    return pl.pallas_call(
        _mm_res_kernel,
        out_shape=out_shape,
        grid=(steps,),
        in_specs=in_specs,
        out_specs=out_specs,
        compiler_params=_params("arbitrary"),
        name="mm_res",
    )(*args)


ATT_PROJ_CHUNK = 4 * ATT_HD


def _att_proj_chunk(xn, w_ref, qg_ref, kg_ref, c, width):
    acc = _dot(xn, w_ref[:, c * ATT_PROJ_CHUNK:(c + 1) * ATT_PROJ_CHUNK])
    n_qk = 2 * (width // 3) // ATT_PROJ_CHUNK
    if c >= n_qk:
        return acc
    gain = qg_ref[...] if c < n_qk // 2 else kg_ref[...]
    heads = []
    for hh in range(ATT_PROJ_CHUNK // ATT_HD):
        a = acc[:, hh * ATT_HD:(hh + 1) * ATT_HD]
        heads.append(a * lax.rsqrt(jnp.mean(a * a, axis=-1, keepdims=True) + EPS) * gain)
    return jnp.concatenate(heads, axis=1)


def _att_proj_kernel(x_ref, g_ref, w_ref, qg_ref, kg_ref, o_ref):
    xn = _rmsnorm_rows(x_ref[...], g_ref[...]).astype(BF16)
    width = o_ref.shape[1]
    for c in range(width // ATT_PROJ_CHUNK):
        sl = slice(c * ATT_PROJ_CHUNK, (c + 1) * ATT_PROJ_CHUNK)
        o_ref[:, sl] = _att_proj_chunk(xn, w_ref, qg_ref, kg_ref, c, width).astype(o_ref.dtype)


def _att_proj_dil_kernel(x_hbm, g_ref, w_ref, qg_ref, kg_ref, o_ref, xbuf, sem, *, picks, rows):
    i = pl.program_id(0)
    slot = i % 2

    def fetch(step, sl, action):
        k = 0
        for mids in picks:
            for s in mids:
                cp = pltpu.make_async_copy(x_hbm.at[pl.ds(step * rows, rows), s], xbuf.at[sl, pl.ds(k * rows, rows)],
                                           sem.at[sl])
                getattr(cp, action)()
                k += 1

    @pl.when(i == 0)
    def _():
        fetch(0, 0, "start")

    @pl.when(i + 1 < pl.num_programs(0))
    def _():
        fetch(i + 1, 1 - slot, "start")

    fetch(i, slot, "wait")
    per = rows * len(picks[0])
    width = o_ref.shape[3]
    xn = _rmsnorm_rows(xbuf[slot], g_ref[...]).astype(BF16)
    for c in range(width // ATT_PROJ_CHUNK):
        sl = slice(c * ATT_PROJ_CHUNK, (c + 1) * ATT_PROJ_CHUNK)
        val = _att_proj_chunk(xn, w_ref, qg_ref, kg_ref, c, width).astype(o_ref.dtype)
        for r in range(len(picks)):
            o_ref[0, r, :, sl] = val[r * per:(r + 1) * per]


def _residue_picks(dil):
    if dil % 8 == 0:
        return dil, [[r] for r in range(dil)]
    assert 8 % dil == 0 and dil > 1
    return 8, [[r + dil * e for e in range(8 // dil)] for r in range(dil)]


def _att_proj_dil(x, nb, length, dil, gain, w, q_gain, k_gain):
    sub, picks = _residue_picks(dil)
    width = w.shape[1]
    rows = ATT_SPAN // 2
    per = rows * len(picks[0])
    ls = length // dil
    steps_per_seq = ls // per
    return pl.pallas_call(
        functools.partial(_att_proj_dil_kernel, picks=picks, rows=rows),
        out_shape=jax.ShapeDtypeStruct((nb, dil, ls, width), BF16),
        grid=(nb * steps_per_seq,),
        in_specs=[
            pl.BlockSpec(memory_space=pl.ANY),
            _resident((1, D_MODEL)),
            _resident((D_MODEL, width)),
            _resident((1, ATT_HD)),
            _resident((1, ATT_HD)),
        ],
        out_specs=pl.BlockSpec((1, dil, per, width), lambda i: (i // steps_per_seq, 0, i % steps_per_seq, 0)),
        scratch_shapes=[pltpu.VMEM((2, dil * per, D_MODEL), F32), pltpu.SemaphoreType.DMA((2,))],
        compiler_params=_params("arbitrary"),
        name="att_proj_dil",
    )(x.reshape(x.shape[0] // sub, sub, D_MODEL), gain, w, q_gain, k_gain)


def _att_proj(x, row_off, rows, tm, gain, w, q_gain, k_gain, out_dtype):
    width = w.shape[1]
    return pl.pallas_call(
        _att_proj_kernel,
        out_shape=jax.ShapeDtypeStruct((rows, width), out_dtype),
        grid=(rows // tm,),
        in_specs=[
            pl.BlockSpec((tm, D_MODEL), lambda i: (i + row_off, 0)),
            _resident((1, D_MODEL)),
            _resident((D_MODEL, width)),
            _resident((1, ATT_HD)),
            _resident((1, ATT_HD)),
        ],
        out_specs=pl.BlockSpec((tm, width), lambda i: (i, 0)),
        compiler_params=_params("parallel"),
        name="att_proj",
    )(x, gain, w, q_gain, k_gain)


def _att_band_kernel(q_ref, kp_ref, kc_ref, vp_ref, vc_ref, bias_ref, o_ref, *scratch, picks, blocks):
    n = pl.program_id(1)
    res = q_ref.shape[1]
    col = lax.broadcasted_iota(jnp.int32, (ATT_SPAN, 2 * ATT_SPAN), 1)
    no_prev = jnp.logical_and(col < ATT_SPAN, n == 0)
    lane = lax.broadcasted_iota(jnp.int32, (ATT_SPAN, LANES), 1)

    def unit(rr, j):
        lse_tile = jnp.zeros((ATT_SPAN, LANES), F32)
        heads = []
        for h in range(ATT_G):
            sl = slice(h * ATT_HD, (h + 1) * ATT_HD)
            if j == 0:
                kcat = jnp.concatenate([kp_ref[0, rr, :, sl], kc_ref[0, rr, 0:ATT_SPAN, sl]], axis=0)
                vcat = jnp.concatenate([vp_ref[0, rr, :, sl], vc_ref[0, rr, 0:ATT_SPAN, sl]], axis=0)
            else:
                kcat = kc_ref[0, rr, (j - 1) * ATT_SPAN:(j + 1) * ATT_SPAN, sl]
                vcat = vc_ref[0, rr, (j - 1) * ATT_SPAN:(j + 1) * ATT_SPAN, sl]
            s = _dot_nt(q_ref[0, rr, j * ATT_SPAN:(j + 1) * ATT_SPAN, sl], kcat) + bias_ref[h]
            if j == 0:
                s = jnp.where(no_prev, NEG, s)
            m = jnp.max(s, axis=-1, keepdims=True)
            p = jnp.exp(s - m)
            den = jnp.sum(p, axis=-1, keepdims=True)
            heads.append(_dot(p.astype(BF16), vcat) / den)
            lse_tile = jnp.where(lane == h, m + jnp.log(den), lse_tile)
        return jnp.concatenate(heads + [lse_tile], axis=1)

    if picks is None:
        for j in range(blocks):
            o_ref[0, j * ATT_SPAN:(j + 1) * ATT_SPAN, :] = unit(0, j)
        return

    obuf, sem = scratch
    b, g = pl.program_id(0), pl.program_id(2)
    steps = pl.num_programs(0) * pl.num_programs(1) * pl.num_programs(2)
    flat = (b * pl.num_programs(1) + n) * pl.num_programs(2) + g
    slot = flat % 2
    per = ATT_SPAN // len(picks[0])

    def writeback(sl, action):
        for rr in range(res):
            for e, mid in enumerate(picks[0]):
                cp = pltpu.make_async_copy(obuf.at[sl, rr, pl.ds(e * per, per)],
                                           o_ref.at[b, pl.ds(n * per, per), g * res + rr + mid], sem.at[sl])
                getattr(cp, action)()

    @pl.when(flat >= 2)
    def _():
        writeback(slot, "wait")

    for rr in range(res):
        obuf[slot, rr] = unit(rr, 0)
    writeback(slot, "start")

    @pl.when(flat == steps - 1)
    def _():
        writeback(slot, "wait")

        @pl.when(steps >= 2)
        def _():
            writeback(1 - slot, "wait")


def _att_band(qkv, dil, bias):
    nb, _, ls, _ = qkv.shape
    units = ATT_BAND_UNITS
    if dil == 1:
        picks, res, blocks = None, 1, units
        out_shape = (nb, ls, OUT_LSE)
        out_spec = pl.BlockSpec((1, blocks * ATT_SPAN, OUT_LSE), lambda b, n, g: (b, n, 0))
        scratch = []
    else:
        sub, picks = _residue_picks(dil)
        res, blocks = units, 1
        out_shape = (nb, ls * dil // sub, sub, OUT_LSE)
        out_spec = pl.BlockSpec(memory_space=pl.ANY)
        scratch = [pltpu.VMEM((2, res, ATT_SPAN, OUT_LSE), F32), pltpu.SemaphoreType.DMA((2,))]
    cur = lambda cb: pl.BlockSpec((1, res, blocks * ATT_SPAN, ATT_W), lambda b, n, g: (b, g, n, cb))
    prev = lambda cb: pl.BlockSpec((1, res, ATT_SPAN, ATT_W), lambda b, n, g: (b, g, jnp.maximum(n * blocks - 1, 0), cb))
    out = pl.pallas_call(
        functools.partial(_att_band_kernel, picks=picks, blocks=blocks),
        out_shape=jax.ShapeDtypeStruct(out_shape, F32),
        grid=(nb, ls // (blocks * ATT_SPAN), dil // res),
        in_specs=[cur(0), prev(1), cur(1), prev(2), cur(2), _resident((ATT_G, ATT_SPAN, 2 * ATT_SPAN))],
        out_specs=out_spec,
        scratch_shapes=scratch,
        compiler_params=_params("arbitrary", "arbitrary", "arbitrary"),
        name="att_band",
    )(qkv, qkv, qkv, qkv, qkv, bias)
    return out.reshape(nb * ls * dil, OUT_LSE)


def _window_rows_kernel(k_ref, v_ref, o_ref):
    halves = o_ref.shape[2]
    per = o_ref.shape[1]
    for e in range(halves):
        for kv, ref in enumerate((k_ref, v_ref)):
            for h in range(ATT_G):
                o_ref[0, :, e, 0, kv, h, :] = ref[0, 0, e * per:(e + 1) * per, h * ATT_HD:(h + 1) * ATT_HD].astype(o_ref.dtype)


def _window_rows(qkv, dil, dtype):
    nb, _, ls, _ = qkv.shape
    halves = 1 if dil == 1 else len(_residue_picks(dil)[1][0])
    per = ATT_SPAN // halves
    last = ls // ATT_SPAN - 1
    out = pl.pallas_call(
        _window_rows_kernel,
        out_shape=jax.ShapeDtypeStruct((nb, per, halves, dil, 2, ATT_G, ATT_HD), dtype),
        grid=(nb, dil),
        in_specs=[pl.BlockSpec((1, 1, ATT_SPAN, ATT_W), lambda b, r: (b, r, last, 1)),
                  pl.BlockSpec((1, 1, ATT_SPAN, ATT_W), lambda b, r: (b, r, last, 2))],
        out_specs=pl.BlockSpec((1, per, halves, 1, 2, ATT_G, ATT_HD), lambda b, r: (b, 0, 0, r, 0, 0, 0)),
        compiler_params=_params("parallel", "parallel"),
        name="window_rows",
    )(qkv, qkv)
    return out.reshape(nb, ATT_SPAN * dil, 2, ATT_G, ATT_HD)


def _att_merge_kernel(o0_ref, o1_ref, o2_ref, w_ref, x_ref, tail_ref, *rest):
    *route_refs, out_ref, b_ref = rest

    def rows_fn():
        acc = x_ref[...]
        refs = (o0_ref, o1_ref, o2_ref)
        for h in range(ATT_G):
            sl = slice(h * ATT_HD, (h + 1) * ATT_HD)
            lses = [r[:, ATT_W + h:ATT_W + h + 1] for r in refs]
            m = jnp.maximum(jnp.maximum(lses[0], lses[1]), lses[2])
            es = [jnp.exp(l - m) for l in lses]
            merged = (es[0] * o0_ref[:, sl] + es[1] * o1_ref[:, sl] + es[2] * o2_ref[:, sl]) / (es[0] + es[1] + es[2])
            acc = acc + _dot(merged.astype(BF16), w_ref[sl, :])
        return acc

    _with_tail_and_buckets(rows_fn, tail_ref, route_refs, out_ref, b_ref)


def _att_merge(outs, w, x, rows, tm, tail, route):
    nblk = rows // tm
    row_spec = lambda width: pl.BlockSpec((tm, width), lambda i: (jnp.minimum(i, nblk - 1), 0))
    return pl.pallas_call(
        _att_merge_kernel,
        out_shape=(jax.ShapeDtypeStruct(((nblk + 1) * tm, D_MODEL), F32),
                   jax.ShapeDtypeStruct(((nblk + 1) * tm, LANES), jnp.int32)),
        grid=(nblk + 1,),
        in_specs=[row_spec(OUT_LSE), row_spec(OUT_LSE), row_spec(OUT_LSE), _resident((ATT_W, D_MODEL)),
                  row_spec(D_MODEL), _resident(tail.shape)] + _route_specs(),
        out_specs=(pl.BlockSpec((tm, D_MODEL), lambda i: (i, 0)), pl.BlockSpec((tm, LANES), lambda i: (i, 0))),
        compiler_params=_params("arbitrary"),
        name="att_merge",
    )(*outs, w, x, tail, *route)


def _t5_bucket(dist):
    max_exact = N_BUCKETS // 2
    d32 = jnp.maximum(dist, 1).astype(F32)
    large = max_exact + (jnp.log(d32 / max_exact) / jnp.log(MAX_DISTANCE / max_exact)
                         * (N_BUCKETS - max_exact)).astype(jnp.int32)
    return jnp.where(dist < max_exact, dist, jnp.minimum(large, N_BUCKETS - 1))


def _bucket_rows(tab, dist):
    onehot = jax.nn.one_hot(_t5_bucket(dist), N_BUCKETS, dtype=F32)
    return jnp.dot(onehot, tab, precision=lax.Precision.HIGHEST)


def _band_bias(tab, dil, halves):
    pos = jnp.arange(ATT_SPAN).reshape(ATT_SPAN // halves, halves).T.reshape(-1)
    a = pos[:, None]
    c = jnp.concatenate([pos, ATT_SPAN + pos])[None, :]
    rel = a - c + ATT_SPAN
    bias = jnp.moveaxis(_bucket_rows(tab, jnp.clip(rel, 0, ATT_SPAN) * dil), -1, 0)
    return jnp.where(((rel >= 0) & (rel <= ATT_SPAN))[None], bias, NEG)


def _att_sample_kernel(qkv_ref, c0_ref, c1_ref, c2_ref, b0_ref, b1_ref, b2_ref, o_ref, *, steps):
    for t in range(steps):
        outs, lses = [], []
        for g, (c_ref, b_ref) in enumerate(((c0_ref, b0_ref), (c1_ref, b1_ref), (c2_ref, b2_ref))):
            dil = ATT_GROUPS[g][1]
            q = qkv_ref[0, t, 0, g]
            if dil == 1:
                keys = jnp.concatenate([c_ref[0, t:, 0, 0], qkv_ref[0, :t + 1, 1, g]], axis=0)
                vals = jnp.concatenate([c_ref[0, t:, 0, 1], qkv_ref[0, :t + 1, 2, g]], axis=0)
            else:
                keys = jnp.concatenate([c_ref[0, :, t, 0], qkv_ref[0, t:t + 1, 1, g]], axis=0)
                vals = jnp.concatenate([c_ref[0, :, t, 1], qkv_ref[0, t:t + 1, 2, g]], axis=0)
            s = jnp.sum(keys * q[None], axis=-1, keepdims=True) + b_ref[...]
            m = jnp.max(s, axis=0)
            p = jnp.exp(s - m[None])
            den = jnp.sum(p, axis=0)
            outs.append(jnp.sum(p * vals, axis=0) / den)
            lses.append(m + jnp.log(den))
        m = jnp.maximum(jnp.maximum(lses[0], lses[1]), lses[2])
        es = [jnp.exp(l - m) for l in lses]
        o_ref[0, t] = (es[0] * outs[0] + es[1] * outs[1] + es[2] * outs[2]) / (es[0] + es[1] + es[2])


def _att_sample(qkv, caches, biases):
    nb, steps = qkv.shape[:2]
    assert steps <= min(d for _, d in ATT_GROUPS[1:]) and all(c.shape[1] == w for c, (w, _) in zip(caches, ATT_GROUPS))
    views, view_specs = [], []
    for c, (win, dil) in zip(caches, ATT_GROUPS):
        views.append(c.reshape(nb, ATT_SPAN, dil, 2, ATT_G, ATT_HD))
        view_specs.append(pl.BlockSpec((1, ATT_SPAN, min(dil, steps), 2, ATT_G, ATT_HD), lambda b: (b, 0, 0, 0, 0, 0)))
    return pl.pallas_call(
        functools.partial(_att_sample_kernel, steps=steps),
        out_shape=jax.ShapeDtypeStruct((nb, steps, ATT_G, ATT_HD), F32),
        grid=(nb,),
        in_specs=[pl.BlockSpec((1, steps, 3, ATT_NG, ATT_G, ATT_HD), lambda b: (b, 0, 0, 0, 0, 0)),
                  *view_specs,
                  *[_resident((ATT_SPAN + 1, ATT_G, LANES))] * ATT_NG],
        out_specs=pl.BlockSpec((1, steps, ATT_G, ATT_HD), lambda b: (b, 0, 0, 0)),
        compiler_params=_params("parallel"),
        name="att_sample",
    )(qkv, *views, *biases)


def _window_shift_kernel(new_ref, old_hbm, out_hbm, buf, rsem, wsem, *, steps, chunks):
    s = pl.program_id(0)
    total = pl.num_programs(0)
    depth, rows = buf.shape[:2]

    def read(step, action):
        b, c, sl = step // chunks, step % chunks, step % depth

        @pl.when(c < chunks - 1)
        def _():
            getattr(pltpu.make_async_copy(old_hbm.at[b, pl.ds(c * rows + steps, rows)], buf.at[sl], rsem.at[sl]), action)()

        @pl.when(c == chunks - 1)
        def _():
            getattr(pltpu.make_async_copy(old_hbm.at[b, pl.ds(c * rows + steps, rows - steps)],
                                          buf.at[sl, pl.ds(0, rows - steps)], rsem.at[sl]), action)()

    def write(step, action):
        b, c, sl = step // chunks, step % chunks, step % depth
        getattr(pltpu.make_async_copy(buf.at[sl], out_hbm.at[b, pl.ds(c * rows, rows)], wsem.at[sl]), action)()

    @pl.when(s == 0)
    def _():
        for ahead in range(depth - 1):
            @pl.when(ahead < total)
            def _():
                read(ahead, "start")

    @pl.when(s >= 1)
    def _():
        write(s - 1, "wait")

    @pl.when(s + depth - 1 < total)
    def _():
        read(s + depth - 1, "start")

    read(s, "wait")

    @pl.when(s % chunks == chunks - 1)
    def _():
        buf[s % depth, rows - steps:rows] = new_ref[0]

    write(s, "start")

    @pl.when(s == total - 1)
    def _():
        write(s, "wait")


def _window_shift(old, new):
    nb, win = old.shape[:2]
    steps = new.shape[1]
    rows = min(win, WINDOW_SHIFT_ROWS)
    chunks = win // rows
    tail = old.shape[2:]
    any_spec = pl.BlockSpec(memory_space=pl.ANY)
    return pl.pallas_call(
        functools.partial(_window_shift_kernel, steps=steps, chunks=chunks),
        out_shape=jax.ShapeDtypeStruct(old.shape, old.dtype),
        grid=(nb * chunks,),
        in_specs=[pl.BlockSpec((1, steps) + tail, lambda s: (s // chunks, 0, 0, 0, 0)), any_spec],
        out_specs=any_spec,
        scratch_shapes=[pltpu.VMEM((WINDOW_SHIFT_DEPTH, rows) + tail, old.dtype),
                        pltpu.SemaphoreType.DMA((WINDOW_SHIFT_DEPTH,)), pltpu.SemaphoreType.DMA((WINDOW_SHIFT_DEPTH,))],
        compiler_params=_params("arbitrary"),
        name="window_shift",
    )(new, old)


def _router_logits(xn, wr_ref, br_ref):
    xh = xn.astype(BF16)
    xl = (xn - xh.astype(F32)).astype(BF16)
    return _dot(xh, wr_ref[0]) + (_dot(xh, wr_ref[1]) + _dot(xl, wr_ref[0])) + br_ref[...]


def _route(logits, grp):
    lane = lax.broadcasted_iota(jnp.int32, logits.shape, 1)
    is_group = lane < MOE_GROUPS
    mx = jnp.max(jnp.where(is_group, logits, NEG), axis=-1, keepdims=True)
    if grp is None:
        grp = jnp.min(jnp.where(jnp.logical_and(is_group, logits == mx), lane, LANES), axis=-1, keepdims=True)
    den = jnp.sum(jnp.where(is_group, jnp.exp(logits - mx), 0.0), axis=-1, keepdims=True)
    sel = jnp.sum(jnp.where(lane == grp, logits, 0.0), axis=-1, keepdims=True)
    p_group = jnp.exp(sel - mx) / den
    lo = MOE_GROUPS + MOE_PER_GROUP * grp
    le = jnp.where(jnp.logical_and(lane >= lo, lane < lo + MOE_PER_GROUP), logits, NEG)
    v1 = jnp.max(le, axis=-1, keepdims=True)
    i1 = jnp.min(jnp.where(le == v1, lane, LANES), axis=-1, keepdims=True)
    le = jnp.where(lane == i1, NEG, le)
    v2 = jnp.max(le, axis=-1, keepdims=True)
    i2 = jnp.min(jnp.where(le == v2, lane, LANES), axis=-1, keepdims=True)
    e2 = jnp.exp(v2 - v1)
    return grp, i1, i2, p_group / (1.0 + e2), p_group * e2 / (1.0 + e2)


MOE_PAIRS = MOE_PER_GROUP * (MOE_PER_GROUP - 1) // 2
MOE_BUCKETS = MOE_GROUPS * MOE_PAIRS


def _bucket_ids(x, g_ref, wr_ref, br_ref):
    logits = _router_logits(_rmsnorm_rows(x, g_ref[...]), wr_ref, br_ref)
    grp, i1, i2, _, _ = _route(logits, None)
    lo = MOE_GROUPS + MOE_PER_GROUP * grp
    a = jnp.minimum(i1, i2) - lo
    b = jnp.maximum(i1, i2) - lo
    pair = ((a * (2 * MOE_PER_GROUP - 1 - a)) >> 1) + b - a - 1
    return jnp.broadcast_to(grp * MOE_PAIRS + pair, logits.shape)


def _moe_dispatch_kernel(slot_ref, zt_ref, x_ref, xs_hbm, zero_ref, sem, zsem, sbuf, ssem, *, n_rows, tile):
    i = pl.program_id(0)
    tm = x_ref.shape[0]
    count = jnp.minimum(tm, n_rows - i * tm)
    n_buckets = zt_ref.shape[0] - 2

    @pl.when(i == 0)
    def _():
        zero_ref[...] = jnp.zeros(zero_ref.shape, zero_ref.dtype)
        parts = tile // zero_ref.shape[0]

        def zero_tile(t, action):
            for part in range(parts):
                row = pl.multiple_of(t * tile + part * zero_ref.shape[0], zero_ref.shape[0])
                getattr(pltpu.make_async_copy(zero_ref, xs_hbm.at[pl.ds(row, zero_ref.shape[0])], zsem), action)()

        for action in ("start", "wait"):
            for b in range(n_buckets):
                @pl.when(zt_ref[b] >= 0)
                def _():
                    zero_tile(zt_ref[b], action)

            def unused(t, carry):
                zero_tile(t, action)
                return carry
            lax.fori_loop(zt_ref[n_buckets], zt_ref[n_buckets + 1], unused, 0)

    last = pl.num_programs(0) - 1

    def wait_block(par):
        pltpu.make_async_copy(sbuf.at[par], xs_hbm.at[pl.ds(0, tm)], ssem.at[par]).wait()

    for par in range(2):
        @pl.when(jnp.logical_and(i >= 2, i % 2 == par))
        def _():
            wait_block(par)

        @pl.when(jnp.logical_and(count == tm, i % 2 == par))
        def _():
            sbuf[par] = x_ref[...]
            for k in range(tm):
                pltpu.make_async_copy(sbuf.at[par, pl.ds(k, 1)], xs_hbm.at[pl.ds(slot_ref[i * tm + k], 1)],
                                      ssem.at[par]).start()

    def start(k, carry):
        pltpu.make_async_copy(x_ref.at[pl.ds(k, 1)], xs_hbm.at[pl.ds(slot_ref[i * tm + k], 1)], sem).start()
        return carry

    def wait(k, carry):
        pltpu.make_async_copy(x_ref.at[pl.ds(0, 1)], xs_hbm.at[pl.ds(0, 1)], sem).wait()
        return carry

    @pl.when(count < tm)
    def _():
        lax.fori_loop(0, count, start, 0)
        lax.fori_loop(0, count, wait, 0)

    for par in range(2):
        @pl.when(jnp.logical_and(i == last, jnp.logical_and(i >= 1, (i - 1) % 2 == par)))
        def _():
            wait_block(par)

        @pl.when(jnp.logical_and(i == last, jnp.logical_and(count == tm, i % 2 == par)))
        def _():
            wait_block(par)


def _moe_dispatch(x, slot_of_row, zero_tiles, n_rows, n_slots, tm, tile):
    return pl.pallas_call(
        functools.partial(_moe_dispatch_kernel, n_rows=n_rows, tile=tile),
        out_shape=jax.ShapeDtypeStruct((n_slots, D_MODEL), F32),
        grid_spec=pltpu.PrefetchScalarGridSpec(
            num_scalar_prefetch=2,
            grid=(pl.cdiv(n_rows, tm),),
            in_specs=[pl.BlockSpec((tm, D_MODEL), lambda i, s, p: (i, 0))],
            out_specs=pl.BlockSpec(memory_space=pl.ANY),
            scratch_shapes=[pltpu.VMEM((ZERO_ROWS, D_MODEL), F32), pltpu.SemaphoreType.DMA(()),
                            pltpu.SemaphoreType.DMA(()), pltpu.VMEM((2, tm, D_MODEL), F32),
                            pltpu.SemaphoreType.DMA((2,))]),
        compiler_params=_params("arbitrary"),
        name="moe_dispatch",
    )(slot_of_row, zero_tiles, x)


def _moe_collect_kernel(slot_ref, ys_hbm, o_ref, gbuf, sem, *, row_off):
    i = pl.program_id(0)
    tm = o_ref.shape[0]

    def issue(step, par):
        for k in range(tm):
            pltpu.make_async_copy(ys_hbm.at[pl.ds(slot_ref[row_off + step * tm + k], 1)], gbuf.at[par, pl.ds(k, 1)],
                                  sem.at[par]).start()

    @pl.when(i == 0)
    def _():
        issue(0, 0)

    for par in range(2):
        @pl.when(jnp.logical_and(i + 1 < pl.num_programs(0), (i + 1) % 2 == par))
        def _():
            issue(i + 1, par)

    for par in range(2):
        @pl.when(i % 2 == par)
        def _():
            pltpu.make_async_copy(ys_hbm.at[pl.ds(0, tm)], gbuf.at[par], sem.at[par]).wait()
            o_ref[...] = gbuf[par]


def _moe_collect(ys, slot_of_row, row_off, rows, tm):
    return pl.pallas_call(
        functools.partial(_moe_collect_kernel, row_off=row_off),
        out_shape=jax.ShapeDtypeStruct((rows, D_MODEL), F32),
        grid_spec=pltpu.PrefetchScalarGridSpec(
            num_scalar_prefetch=1,
            grid=(rows // tm,),
            in_specs=[pl.BlockSpec(memory_space=pl.ANY)],
            out_specs=pl.BlockSpec((tm, D_MODEL), lambda i, s: (i, 0)),
            scratch_shapes=[pltpu.VMEM((2, tm, D_MODEL), F32), pltpu.SemaphoreType.DMA((2,))]),
        compiler_params=_params("arbitrary"),
        name="moe_collect",
    )(slot_of_row, ys)


def _moe_expert_kernel(ea_ref, eb_ref, used_ref, x_ref, gain_ref, wr_ref, br_ref,
                       wga_ref, wua_ref, wda_ref, wgb_ref, wub_ref, wdb_ref, y_ref):
    t = pl.program_id(0)

    @pl.when(t >= used_ref[0])
    def _():
        y_ref[...] = jnp.zeros(y_ref.shape, y_ref.dtype)

    @pl.when(t < used_ref[0])
    def _():
        x = x_ref[...]
        xn = _rmsnorm_rows(x, gain_ref[...])
        _, i1, i2, w1, w2 = _route(_router_logits(xn, wr_ref, br_ref), ea_ref[t] // MOE_PER_GROUP)
        xb = xn.astype(BF16)
        acc = x
        for e_ref, wg_ref, wu_ref, wd_ref in ((ea_ref, wga_ref, wua_ref, wda_ref), (eb_ref, wgb_ref, wub_ref, wdb_ref)):
            e_lane = MOE_GROUPS + e_ref[t]
            ce = jnp.where(i1 == e_lane, w1, 0.0) + jnp.where(i2 == e_lane, w2, 0.0)
            hg = _dot(xb, wg_ref[0, 0].astype(BF16))
            hid = hg * jax.nn.sigmoid(hg) * _dot(xb, wu_ref[0, 0].astype(BF16))
            acc = acc + ce * _dot(hid.astype(BF16), wd_ref[0, 0].astype(BF16))
        y_ref[...] = acc


def _moe_plan(bucket, tile, n_tiles):
    onehot = (bucket[:, None] == jnp.arange(MOE_BUCKETS)[None, :]).astype(jnp.int32)
    csum = jnp.cumsum(onehot, axis=0)
    counts = csum[-1]
    tiles_per = (counts + tile - 1) // tile
    tile_end = jnp.cumsum(tiles_per)
    tile_start = tile_end - tiles_per
    slot_of_row = jnp.sum(onehot * (tile_start[None, :] * tile + csum - 1), axis=1)
    tile_bucket = jnp.minimum(jnp.sum(jnp.arange(n_tiles)[:, None] >= tile_end[None, :], axis=1), MOE_BUCKETS - 1)
    pair = tile_bucket % MOE_PAIRS
    first = jnp.sum(pair[:, None] >= jnp.cumsum(jnp.arange(MOE_PER_GROUP - 1, 0, -1))[None, :], axis=1)
    second = pair - ((first * (2 * MOE_PER_GROUP - 1 - first)) >> 1) + first + 1
    base = (tile_bucket // MOE_PAIRS) * MOE_PER_GROUP
    used = tile_end[-1:]
    zero_tiles = jnp.concatenate([jnp.where(tiles_per > 0, tile_end - 1, -1), used, jnp.full((1,), n_tiles)])
    i32 = lambda v: v.astype(jnp.int32)
    return i32(slot_of_row), i32(base + first), i32(base + second), i32(used), i32(zero_tiles)


def _moe(x, buckets, n_prompt, n_sample, layer, gain, wr_pair, br, wg, wu, wd):
    n = n_prompt + n_sample
    tile = MOE_TILE
    n_tiles = n // tile + MOE_BUCKETS
    slot_of_row, expert_a, expert_b, n_used, zero_tiles = _moe_plan(buckets[:n, 0], tile, n_tiles)
    xs = _moe_dispatch(x, slot_of_row, zero_tiles, n, n_tiles * tile, ROW_TILE, tile)
    w_spec = lambda shape, which: pl.BlockSpec((1, 1) + shape, lambda t, ea, eb, u: (layer, (ea, eb)[which][t], 0, 0))
    const = lambda shape: pl.BlockSpec(shape, lambda t, ea, eb, u: (0,) * len(shape), pipeline_mode=pl.Buffered(1))
    w_specs = [w_spec(shape, which) for which in range(2)
               for shape in ((D_MODEL, MOE_HIDDEN), (D_MODEL, MOE_HIDDEN), (MOE_HIDDEN, D_MODEL))]
    ys = pl.pallas_call(
        _moe_expert_kernel,
        out_shape=jax.ShapeDtypeStruct((n_tiles * tile, D_MODEL), F32),
        grid_spec=pltpu.PrefetchScalarGridSpec(
            num_scalar_prefetch=3,
            grid=(n_tiles,),
            in_specs=[pl.BlockSpec((tile, D_MODEL), lambda t, ea, eb, u: (jnp.minimum(t, u[0] - 1), 0)),
                      const((1, D_MODEL)), const((2, D_MODEL, LANES)), const((1, LANES)), *w_specs],
            out_specs=pl.BlockSpec((tile, D_MODEL), lambda t, ea, eb, u: (t, 0))),
        compiler_params=_params("arbitrary"),
        name="moe_experts",
    )(expert_a, expert_b, n_used, xs, gain, wr_pair, br, wg, wu, wd, wg, wu, wd)
    return (_moe_collect(ys, slot_of_row, 0, n_prompt, ROW_TILE),
            _moe_collect(ys, slot_of_row, n_prompt, n_sample, n_sample))


def _rope_tables(pos):
    half = RET_DK // 2
    inv = 1.0 / (ROPE_BASE ** jnp.linspace(0.0, 1.0, half, dtype=F32))
    ang = jnp.repeat(pos.astype(F32)[:, None] * inv[None, :], 2, axis=-1)
    sign = jnp.where(jnp.arange(RET_DK) % 2 == 0, -1.0, 1.0).astype(F32)
    return jnp.cos(ang), jnp.sin(ang) * sign


def kernel(x_prompt, x_sample, state_ret, cache_win0, cache_win1, cache_win2, rel_bias, norm_mix, norm_ffn, w_in_ret, w_out_ret, w_in_att, q_norm, k_norm, w_out_att, w_router_group, b_router_group, w_router_expert, b_router_expert, w_gate, w_up, w_down):
    nb_p, len_p, _ = x_prompt.shape
    nb_s, len_s, _ = x_sample.shape
    n_p, n_s = nb_p * len_p, nb_s * len_s
    n_all = n_p + n_s
    assert n_p % ROW_TILE == 0 and n_p % n_s == 0 and len_p % RET_CHUNK == 0 and len_s <= SAMPLE_PAD
    assert all(len_p % (ATT_SPAN * d) == 0 for _, d in ATT_GROUPS)
    caches = (cache_win0[0], cache_win1[0], cache_win2[0])

    xp = x_prompt.reshape(n_p, D_MODEL)
    xs = x_sample.reshape(n_s, D_MODEL)
    row = lambda v: v.reshape(1, -1).astype(F32)

    def moe_weights(i):
        wr = jnp.zeros((D_MODEL, LANES), F32)
        wr = wr.at[:, :MOE_GROUPS].set(w_router_group[i]).at[:, MOE_GROUPS:MOE_GROUPS + MOE_EXPERTS].set(w_router_expert[i])
        br = jnp.zeros((1, LANES), F32)
        br = br.at[0, :MOE_GROUPS].set(b_router_group[i]).at[0, MOE_GROUPS:MOE_GROUPS + MOE_EXPERTS].set(b_router_expert[i])
        wr_hi = wr.astype(BF16)
        wr_pair = jnp.stack([wr_hi, (wr - wr_hi.astype(F32)).astype(BF16)])
        return (row(norm_ffn[i]), wr_pair, br, w_gate, w_up, w_down)

    gain0 = row(norm_mix[0])
    w_in = w_in_ret[0].astype(BF16)
    w_out = w_out_ret[0].astype(BF16)
    cos_p, sin_p = _rope_tables(jnp.arange(len_p))
    pos_s = jnp.tile(PAST_LEN + jnp.arange(SAMPLE_PAD), nb_s)
    cos_s, sin_s = _rope_tables(pos_s)
    xs_pad = jnp.pad(x_sample, ((0, 0), (0, SAMPLE_PAD - len_s), (0, 0))).reshape(nb_s * SAMPLE_PAD, D_MODEL)

    by_size = sorted(range(ATT_NG), key=lambda g: -caches[g].shape[1])
    moved = {}
    zero_state = jnp.zeros((nb_p, RET_HEADS, RET_DK, RET_DV), F32)
    g_proj, g_core = by_size[0], by_size[1]
    if _can_shift_behind(caches[g_proj], n_p // ROW_TILE):
        proj_p, moved[g_proj] = _ret_proj(xp, 0, n_p, ROW_TILE, gain0, w_in, cos_p, sin_p, caches[g_proj], len_s)
    else:
        proj_p = _ret_proj(xp, 0, n_p, ROW_TILE, gain0, w_in, cos_p, sin_p)
    proj_s = _ret_proj(xs_pad, 0, nb_s * SAMPLE_PAD, nb_s * SAMPLE_PAD, gain0, w_in, cos_s, sin_s)
    if _can_shift_behind(caches[g_core], nb_p * (len_p // RET_CHUNK)):
        o_p, ret_p, moved[g_core] = _ret_core(proj_p, zero_state, nb_p, len_p // RET_CHUNK, RET_CHUNK, RET_CHUNK,
                                              window=caches[g_core], shift=len_s)
    else:
        o_p, ret_p = _ret_core(proj_p, zero_state, nb_p, len_p // RET_CHUNK, RET_CHUNK, RET_CHUNK)
    o_s, ret_s = _ret_core(proj_s, state_ret[0].astype(F32), nb_s, 1, SAMPLE_PAD, len_s)
    o_s = o_s.reshape(nb_s, SAMPLE_PAD, RET_VW)[:, :len_s].reshape(n_s, RET_VW)
    moe0 = moe_weights(0)
    x1, buckets = _mm_res(o_p, w_out, xp, 0, ROW_TILE, tail=_mm_res(o_s, w_out, xs, 0, n_s), route=moe0[:3])
    x2, x2_s = _moe(x1, buckets, n_p, n_s, 0, *moe0)

    gain1 = row(norm_mix[1])
    w_in = w_in_att[0].astype(BF16)
    w_out = w_out_att[0].astype(BF16)
    q_gain = row(q_norm[0]) * (ATT_HD ** -0.5)
    k_gain = row(k_norm[0])
    qkv_s = _att_proj(x2_s, 0, n_s, n_s, gain1, w_in, q_gain, k_gain, F32)
    qkv_s = qkv_s.reshape(nb_s, len_s, 3, ATT_NG, ATT_G, ATT_HD)

    outs, win_p, win_s, sample_bias = [], [], [], []
    for g, (win, dil) in enumerate(ATT_GROUPS):
        tab = rel_bias[:, g * ATT_G:(g + 1) * ATT_G].astype(F32)
        w_g = jnp.concatenate([w_in[:, (i * ATT_NG + g) * ATT_W:(i * ATT_NG + g + 1) * ATT_W] for i in range(3)], axis=1)
        if dil == 1:
            halves = 1
            qkv_g = _att_proj(x2, 0, n_p, ROW_TILE, gain1, w_g, q_gain, k_gain, BF16).reshape(nb_p, 1, len_p, 3 * ATT_W)
        else:
            halves = len(_residue_picks(dil)[1][0])
            qkv_g = _att_proj_dil(x2, nb_p, len_p, dil, gain1, w_g, q_gain, k_gain)
        outs.append(_att_band(qkv_g, dil, _band_bias(tab, dil, halves)))
        win_p.append(_window_rows(qkv_g, dil, caches[g].dtype)[None])
        new_rows = qkv_s[:, :, 1:, g].astype(caches[g].dtype)
        if g in moved:
            win_s.append(_window_tail(moved[g], new_rows)[None])
        else:
            win_s.append(_window_shift(caches[g], new_rows)[None])
        b = _bucket_rows(tab, (ATT_SPAN - jnp.arange(ATT_SPAN + 1)) * dil)
        sample_bias.append(jnp.broadcast_to(b[:, :, None], (ATT_SPAN + 1, ATT_G, LANES)))
    o_s = _att_sample(qkv_s, caches, sample_bias)
    x3_s = _mm_res(o_s.reshape(n_s, ATT_W).astype(BF16), w_out, x2_s, 0, n_s)
    moe1 = moe_weights(1)
    x3, buckets = _att_merge(outs, w_out, x2, n_p, ROW_TILE, x3_s, moe1[:3])
    y_p, y_s = _moe(x3, buckets, n_p, n_s, 1, *moe1)

    y_p = y_p.reshape(nb_p, len_p, D_MODEL)
    y_s = y_s.reshape(nb_s, len_s, D_MODEL)
    dt = state_ret.dtype
    return (y_p, y_s, ret_p[None].astype(dt), ret_s[None].astype(dt),
            win_p[0], win_s[0], win_p[1], win_s[1], win_p[2], win_s[2])
```

```python
import functools

import jax
import jax.numpy as jnp
from jax import lax
from jax.experimental import pallas as pl
from jax.experimental.pallas import tpu as pltpu

F32 = jnp.float32
BF16 = jnp.bfloat16

D_MODEL = 1024
PAST_LEN = 16384
RET_HEADS = 4
RET_DK = 256
RET_DV = 512
RET_QK = RET_HEADS * RET_DK
RET_VW = RET_HEADS * RET_DV
RET_IN = 2 * RET_QK + 2 * RET_VW
ROPE_BASE = 10000.0
ATT_GROUPS = ((128, 1), (512, 4), (2048, 16))
ATT_NG = len(ATT_GROUPS)
ATT_G = 8
ATT_HD = 128
ATT_W = ATT_G * ATT_HD
ATT_IN = 3 * ATT_NG * ATT_W
ATT_SPAN = 128
N_BUCKETS = 32
MAX_DISTANCE = 2048
MOE_GROUPS = 4
MOE_PER_GROUP = 4
MOE_EXPERTS = MOE_GROUPS * MOE_PER_GROUP
MOE_HIDDEN = 512
EPS = 1e-6
NEG = -1e30

LANES = 128
V7X_VMEM_LIMIT_BYTES = 56 * 1024 * 1024
ROW_TILE = 512
RET_CHUNK = 256
SAMPLE_PAD = 16
MOE_TILE = 512
WINDOW_SHIFT_ROWS = 256
WINDOW_SHIFT_DEPTH = 3
BACKGROUND_CHUNKS_PER_STEP = 4
ZERO_ROWS = 256
ATT_BAND_UNITS = 4
OUT_LSE = ATT_W + LANES


def _params(*sem):
    return pltpu.CompilerParams(dimension_semantics=sem, vmem_limit_bytes=V7X_VMEM_LIMIT_BYTES)


def _resident(shape):
    return pl.BlockSpec(shape, lambda *_: (0,) * len(shape), pipeline_mode=pl.Buffered(1))


def _rmsnorm_rows(x, gain):
    return x * lax.rsqrt(jnp.mean(x * x, axis=-1, keepdims=True) + EPS) * gain


def _dot(a, b):
    return jnp.dot(a, b, preferred_element_type=F32)


def _dot_nt(a, b):
    return lax.dot_general(a, b, (((1,), (1,)), ((), ())), preferred_element_type=F32)


def _dot_tn(a, b):
    return lax.dot_general(a, b, (((0,), (0,)), ((), ())), preferred_element_type=F32)


def _ret_proj_kernel(x_ref, g_ref, w_ref, cos_ref, sin_ref, *rest, shift=None, chunks=None):
    if shift is None:
        (o_ref,) = rest
    else:
        old_hbm, o_ref, moved_hbm, buf, rsem, wsem = rest
        background = _BackgroundShift(pl.program_id(0), pl.num_programs(0) - 1, old_hbm, moved_hbm, buf, rsem, wsem,
                                      shift, chunks)
        background.begin()
    xn = _rmsnorm_rows(x_ref[...], g_ref[...]).astype(BF16)
    cos = cos_ref[...]
    sin = sin_ref[...]
    even = (lax.broadcasted_iota(jnp.int32, cos.shape, 1) & 1) == 0
    for c in range(2 * RET_HEADS):
        sl = slice(c * RET_DK, (c + 1) * RET_DK)
        acc = _dot(xn, w_ref[:, sl])
        partner = jnp.where(even, pltpu.roll(acc, RET_DK - 1, 1), pltpu.roll(acc, 1, 1))
        r = acc * cos + partner * sin
        if c >= RET_HEADS:
            r = r * (RET_DK ** -0.5)
        o_ref[:, sl] = r.astype(o_ref.dtype)
    for c in range(2 * RET_QK // RET_DV, RET_IN // RET_DV):
        sl = slice(c * RET_DV, (c + 1) * RET_DV)
        o_ref[:, sl] = _dot(xn, w_ref[:, sl]).astype(o_ref.dtype)
    if shift is not None:
        background.end()


def _ret_proj(x, row_off, rows, tm, gain, w, cos, sin, window=None, shift=None):
    tab_blocks = cos.shape[0] // tm
    in_specs = [
        pl.BlockSpec((tm, D_MODEL), lambda i: (i + row_off, 0)),
        _resident((1, D_MODEL)),
        _resident((D_MODEL, RET_IN)),
        pl.BlockSpec((tm, RET_DK), lambda i: (i % tab_blocks, 0)),
        pl.BlockSpec((tm, RET_DK), lambda i: (i % tab_blocks, 0)),
    ]
    args = [x, gain, w, cos, sin]
    out_shape = [jax.ShapeDtypeStruct((rows, RET_IN), BF16)]
    out_specs = [pl.BlockSpec((tm, RET_IN), lambda i: (i, 0))]
    scratch, kwargs = [], {}
    if window is not None:
        in_spec, shape, out_spec, scratch, chunks = _background_shift_specs(window, rows // tm)
        in_specs.append(in_spec)
        args.append(window)
        out_shape.append(shape)
        out_specs.append(out_spec)
        kwargs = dict(shift=shift, chunks=chunks)
    out = pl.pallas_call(
        functools.partial(_ret_proj_kernel, **kwargs),
        out_shape=tuple(out_shape),
        grid=(rows // tm,),
        in_specs=in_specs,
        out_specs=tuple(out_specs),
        scratch_shapes=scratch,
        compiler_params=_params("arbitrary"),
        name="ret_proj",
    )(*args)
    return out if window is not None else out[0]


def _window_move(q, slot, old_hbm, out_hbm, buf, sem, shift, chunks, kind, action):
    b, c = q // chunks, q % chunks
    rows = buf.shape[1]
    if kind == "write":
        getattr(pltpu.make_async_copy(buf.at[slot], out_hbm.at[b, pl.ds(c * rows, rows)], sem.at[slot]), action)()
        return
    for last, n in ((False, rows), (True, rows - shift)):
        @pl.when(c == chunks - 1 if last else c != chunks - 1)
        def _():
            getattr(pltpu.make_async_copy(old_hbm.at[b, pl.ds(c * rows + shift, n)], buf.at[slot, pl.ds(0, n)],
                                          sem.at[slot]), action)()
            if last and action == "wait":
                buf[slot, rows - shift:rows] = jnp.zeros((shift,) + buf.shape[2:], buf.dtype)


class _BackgroundShift:
    def __init__(self, step, last_step, old_hbm, out_hbm, buf, rsem, wsem, shift, chunks):
        self.step, self.last_step, self.per_step = step, last_step, buf.shape[0] // 2
        self.args = (old_hbm, out_hbm, buf)
        self.rsem, self.wsem, self.shift, self.chunks = rsem, wsem, shift, chunks

    def _each(self, step, sem, kind, action):
        for j in range(self.per_step):
            slot = (step % 2) * self.per_step + j
            _window_move(step * self.per_step + j, slot, *self.args, sem, self.shift, self.chunks, kind, action)

    def begin(self):
        @pl.when(self.step >= 2)
        def _():
            self._each(self.step - 2, self.wsem, "write", "wait")
        self._each(self.step, self.rsem, "read", "start")

    def end(self):
        self._each(self.step, self.rsem, "read", "wait")
        self._each(self.step, self.wsem, "write", "start")

        @pl.when(self.step == self.last_step)
        def _():
            self._each(self.step, self.wsem, "write", "wait")

            @pl.when(self.step >= 1)
            def _():
                self._each(self.step - 1, self.wsem, "write", "wait")


def _background_chunk_rows(window, steps):
    for rows in (WINDOW_SHIFT_ROWS, WINDOW_SHIFT_ROWS // 2):
        chunks = window.shape[0] * (window.shape[1] // rows)
        if window.shape[1] % rows == 0 and chunks % steps == 0 and 1 <= chunks // steps <= BACKGROUND_CHUNKS_PER_STEP:
            return rows
    return None


def _background_shift_specs(window, steps):
    rows = _background_chunk_rows(window, steps)
    chunks = window.shape[1] // rows
    per_step = window.shape[0] * chunks // steps
    any_spec = pl.BlockSpec(memory_space=pl.ANY)
    scratch = [pltpu.VMEM((2 * per_step, rows) + window.shape[2:], window.dtype),
               pltpu.SemaphoreType.DMA((2 * per_step,)), pltpu.SemaphoreType.DMA((2 * per_step,))]
    return any_spec, jax.ShapeDtypeStruct(window.shape, window.dtype), any_spec, scratch, chunks


def _can_shift_behind(window, steps):
    return _background_chunk_rows(window, steps) is not None


def _ret_core_kernel(cdec_ref, q_ref, k_ref, v_ref, gt_ref, s0_ref, intra_ref, qdec_ref, kdec_ref, *rest,
                     shift=None, chunks=None):
    if shift is None:
        o_ref, st_ref = rest
    else:
        old_hbm, o_ref, st_ref, moved_hbm, buf, rsem, wsem = rest
        background = _BackgroundShift(pl.program_id(0) * pl.num_programs(1) + pl.program_id(1),
                                      pl.num_programs(0) * pl.num_programs(1) - 1,
                                      old_hbm, moved_hbm, buf, rsem, wsem, shift, chunks)
        background.begin()

    @pl.when(pl.program_id(1) == 0)
    def _():
        st_ref[...] = s0_ref[...]

    for h in range(RET_HEADS):
        qk = slice(h * RET_DK, (h + 1) * RET_DK)
        vv = slice(h * RET_DV, (h + 1) * RET_DV)
        qh, kh, vh = q_ref[:, qk], k_ref[:, qk], v_ref[:, vv]
        s = st_ref[0, h]
        sc = _dot_nt(qh, kh) * intra_ref[h]
        o = _dot(sc.astype(BF16), vh)
        o = o + _dot(qh, s.astype(BF16)) * jnp.concatenate([qdec_ref[h]] * (RET_DV // LANES), axis=1)
        kd = (kh.astype(F32) * jnp.concatenate([kdec_ref[h]] * (RET_DK // LANES), axis=1)).astype(BF16)
        st_ref[0, h] = s * cdec_ref[h] + _dot_tn(kd, vh)
        o = o * lax.rsqrt(jnp.mean(o * o, axis=-1, keepdims=True) + EPS)
        g = gt_ref[:, vv].astype(F32)
        o_ref[:, vv] = (o * (g * jax.nn.sigmoid(g))).astype(o_ref.dtype)

    if shift is not None:
        background.end()


def _ret_decay_tables(chunk, n_real):
    log_g = jnp.log(1.0 - 2.0 ** (-5.0 - jnp.arange(RET_HEADS, dtype=F32)))
    idx = jnp.arange(chunk, dtype=F32)
    diff = idx[:, None] - idx[None, :]
    intra = jnp.where(diff >= 0, jnp.exp(jnp.maximum(diff, 0.0)[None] * log_g[:, None, None]), 0.0)
    q_dec = jnp.exp((idx + 1.0)[None, :] * log_g[:, None])
    k_dec = jnp.where(idx[None, :] < n_real, jnp.exp((n_real - 1.0 - idx)[None, :] * log_g[:, None]), 0.0)
    c_dec = jnp.exp(n_real * log_g)
    rep = lambda t: jnp.broadcast_to(t[:, :, None], (RET_HEADS, chunk, LANES))
    return c_dec, intra, rep(q_dec), rep(k_dec)


def _ret_core(proj, s0, nseq, nchunk, chunk, n_real, window=None, shift=None):
    c_dec, intra, q_dec, k_dec = _ret_decay_tables(chunk, n_real)
    rows = nseq * nchunk * chunk
    row = lambda b, c: b * nchunk + c
    st_spec = pl.BlockSpec((1, RET_HEADS, RET_DK, RET_DV), lambda b, c: (b, 0, 0, 0))
    in_specs = [
        pl.BlockSpec(memory_space=pltpu.SMEM),
        pl.BlockSpec((chunk, RET_QK), lambda b, c: (row(b, c), 0)),
        pl.BlockSpec((chunk, RET_QK), lambda b, c: (row(b, c), 1)),
        pl.BlockSpec((chunk, RET_VW), lambda b, c: (row(b, c), 1)),
        pl.BlockSpec((chunk, RET_VW), lambda b, c: (row(b, c), 2)),
        st_spec,
        _resident((RET_HEADS, chunk, chunk)),
        _resident((RET_HEADS, chunk, LANES)),
        _resident((RET_HEADS, chunk, LANES)),
    ]
    args = [c_dec, proj, proj, proj, proj, s0, intra, q_dec, k_dec]
    out_shape = [jax.ShapeDtypeStruct((rows, RET_VW), BF16), jax.ShapeDtypeStruct((nseq, RET_HEADS, RET_DK, RET_DV), F32)]
    out_specs = [pl.BlockSpec((chunk, RET_VW), lambda b, c: (row(b, c), 0)), st_spec]
    scratch, kwargs = [], {}
    if window is not None:
        in_spec, shape, out_spec, scratch, chunks = _background_shift_specs(window, nseq * nchunk)
        in_specs.append(in_spec)
        args.append(window)
        out_shape.append(shape)
        out_specs.append(out_spec)
        kwargs = dict(shift=shift, chunks=chunks)
    return pl.pallas_call(
        functools.partial(_ret_core_kernel, **kwargs),
        out_shape=tuple(out_shape),
        grid=(nseq, nchunk),
        in_specs=in_specs,
        out_specs=tuple(out_specs),
        scratch_shapes=scratch,
        compiler_params=_params("arbitrary", "arbitrary"),
        name="ret_core",
    )(*args)


def _window_tail_kernel(new_ref, win_hbm, out_hbm, sem):
    b = pl.program_id(0)
    steps = new_ref.shape[1]
    cp = pltpu.make_async_copy(new_ref.at[0], out_hbm.at[b, pl.ds(out_hbm.shape[1] - steps, steps)], sem)
    cp.start()
    cp.wait()


def _window_tail(shifted, new):
    nb, steps = new.shape[:2]
    return pl.pallas_call(
        _window_tail_kernel,
        out_shape=jax.ShapeDtypeStruct(shifted.shape, shifted.dtype),
        grid=(nb,),
        in_specs=[pl.BlockSpec((1, steps) + new.shape[2:], lambda b: (b, 0, 0, 0, 0)), pl.BlockSpec(memory_space=pl.ANY)],
        out_specs=pl.BlockSpec(memory_space=pl.ANY),
        scratch_shapes=[pltpu.SemaphoreType.DMA(())],
        input_output_aliases={1: 0},
        compiler_params=_params("arbitrary"),
        name="window_tail",
    )(new, shifted)


def _with_tail_and_buckets(rows_fn, tail_ref, route_refs, o_ref, b_ref):
    last = pl.num_programs(0) - 1

    def emit(rows):
        o_ref[...] = rows
        b_ref[...] = _bucket_ids(rows, *route_refs)

    @pl.when(pl.program_id(0) < last)
    def _():
        emit(rows_fn())

    @pl.when(pl.program_id(0) == last)
    def _():
        pad = jnp.zeros((o_ref.shape[0] - tail_ref.shape[0], o_ref.shape[1]), o_ref.dtype)
        emit(jnp.concatenate([tail_ref[...], pad], axis=0))


def _route_specs():
    return [_resident((1, D_MODEL)), _resident((2, D_MODEL, LANES)), _resident((1, LANES))]


def _mm_res_kernel(a_ref, w_ref, x_ref, *rest):
    rows_fn = lambda: x_ref[...] + _dot(a_ref[...], w_ref[...])
    if len(rest) == 1:
        rest[0][...] = rows_fn()
    else:
        tail_ref, *route_refs, o_ref, b_ref = rest
        _with_tail_and_buckets(rows_fn, tail_ref, route_refs, o_ref, b_ref)


def _mm_res(a, w, x, x_off, tm, tail=None, route=None):
    m, k = a.shape
    nblk = m // tm
    clamp = (lambda i: jnp.minimum(i, nblk - 1)) if tail is not None else (lambda i: i)
    in_specs = [
        pl.BlockSpec((tm, k), lambda i: (clamp(i), 0)),
        _resident((k, D_MODEL)),
        pl.BlockSpec((tm, D_MODEL), lambda i: (clamp(i) + x_off, 0)),
    ]
    args = [a, w, x]
    steps = nblk
    out_shape = jax.ShapeDtypeStruct((nblk * tm, D_MODEL), F32)
    out_specs = pl.BlockSpec((tm, D_MODEL), lambda i: (i, 0))
    if tail is not None:
        steps = nblk + 1
        in_specs += [_resident(tail.shape)] + _route_specs()
        args += [tail, *route]
        out_shape = (jax.ShapeDtypeStruct((steps * tm, D_MODEL), F32), jax.ShapeDtypeStruct((steps * tm, LANES), jnp.int32))
        out_specs = (out_specs, pl.BlockSpec((tm, LANES), lambda i: (i, 0)))
    return pl.pallas_call(
        _mm_res_kernel,
        out_shape=out_shape,
        grid=(steps,),
        in_specs=in_specs,
        out_specs=out_specs,
        compiler_params=_params("arbitrary"),
        name="mm_res",
    )(*args)


ATT_PROJ_CHUNK = 4 * ATT_HD


def _att_proj_chunk(xn, w_ref, qg_ref, kg_ref, c, width):
    acc = _dot(xn, w_ref[:, c * ATT_PROJ_CHUNK:(c + 1) * ATT_PROJ_CHUNK])
    n_qk = 2 * (width // 3) // ATT_PROJ_CHUNK
    if c >= n_qk:
        return acc
    gain = qg_ref[...] if c < n_qk // 2 else kg_ref[...]
    heads = []
    for hh in range(ATT_PROJ_CHUNK // ATT_HD):
        a = acc[:, hh * ATT_HD:(hh + 1) * ATT_HD]
        heads.append(a * lax.rsqrt(jnp.mean(a * a, axis=-1, keepdims=True) + EPS) * gain)
    return jnp.concatenate(heads, axis=1)


def _att_proj_kernel(x_ref, g_ref, w_ref, qg_ref, kg_ref, *rest, shift=None, chunks=None):
    if shift is None:
        (o_ref,) = rest
    else:
        old_hbm, o_ref, moved_hbm, buf, rsem, wsem = rest
        background = _BackgroundShift(pl.program_id(0), pl.num_programs(0) - 1, old_hbm, moved_hbm, buf, rsem, wsem,
                                      shift, chunks)
        background.begin()
    xn = _rmsnorm_rows(x_ref[...], g_ref[...]).astype(BF16)
    width = o_ref.shape[1]
    for c in range(width // ATT_PROJ_CHUNK):
        sl = slice(c * ATT_PROJ_CHUNK, (c + 1) * ATT_PROJ_CHUNK)
        o_ref[:, sl] = _att_proj_chunk(xn, w_ref, qg_ref, kg_ref, c, width).astype(o_ref.dtype)
    if shift is not None:
        background.end()


def _att_proj_dil_kernel(x_hbm, g_ref, w_ref, qg_ref, kg_ref, o_ref, xbuf, sem, *, picks, rows):
    i = pl.program_id(0)
    slot = i % 2

    def fetch(step, sl, action):
        k = 0
        for mids in picks:
            for s in mids:
                cp = pltpu.make_async_copy(x_hbm.at[pl.ds(step * rows, rows), s], xbuf.at[sl, pl.ds(k * rows, rows)],
                                           sem.at[sl])
                getattr(cp, action)()
                k += 1

    @pl.when(i == 0)
    def _():
        fetch(0, 0, "start")

    @pl.when(i + 1 < pl.num_programs(0))
    def _():
        fetch(i + 1, 1 - slot, "start")

    fetch(i, slot, "wait")
    per = rows * len(picks[0])
    width = o_ref.shape[3]
    xn = _rmsnorm_rows(xbuf[slot], g_ref[...]).astype(BF16)
    for c in range(width // ATT_PROJ_CHUNK):
        sl = slice(c * ATT_PROJ_CHUNK, (c + 1) * ATT_PROJ_CHUNK)
        val = _att_proj_chunk(xn, w_ref, qg_ref, kg_ref, c, width).astype(o_ref.dtype)
        for r in range(len(picks)):
            o_ref[0, r, :, sl] = val[r * per:(r + 1) * per]


def _residue_picks(dil):
    if dil % 8 == 0:
        return dil, [[r] for r in range(dil)]
    assert 8 % dil == 0 and dil > 1
    return 8, [[r + dil * e for e in range(8 // dil)] for r in range(dil)]


def _att_proj_dil(x, nb, length, dil, gain, w, q_gain, k_gain):
    sub, picks = _residue_picks(dil)
    width = w.shape[1]
    rows = ATT_SPAN // 2
    per = rows * len(picks[0])
    ls = length // dil
    steps_per_seq = ls // per
    return pl.pallas_call(
        functools.partial(_att_proj_dil_kernel, picks=picks, rows=rows),
        out_shape=jax.ShapeDtypeStruct((nb, dil, ls, width), BF16),
        grid=(nb * steps_per_seq,),
        in_specs=[
            pl.BlockSpec(memory_space=pl.ANY),
            _resident((1, D_MODEL)),
            _resident((D_MODEL, width)),
            _resident((1, ATT_HD)),
            _resident((1, ATT_HD)),
        ],
        out_specs=pl.BlockSpec((1, dil, per, width), lambda i: (i // steps_per_seq, 0, i % steps_per_seq, 0)),
        scratch_shapes=[pltpu.VMEM((2, dil * per, D_MODEL), F32), pltpu.SemaphoreType.DMA((2,))],
        compiler_params=_params("arbitrary"),
        name="att_proj_dil",
    )(x.reshape(x.shape[0] // sub, sub, D_MODEL), gain, w, q_gain, k_gain)


def _att_proj(x, row_off, rows, tm, gain, w, q_gain, k_gain, out_dtype, window=None, shift=None):
    width = w.shape[1]
    in_specs = [
        pl.BlockSpec((tm, D_MODEL), lambda i: (i + row_off, 0)),
        _resident((1, D_MODEL)),
        _resident((D_MODEL, width)),
        _resident((1, ATT_HD)),
        _resident((1, ATT_HD)),
    ]
    args = [x, gain, w, q_gain, k_gain]
    out_shape = [jax.ShapeDtypeStruct((rows, width), out_dtype)]
    out_specs = [pl.BlockSpec((tm, width), lambda i: (i, 0))]
    scratch, kwargs = [], {}
    if window is not None:
        in_spec, shape, out_spec, scratch, chunks = _background_shift_specs(window, rows // tm)
        in_specs.append(in_spec)
        args.append(window)
        out_shape.append(shape)
        out_specs.append(out_spec)
        kwargs = dict(shift=shift, chunks=chunks)
    out = pl.pallas_call(
        functools.partial(_att_proj_kernel, **kwargs),
        out_shape=tuple(out_shape),
        grid=(rows // tm,),
        in_specs=in_specs,
        out_specs=tuple(out_specs),
        scratch_shapes=scratch,
        compiler_params=_params("arbitrary"),
        name="att_proj",
    )(*args)
    return out if window is not None else out[0]


def _att_band_kernel(q_ref, kp_ref, kc_ref, vp_ref, vc_ref, bias_ref, o_ref, *scratch, picks, blocks):
    n = pl.program_id(1)
    res = q_ref.shape[1]
    col = lax.broadcasted_iota(jnp.int32, (ATT_SPAN, 2 * ATT_SPAN), 1)
    no_prev = jnp.logical_and(col < ATT_SPAN, n == 0)
    lane = lax.broadcasted_iota(jnp.int32, (ATT_SPAN, LANES), 1)

    def unit(rr, j):
        lse_tile = jnp.zeros((ATT_SPAN, LANES), F32)
        heads = []
        for h in range(ATT_G):
            sl = slice(h * ATT_HD, (h + 1) * ATT_HD)
            if j == 0:
                kcat = jnp.concatenate([kp_ref[0, rr, :, sl], kc_ref[0, rr, 0:ATT_SPAN, sl]], axis=0)
                vcat = jnp.concatenate([vp_ref[0, rr, :, sl], vc_ref[0, rr, 0:ATT_SPAN, sl]], axis=0)
            else:
                kcat = kc_ref[0, rr, (j - 1) * ATT_SPAN:(j + 1) * ATT_SPAN, sl]
                vcat = vc_ref[0, rr, (j - 1) * ATT_SPAN:(j + 1) * ATT_SPAN, sl]
            s = _dot_nt(q_ref[0, rr, j * ATT_SPAN:(j + 1) * ATT_SPAN, sl], kcat) + bias_ref[h]
            if j == 0:
                s = jnp.where(no_prev, NEG, s)
            m = jnp.max(s, axis=-1, keepdims=True)
            p = jnp.exp(s - m)
            den = jnp.sum(p, axis=-1, keepdims=True)
            heads.append(_dot(p.astype(BF16), vcat) / den)
            lse_tile = jnp.where(lane == h, m + jnp.log(den), lse_tile)
        return jnp.concatenate(heads + [lse_tile], axis=1)

    if picks is None:
        for j in range(blocks):
            o_ref[0, j * ATT_SPAN:(j + 1) * ATT_SPAN, :] = unit(0, j)
        return

    obuf, sem = scratch
    b, g = pl.program_id(0), pl.program_id(2)
    steps = pl.num_programs(0) * pl.num_programs(1) * pl.num_programs(2)
    flat = (b * pl.num_programs(1) + n) * pl.num_programs(2) + g
    slot = flat % 2
    per = ATT_SPAN // len(picks[0])

    def writeback(sl, action):
        for rr in range(res):
            for e, mid in enumerate(picks[0]):
                cp = pltpu.make_async_copy(obuf.at[sl, rr, pl.ds(e * per, per)],
                                           o_ref.at[b, pl.ds(n * per, per), g * res + rr + mid], sem.at[sl])
                getattr(cp, action)()

    @pl.when(flat >= 2)
    def _():
        writeback(slot, "wait")

    for rr in range(res):
        obuf[slot, rr] = unit(rr, 0)
    writeback(slot, "start")

    @pl.when(flat == steps - 1)
    def _():
        writeback(slot, "wait")

        @pl.when(steps >= 2)
        def _():
            writeback(1 - slot, "wait")


def _att_band(qkv, dil, bias):
    nb, _, ls, _ = qkv.shape
    units = ATT_BAND_UNITS
    if dil == 1:
        picks, res, blocks = None, 1, units
        out_shape = (nb, ls, OUT_LSE)
        out_spec = pl.BlockSpec((1, blocks * ATT_SPAN, OUT_LSE), lambda b, n, g: (b, n, 0))
        scratch = []
    else:
        sub, picks = _residue_picks(dil)
        res, blocks = units, 1
        out_shape = (nb, ls * dil // sub, sub, OUT_LSE)
        out_spec = pl.BlockSpec(memory_space=pl.ANY)
        scratch = [pltpu.VMEM((2, res, ATT_SPAN, OUT_LSE), F32), pltpu.SemaphoreType.DMA((2,))]
    cur = lambda cb: pl.BlockSpec((1, res, blocks * ATT_SPAN, ATT_W), lambda b, n, g: (b, g, n, cb))
    prev = lambda cb: pl.BlockSpec((1, res, ATT_SPAN, ATT_W), lambda b, n, g: (b, g, jnp.maximum(n * blocks - 1, 0), cb))
    out = pl.pallas_call(
        functools.partial(_att_band_kernel, picks=picks, blocks=blocks),
        out_shape=jax.ShapeDtypeStruct(out_shape, F32),
        grid=(nb, ls // (blocks * ATT_SPAN), dil // res),
        in_specs=[cur(0), prev(1), cur(1), prev(2), cur(2), _resident((ATT_G, ATT_SPAN, 2 * ATT_SPAN))],
        out_specs=out_spec,
        scratch_shapes=scratch,
        compiler_params=_params("arbitrary", "arbitrary", "arbitrary"),
        name="att_band",
    )(qkv, qkv, qkv, qkv, qkv, bias)
    return out.reshape(nb * ls * dil, OUT_LSE)


def _window_rows_kernel(k_ref, v_ref, o_ref):
    halves = o_ref.shape[2]
    per = o_ref.shape[1]
    for e in range(halves):
        for kv, ref in enumerate((k_ref, v_ref)):
            for h in range(ATT_G):
                o_ref[0, :, e, 0, kv, h, :] = ref[0, 0, e * per:(e + 1) * per, h * ATT_HD:(h + 1) * ATT_HD].astype(o_ref.dtype)


def _window_rows(qkv, dil, dtype):
    nb, _, ls, _ = qkv.shape
    halves = 1 if dil == 1 else len(_residue_picks(dil)[1][0])
    per = ATT_SPAN // halves
    last = ls // ATT_SPAN - 1
    out = pl.pallas_call(
        _window_rows_kernel,
        out_shape=jax.ShapeDtypeStruct((nb, per, halves, dil, 2, ATT_G, ATT_HD), dtype),
        grid=(nb, dil),
        in_specs=[pl.BlockSpec((1, 1, ATT_SPAN, ATT_W), lambda b, r: (b, r, last, 1)),
                  pl.BlockSpec((1, 1, ATT_SPAN, ATT_W), lambda b, r: (b, r, last, 2))],
        out_specs=pl.BlockSpec((1, per, halves, 1, 2, ATT_G, ATT_HD), lambda b, r: (b, 0, 0, r, 0, 0, 0)),
        compiler_params=_params("parallel", "parallel"),
        name="window_rows",
    )(qkv, qkv)
    return out.reshape(nb, ATT_SPAN * dil, 2, ATT_G, ATT_HD)


def _att_merge_kernel(o0_ref, o1_ref, o2_ref, w_ref, x_ref, tail_ref, *rest):
    *route_refs, out_ref, b_ref = rest

    def rows_fn():
        acc = x_ref[...]
        refs = (o0_ref, o1_ref, o2_ref)
        for h in range(ATT_G):
            sl = slice(h * ATT_HD, (h + 1) * ATT_HD)
            lses = [r[:, ATT_W + h:ATT_W + h + 1] for r in refs]
            m = jnp.maximum(jnp.maximum(lses[0], lses[1]), lses[2])
            es = [jnp.exp(l - m) for l in lses]
            merged = (es[0] * o0_ref[:, sl] + es[1] * o1_ref[:, sl] + es[2] * o2_ref[:, sl]) / (es[0] + es[1] + es[2])
            acc = acc + _dot(merged.astype(BF16), w_ref[sl, :])
        return acc

    _with_tail_and_buckets(rows_fn, tail_ref, route_refs, out_ref, b_ref)


def _att_merge(outs, w, x, rows, tm, tail, route):
    nblk = rows // tm
    row_spec = lambda width: pl.BlockSpec((tm, width), lambda i: (jnp.minimum(i, nblk - 1), 0))
    return pl.pallas_call(
        _att_merge_kernel,
        out_shape=(jax.ShapeDtypeStruct(((nblk + 1) * tm, D_MODEL), F32),
                   jax.ShapeDtypeStruct(((nblk + 1) * tm, LANES), jnp.int32)),
        grid=(nblk + 1,),
        in_specs=[row_spec(OUT_LSE), row_spec(OUT_LSE), row_spec(OUT_LSE), _resident((ATT_W, D_MODEL)),
                  row_spec(D_MODEL), _resident(tail.shape)] + _route_specs(),
        out_specs=(pl.BlockSpec((tm, D_MODEL), lambda i: (i, 0)), pl.BlockSpec((tm, LANES), lambda i: (i, 0))),
        compiler_params=_params("arbitrary"),
        name="att_merge",
    )(*outs, w, x, tail, *route)


def _t5_bucket(dist):
    max_exact = N_BUCKETS // 2
    d32 = jnp.maximum(dist, 1).astype(F32)
    large = max_exact + (jnp.log(d32 / max_exact) / jnp.log(MAX_DISTANCE / max_exact)
                         * (N_BUCKETS - max_exact)).astype(jnp.int32)
    return jnp.where(dist < max_exact, dist, jnp.minimum(large, N_BUCKETS - 1))


def _bucket_rows(tab, dist):
    onehot = jax.nn.one_hot(_t5_bucket(dist), N_BUCKETS, dtype=F32)
    return jnp.dot(onehot, tab, precision=lax.Precision.HIGHEST)


def _band_bias(tab, dil, halves):
    pos = jnp.arange(ATT_SPAN).reshape(ATT_SPAN // halves, halves).T.reshape(-1)
    a = pos[:, None]
    c = jnp.concatenate([pos, ATT_SPAN + pos])[None, :]
    rel = a - c + ATT_SPAN
    bias = jnp.moveaxis(_bucket_rows(tab, jnp.clip(rel, 0, ATT_SPAN) * dil), -1, 0)
    return jnp.where(((rel >= 0) & (rel <= ATT_SPAN))[None], bias, NEG)


def _att_sample_kernel(qkv_ref, c0_ref, c1_ref, c2_ref, b0_ref, b1_ref, b2_ref, o_ref, *, steps):
    for t in range(steps):
        outs, lses = [], []
        for g, (c_ref, b_ref) in enumerate(((c0_ref, b0_ref), (c1_ref, b1_ref), (c2_ref, b2_ref))):
            dil = ATT_GROUPS[g][1]
            q = qkv_ref[0, t, 0, g]
            if dil == 1:
                keys = jnp.concatenate([c_ref[0, t:, 0, 0], qkv_ref[0, :t + 1, 1, g]], axis=0)
                vals = jnp.concatenate([c_ref[0, t:, 0, 1], qkv_ref[0, :t + 1, 2, g]], axis=0)
            else:
                keys = jnp.concatenate([c_ref[0, :, t, 0], qkv_ref[0, t:t + 1, 1, g]], axis=0)
                vals = jnp.concatenate([c_ref[0, :, t, 1], qkv_ref[0, t:t + 1, 2, g]], axis=0)
            s = jnp.sum(keys * q[None], axis=-1, keepdims=True) + b_ref[...]
            m = jnp.max(s, axis=0)
            p = jnp.exp(s - m[None])
            den = jnp.sum(p, axis=0)
            outs.append(jnp.sum(p * vals, axis=0) / den)
            lses.append(m + jnp.log(den))
        m = jnp.maximum(jnp.maximum(lses[0], lses[1]), lses[2])
        es = [jnp.exp(l - m) for l in lses]
        o_ref[0, t] = (es[0] * outs[0] + es[1] * outs[1] + es[2] * outs[2]) / (es[0] + es[1] + es[2])


def _att_sample(qkv, caches, biases):
    nb, steps = qkv.shape[:2]
    assert steps <= min(d for _, d in ATT_GROUPS[1:]) and all(c.shape[1] == w for c, (w, _) in zip(caches, ATT_GROUPS))
    views, view_specs = [], []
    for c, (win, dil) in zip(caches, ATT_GROUPS):
        views.append(c.reshape(nb, ATT_SPAN, dil, 2, ATT_G, ATT_HD))
        view_specs.append(pl.BlockSpec((1, ATT_SPAN, min(dil, steps), 2, ATT_G, ATT_HD), lambda b: (b, 0, 0, 0, 0, 0)))
    return pl.pallas_call(
        functools.partial(_att_sample_kernel, steps=steps),
        out_shape=jax.ShapeDtypeStruct((nb, steps, ATT_G, ATT_HD), F32),
        grid=(nb,),
        in_specs=[pl.BlockSpec((1, steps, 3, ATT_NG, ATT_G, ATT_HD), lambda b: (b, 0, 0, 0, 0, 0)),
                  *view_specs,
                  *[_resident((ATT_SPAN + 1, ATT_G, LANES))] * ATT_NG],
        out_specs=pl.BlockSpec((1, steps, ATT_G, ATT_HD), lambda b: (b, 0, 0, 0)),
        compiler_params=_params("parallel"),
        name="att_sample",
    )(qkv, *views, *biases)


def _window_shift_kernel(new_ref, old_hbm, out_hbm, buf, rsem, wsem, *, steps, chunks):
    s = pl.program_id(0)
    total = pl.num_programs(0)
    depth, rows = buf.shape[:2]

    def read(step, action):
        b, c, sl = step // chunks, step % chunks, step % depth

        @pl.when(c < chunks - 1)
        def _():
            getattr(pltpu.make_async_copy(old_hbm.at[b, pl.ds(c * rows + steps, rows)], buf.at[sl], rsem.at[sl]), action)()

        @pl.when(c == chunks - 1)
        def _():
            getattr(pltpu.make_async_copy(old_hbm.at[b, pl.ds(c * rows + steps, rows - steps)],
                                          buf.at[sl, pl.ds(0, rows - steps)], rsem.at[sl]), action)()

    def write(step, action):
        b, c, sl = step // chunks, step % chunks, step % depth
        getattr(pltpu.make_async_copy(buf.at[sl], out_hbm.at[b, pl.ds(c * rows, rows)], wsem.at[sl]), action)()

    @pl.when(s == 0)
    def _():
        for ahead in range(depth - 1):
            @pl.when(ahead < total)
            def _():
                read(ahead, "start")

    @pl.when(s >= 1)
    def _():
        write(s - 1, "wait")

    @pl.when(s + depth - 1 < total)
    def _():
        read(s + depth - 1, "start")

    read(s, "wait")

    @pl.when(s % chunks == chunks - 1)
    def _():
        buf[s % depth, rows - steps:rows] = new_ref[0]

    write(s, "start")

    @pl.when(s == total - 1)
    def _():
        write(s, "wait")


def _window_shift(old, new):
    nb, win = old.shape[:2]
    steps = new.shape[1]
    rows = min(win, WINDOW_SHIFT_ROWS)
    chunks = win // rows
    tail = old.shape[2:]
    any_spec = pl.BlockSpec(memory_space=pl.ANY)
    return pl.pallas_call(
        functools.partial(_window_shift_kernel, steps=steps, chunks=chunks),
        out_shape=jax.ShapeDtypeStruct(old.shape, old.dtype),
        grid=(nb * chunks,),
        in_specs=[pl.BlockSpec((1, steps) + tail, lambda s: (s // chunks, 0, 0, 0, 0)), any_spec],
        out_specs=any_spec,
        scratch_shapes=[pltpu.VMEM((WINDOW_SHIFT_DEPTH, rows) + tail, old.dtype),
                        pltpu.SemaphoreType.DMA((WINDOW_SHIFT_DEPTH,)), pltpu.SemaphoreType.DMA((WINDOW_SHIFT_DEPTH,))],
        compiler_params=_params("arbitrary"),
        name="window_shift",
    )(new, old)


def _router_logits(xn, wr_ref, br_ref):
    xh = xn.astype(BF16)
    xl = (xn - xh.astype(F32)).astype(BF16)
    return _dot(xh, wr_ref[0]) + (_dot(xh, wr_ref[1]) + _dot(xl, wr_ref[0])) + br_ref[...]


def _route(logits, grp):
    lane = lax.broadcasted_iota(jnp.int32, logits.shape, 1)
    is_group = lane < MOE_GROUPS
    mx = jnp.max(jnp.where(is_group, logits, NEG), axis=-1, keepdims=True)
    if grp is None:
        grp = jnp.min(jnp.where(jnp.logical_and(is_group, logits == mx), lane, LANES), axis=-1, keepdims=True)
    den = jnp.sum(jnp.where(is_group, jnp.exp(logits - mx), 0.0), axis=-1, keepdims=True)
    sel = jnp.sum(jnp.where(lane == grp, logits, 0.0), axis=-1, keepdims=True)
    p_group = jnp.exp(sel - mx) / den
    lo = MOE_GROUPS + MOE_PER_GROUP * grp
    le = jnp.where(jnp.logical_and(lane >= lo, lane < lo + MOE_PER_GROUP), logits, NEG)
    v1 = jnp.max(le, axis=-1, keepdims=True)
    i1 = jnp.min(jnp.where(le == v1, lane, LANES), axis=-1, keepdims=True)
    le = jnp.where(lane == i1, NEG, le)
    v2 = jnp.max(le, axis=-1, keepdims=True)
    i2 = jnp.min(jnp.where(le == v2, lane, LANES), axis=-1, keepdims=True)
    e2 = jnp.exp(v2 - v1)
    return grp, i1, i2, p_group / (1.0 + e2), p_group * e2 / (1.0 + e2)


MOE_PAIRS = MOE_PER_GROUP * (MOE_PER_GROUP - 1) // 2
MOE_BUCKETS = MOE_GROUPS * MOE_PAIRS


def _bucket_ids(x, g_ref, wr_ref, br_ref):
    logits = _router_logits(_rmsnorm_rows(x, g_ref[...]), wr_ref, br_ref)
    grp, i1, i2, _, _ = _route(logits, None)
    lo = MOE_GROUPS + MOE_PER_GROUP * grp
    a = jnp.minimum(i1, i2) - lo
    b = jnp.maximum(i1, i2) - lo
    pair = ((a * (2 * MOE_PER_GROUP - 1 - a)) >> 1) + b - a - 1
    return jnp.broadcast_to(grp * MOE_PAIRS + pair, logits.shape)


def _moe_dispatch_kernel(slot_ref, zt_ref, x_ref, xs_hbm, zero_ref, sem, zsem, sbuf, ssem, *, n_rows, tile):
    i = pl.program_id(0)
    tm = x_ref.shape[0]
    count = jnp.minimum(tm, n_rows - i * tm)
    n_buckets = zt_ref.shape[0] - 2

    @pl.when(i == 0)
    def _():
        zero_ref[...] = jnp.zeros(zero_ref.shape, zero_ref.dtype)
        parts = tile // zero_ref.shape[0]

        def zero_tile(t, action):
            for part in range(parts):
                row = pl.multiple_of(t * tile + part * zero_ref.shape[0], zero_ref.shape[0])
                getattr(pltpu.make_async_copy(zero_ref, xs_hbm.at[pl.ds(row, zero_ref.shape[0])], zsem), action)()

        for action in ("start", "wait"):
            for b in range(n_buckets):
                @pl.when(zt_ref[b] >= 0)
                def _():
                    zero_tile(zt_ref[b], action)

            def unused(t, carry):
                zero_tile(t, action)
                return carry
            lax.fori_loop(zt_ref[n_buckets], zt_ref[n_buckets + 1], unused, 0)

    last = pl.num_programs(0) - 1

    def wait_block(par):
        pltpu.make_async_copy(sbuf.at[par], xs_hbm.at[pl.ds(0, tm)], ssem.at[par]).wait()

    for par in range(2):
        @pl.when(jnp.logical_and(i >= 2, i % 2 == par))
        def _():
            wait_block(par)

        @pl.when(jnp.logical_and(count == tm, i % 2 == par))
        def _():
            sbuf[par] = x_ref[...]
            for k in range(tm):
                pltpu.make_async_copy(sbuf.at[par, pl.ds(k, 1)], xs_hbm.at[pl.ds(slot_ref[i * tm + k], 1)],
                                      ssem.at[par]).start()

    def start(k, carry):
        pltpu.make_async_copy(x_ref.at[pl.ds(k, 1)], xs_hbm.at[pl.ds(slot_ref[i * tm + k], 1)], sem).start()
        return carry

    def wait(k, carry):
        pltpu.make_async_copy(x_ref.at[pl.ds(0, 1)], xs_hbm.at[pl.ds(0, 1)], sem).wait()
        return carry

    @pl.when(count < tm)
    def _():
        lax.fori_loop(0, count, start, 0)
        lax.fori_loop(0, count, wait, 0)

    for par in range(2):
        @pl.when(jnp.logical_and(i == last, jnp.logical_and(i >= 1, (i - 1) % 2 == par)))
        def _():
            wait_block(par)

        @pl.when(jnp.logical_and(i == last, jnp.logical_and(count == tm, i % 2 == par)))
        def _():
            wait_block(par)


def _moe_dispatch(x, slot_of_row, zero_tiles, n_rows, n_slots, tm, tile):
    return pl.pallas_call(
        functools.partial(_moe_dispatch_kernel, n_rows=n_rows, tile=tile),
        out_shape=jax.ShapeDtypeStruct((n_slots, D_MODEL), F32),
        grid_spec=pltpu.PrefetchScalarGridSpec(
            num_scalar_prefetch=2,
            grid=(pl.cdiv(n_rows, tm),),
            in_specs=[pl.BlockSpec((tm, D_MODEL), lambda i, s, p: (i, 0))],
            out_specs=pl.BlockSpec(memory_space=pl.ANY),
            scratch_shapes=[pltpu.VMEM((ZERO_ROWS, D_MODEL), F32), pltpu.SemaphoreType.DMA(()),
                            pltpu.SemaphoreType.DMA(()), pltpu.VMEM((2, tm, D_MODEL), F32),
                            pltpu.SemaphoreType.DMA((2,))]),
        compiler_params=_params("arbitrary"),
        name="moe_dispatch",
    )(slot_of_row, zero_tiles, x)


def _moe_collect_kernel(slot_ref, ys_hbm, o_ref, gbuf, sem, *, row_off):
    i = pl.program_id(0)
    tm = o_ref.shape[0]

    def issue(step, par):
        for k in range(tm):
            pltpu.make_async_copy(ys_hbm.at[pl.ds(slot_ref[row_off + step * tm + k], 1)], gbuf.at[par, pl.ds(k, 1)],
                                  sem.at[par]).start()

    @pl.when(i == 0)
    def _():
        issue(0, 0)

    for par in range(2):
        @pl.when(jnp.logical_and(i + 1 < pl.num_programs(0), (i + 1) % 2 == par))
        def _():
            issue(i + 1, par)

    for par in range(2):
        @pl.when(i % 2 == par)
        def _():
            pltpu.make_async_copy(ys_hbm.at[pl.ds(0, tm)], gbuf.at[par], sem.at[par]).wait()
            o_ref[...] = gbuf[par]


def _moe_collect(ys, slot_of_row, row_off, rows, tm):
    return pl.pallas_call(
        functools.partial(_moe_collect_kernel, row_off=row_off),
        out_shape=jax.ShapeDtypeStruct((rows, D_MODEL), F32),
        grid_spec=pltpu.PrefetchScalarGridSpec(
            num_scalar_prefetch=1,
            grid=(rows // tm,),
            in_specs=[pl.BlockSpec(memory_space=pl.ANY)],
            out_specs=pl.BlockSpec((tm, D_MODEL), lambda i, s: (i, 0)),
            scratch_shapes=[pltpu.VMEM((2, tm, D_MODEL), F32), pltpu.SemaphoreType.DMA((2,))]),
        compiler_params=_params("arbitrary"),
        name="moe_collect",
    )(slot_of_row, ys)


def _moe_expert_kernel(ea_ref, eb_ref, used_ref, x_ref, gain_ref, wr_ref, br_ref,
                       wga_ref, wua_ref, wda_ref, wgb_ref, wub_ref, wdb_ref, y_ref):
    t = pl.program_id(0)

    @pl.when(t >= used_ref[0])
    def _():
        y_ref[...] = jnp.zeros(y_ref.shape, y_ref.dtype)

    @pl.when(t < used_ref[0])
    def _():
        x = x_ref[...]
        xn = _rmsnorm_rows(x, gain_ref[...])
        _, i1, i2, w1, w2 = _route(_router_logits(xn, wr_ref, br_ref), ea_ref[t] // MOE_PER_GROUP)
        xb = xn.astype(BF16)
        acc = x
        for e_ref, wg_ref, wu_ref, wd_ref in ((ea_ref, wga_ref, wua_ref, wda_ref), (eb_ref, wgb_ref, wub_ref, wdb_ref)):
            e_lane = MOE_GROUPS + e_ref[t]
            ce = jnp.where(i1 == e_lane, w1, 0.0) + jnp.where(i2 == e_lane, w2, 0.0)
            hg = _dot(xb, wg_ref[0, 0].astype(BF16))
            hid = hg * jax.nn.sigmoid(hg) * _dot(xb, wu_ref[0, 0].astype(BF16))
            acc = acc + ce * _dot(hid.astype(BF16), wd_ref[0, 0].astype(BF16))
        y_ref[...] = acc


def _moe_plan(bucket, tile, n_tiles):
    onehot = (bucket[:, None] == jnp.arange(MOE_BUCKETS)[None, :]).astype(jnp.int32)
    csum = jnp.cumsum(onehot, axis=0)
    counts = csum[-1]
    tiles_per = (counts + tile - 1) // tile
    tile_end = jnp.cumsum(tiles_per)
    tile_start = tile_end - tiles_per
    slot_of_row = jnp.sum(onehot * (tile_start[None, :] * tile + csum - 1), axis=1)
    tile_bucket = jnp.minimum(jnp.sum(jnp.arange(n_tiles)[:, None] >= tile_end[None, :], axis=1), MOE_BUCKETS - 1)
    pair = tile_bucket % MOE_PAIRS
    first = jnp.sum(pair[:, None] >= jnp.cumsum(jnp.arange(MOE_PER_GROUP - 1, 0, -1))[None, :], axis=1)
    second = pair - ((first * (2 * MOE_PER_GROUP - 1 - first)) >> 1) + first + 1
    base = (tile_bucket // MOE_PAIRS) * MOE_PER_GROUP
    used = tile_end[-1:]
    zero_tiles = jnp.concatenate([jnp.where(tiles_per > 0, tile_end - 1, -1), used, jnp.full((1,), n_tiles)])
    i32 = lambda v: v.astype(jnp.int32)
    return i32(slot_of_row), i32(base + first), i32(base + second), i32(used), i32(zero_tiles)


def _moe(x, buckets, n_prompt, n_sample, layer, gain, wr_pair, br, wg, wu, wd):
    n = n_prompt + n_sample
    tile = MOE_TILE
    n_tiles = n // tile + MOE_BUCKETS
    slot_of_row, expert_a, expert_b, n_used, zero_tiles = _moe_plan(buckets[:n, 0], tile, n_tiles)
    xs = _moe_dispatch(x, slot_of_row, zero_tiles, n, n_tiles * tile, ROW_TILE, tile)
    w_spec = lambda shape, which: pl.BlockSpec((1, 1) + shape, lambda t, ea, eb, u: (layer, (ea, eb)[which][t], 0, 0))
    const = lambda shape: pl.BlockSpec(shape, lambda t, ea, eb, u: (0,) * len(shape), pipeline_mode=pl.Buffered(1))
    w_specs = [w_spec(shape, which) for which in range(2)
               for shape in ((D_MODEL, MOE_HIDDEN), (D_MODEL, MOE_HIDDEN), (MOE_HIDDEN, D_MODEL))]
    ys = pl.pallas_call(
        _moe_expert_kernel,
        out_shape=jax.ShapeDtypeStruct((n_tiles * tile, D_MODEL), F32),
        grid_spec=pltpu.PrefetchScalarGridSpec(
            num_scalar_prefetch=3,
            grid=(n_tiles,),
            in_specs=[pl.BlockSpec((tile, D_MODEL), lambda t, ea, eb, u: (jnp.minimum(t, u[0] - 1), 0)),
                      const((1, D_MODEL)), const((2, D_MODEL, LANES)), const((1, LANES)), *w_specs],
            out_specs=pl.BlockSpec((tile, D_MODEL), lambda t, ea, eb, u: (t, 0))),
        compiler_params=_params("arbitrary"),
        name="moe_experts",
    )(expert_a, expert_b, n_used, xs, gain, wr_pair, br, wg, wu, wd, wg, wu, wd)
    return (_moe_collect(ys, slot_of_row, 0, n_prompt, ROW_TILE),
            _moe_collect(ys, slot_of_row, n_prompt, n_sample, n_sample))


def _rope_tables(pos):
    half = RET_DK // 2
    inv = 1.0 / (ROPE_BASE ** jnp.linspace(0.0, 1.0, half, dtype=F32))
    ang = jnp.repeat(pos.astype(F32)[:, None] * inv[None, :], 2, axis=-1)
    sign = jnp.where(jnp.arange(RET_DK) % 2 == 0, -1.0, 1.0).astype(F32)
    return jnp.cos(ang), jnp.sin(ang) * sign


def kernel(x_prompt, x_sample, state_ret, cache_win0, cache_win1, cache_win2, rel_bias, norm_mix, norm_ffn, w_in_ret, w_out_ret, w_in_att, q_norm, k_norm, w_out_att, w_router_group, b_router_group, w_router_expert, b_router_expert, w_gate, w_up, w_down):
    nb_p, len_p, _ = x_prompt.shape
    nb_s, len_s, _ = x_sample.shape
    n_p, n_s = nb_p * len_p, nb_s * len_s
    n_all = n_p + n_s
    assert n_p % ROW_TILE == 0 and n_p % n_s == 0 and len_p % RET_CHUNK == 0 and len_s <= SAMPLE_PAD
    assert all(len_p % (ATT_SPAN * d) == 0 for _, d in ATT_GROUPS)
    caches = (cache_win0[0], cache_win1[0], cache_win2[0])

    xp = x_prompt.reshape(n_p, D_MODEL)
    xs = x_sample.reshape(n_s, D_MODEL)
    row = lambda v: v.reshape(1, -1).astype(F32)

    def moe_weights(i):
        wr = jnp.zeros((D_MODEL, LANES), F32)
        wr = wr.at[:, :MOE_GROUPS].set(w_router_group[i]).at[:, MOE_GROUPS:MOE_GROUPS + MOE_EXPERTS].set(w_router_expert[i])
        br = jnp.zeros((1, LANES), F32)
        br = br.at[0, :MOE_GROUPS].set(b_router_group[i]).at[0, MOE_GROUPS:MOE_GROUPS + MOE_EXPERTS].set(b_router_expert[i])
        wr_hi = wr.astype(BF16)
        wr_pair = jnp.stack([wr_hi, (wr - wr_hi.astype(F32)).astype(BF16)])
        return (row(norm_ffn[i]), wr_pair, br, w_gate, w_up, w_down)

    gain0 = row(norm_mix[0])
    w_in = w_in_ret[0].astype(BF16)
    w_out = w_out_ret[0].astype(BF16)
    cos_p, sin_p = _rope_tables(jnp.arange(len_p))
    pos_s = jnp.tile(PAST_LEN + jnp.arange(SAMPLE_PAD), nb_s)
    cos_s, sin_s = _rope_tables(pos_s)
    xs_pad = jnp.pad(x_sample, ((0, 0), (0, SAMPLE_PAD - len_s), (0, 0))).reshape(nb_s * SAMPLE_PAD, D_MODEL)

    by_size = sorted(range(ATT_NG), key=lambda g: -caches[g].shape[1])
    moved = {}
    zero_state = jnp.zeros((nb_p, RET_HEADS, RET_DK, RET_DV), F32)
    g_proj, g_att = by_size[0], by_size[1]
    if _can_shift_behind(caches[g_proj], n_p // ROW_TILE):
        proj_p, moved[g_proj] = _ret_proj(xp, 0, n_p, ROW_TILE, gain0, w_in, cos_p, sin_p, caches[g_proj], len_s)
    else:
        proj_p = _ret_proj(xp, 0, n_p, ROW_TILE, gain0, w_in, cos_p, sin_p)
    proj_s = _ret_proj(xs_pad, 0, nb_s * SAMPLE_PAD, nb_s * SAMPLE_PAD, gain0, w_in, cos_s, sin_s)
    o_p, ret_p = _ret_core(proj_p, zero_state, nb_p, len_p // RET_CHUNK, RET_CHUNK, RET_CHUNK)
    o_s, ret_s = _ret_core(proj_s, state_ret[0].astype(F32), nb_s, 1, SAMPLE_PAD, len_s)
    o_s = o_s.reshape(nb_s, SAMPLE_PAD, RET_VW)[:, :len_s].reshape(n_s, RET_VW)
    moe0 = moe_weights(0)
    x1, buckets = _mm_res(o_p, w_out, xp, 0, ROW_TILE, tail=_mm_res(o_s, w_out, xs, 0, n_s), route=moe0[:3])
    x2, x2_s = _moe(x1, buckets, n_p, n_s, 0, *moe0)

    gain1 = row(norm_mix[1])
    w_in = w_in_att[0].astype(BF16)
    w_out = w_out_att[0].astype(BF16)
    q_gain = row(q_norm[0]) * (ATT_HD ** -0.5)
    k_gain = row(k_norm[0])
    qkv_s = _att_proj(x2_s, 0, n_s, n_s, gain1, w_in, q_gain, k_gain, F32)
    qkv_s = qkv_s.reshape(nb_s, len_s, 3, ATT_NG, ATT_G, ATT_HD)

    outs, win_p, win_s, sample_bias = [], [], [], []
    for g, (win, dil) in enumerate(ATT_GROUPS):
        tab = rel_bias[:, g * ATT_G:(g + 1) * ATT_G].astype(F32)
        w_g = jnp.concatenate([w_in[:, (i * ATT_NG + g) * ATT_W:(i * ATT_NG + g + 1) * ATT_W] for i in range(3)], axis=1)
        if dil == 1:
            halves = 1
            if _can_shift_behind(caches[g_att], n_p // ROW_TILE):
                qkv_g, moved[g_att] = _att_proj(x2, 0, n_p, ROW_TILE, gain1, w_g, q_gain, k_gain, BF16,
                                                caches[g_att], len_s)
            else:
                qkv_g = _att_proj(x2, 0, n_p, ROW_TILE, gain1, w_g, q_gain, k_gain, BF16)
            qkv_g = qkv_g.reshape(nb_p, 1, len_p, 3 * ATT_W)
        else:
            halves = len(_residue_picks(dil)[1][0])
            qkv_g = _att_proj_dil(x2, nb_p, len_p, dil, gain1, w_g, q_gain, k_gain)
        outs.append(_att_band(qkv_g, dil, _band_bias(tab, dil, halves)))
        win_p.append(_window_rows(qkv_g, dil, caches[g].dtype)[None])
        b = _bucket_rows(tab, (ATT_SPAN - jnp.arange(ATT_SPAN + 1)) * dil)
        sample_bias.append(jnp.broadcast_to(b[:, :, None], (ATT_SPAN + 1, ATT_G, LANES)))
    for g in range(ATT_NG):
        new_rows = qkv_s[:, :, 1:, g].astype(caches[g].dtype)
        win_s.append((_window_tail(moved[g], new_rows) if g in moved else _window_shift(caches[g], new_rows))[None])
    o_s = _att_sample(qkv_s, caches, sample_bias)
    x3_s = _mm_res(o_s.reshape(n_s, ATT_W).astype(BF16), w_out, x2_s, 0, n_s)
    moe1 = moe_weights(1)
    x3, buckets = _att_merge(outs, w_out, x2, n_p, ROW_TILE, x3_s, moe1[:3])
    y_p, y_s = _moe(x3, buckets, n_p, n_s, 1, *moe1)

    y_p = y_p.reshape(nb_p, len_p, D_MODEL)
    y_s = y_s.reshape(nb_s, len_s, D_MODEL)
    dt = state_ret.dtype
    return (y_p, y_s, ret_p[None].astype(dt), ret_s[None].astype(dt),
            win_p[0], win_s[0], win_p[1], win_s[1], win_p[2], win_s[2])
```
